```python
import math
import jax, jax.numpy as jnp
from jax import lax
import numpy as np

D_MODEL = 1024
BATCH = 2
SEQ = 8192
DEPTH = 1
DEC_BATCH = 1
DEC_SEQ = 16384
PAST_LEN = 128

HY_WIDTH = 512
ATT_WIDTH = 512
N_ATT_HEADS = 4
ATT_HEAD_DIM = 64
ATT_V_DIM = 2 * ATT_HEAD_DIM
ROT_DIM = ATT_HEAD_DIM // 4
ROPE_THETA = 500000.0
IN_WIDTH = 3 * HY_WIDTH + 3 * ATT_WIDTH
SHORT_CONV = 3
FILTER_BANDS = 8
FILTER_EMB = 1 + 2 * FILTER_BANDS
FILTER_HIDDEN = 64
FAST_DECAY_PCT = 0.3
SLOW_DECAY_PCT = 1.5
DECAY_TARGET = 1e-2
PEER_HEADS = 8
PEER_KEYS = 128
PEER_EXPERTS = PEER_KEYS * PEER_KEYS
PEER_TOPK = 16
PEER_HALF = 128
Q_BLOCK = 128
TOK_BLOCK = 128
EPS = 1e-6
SUBLN_EPS = 1e-5

kernel_name = "hymba_hyena_diffattn_peer_encoder"

F32 = jnp.float32


def rms_norm(x, g, eps=EPS):
    xf = x.astype(F32)
    y = xf * lax.rsqrt(jnp.mean(xf * xf, axis=-1, keepdims=True) + eps)
    return (y * g.astype(F32)).astype(x.dtype)


def short_conv(z, w, b):
    L = z.shape[1]
    zp = jnp.pad(z, ((0, 0), (1, 1), (0, 0)))
    return zp[:, :L] * w[0] + zp[:, 1:L + 1] * w[1] + zp[:, 2:] * w[2] + b


def hyena_filter(L, w1, b1, f1, w2, b2, f2, w3, b3):
    t = jnp.linspace(0.0, 1.0, L, dtype=F32)[:, None]
    w = 2.0 * math.pi * jnp.arange(L, dtype=F32) / L
    bands = jnp.linspace(1e-4, FILTER_BANDS - 1, FILTER_BANDS, dtype=F32)
    ang = w[:, None] * bands[None, :]
    feats = jnp.concatenate([t, jnp.cos(ang), -jnp.sin(ang)], axis=-1)
    h = jnp.sin(f1.astype(F32) * (feats @ w1.astype(F32) + b1.astype(F32)))
    h = jnp.sin(f2.astype(F32) * (h @ w2.astype(F32) + b2.astype(F32)))
    h = h @ w3.astype(F32) + b3.astype(F32)
    max_decay = math.log(DECAY_TARGET) / FAST_DECAY_PCT
    min_decay = math.log(DECAY_TARGET) / SLOW_DECAY_PCT
    deltas = jnp.abs(jnp.linspace(min_decay, max_decay, HY_WIDTH, dtype=F32))
    decay = jnp.exp(-t * deltas[None, :])
    h = h.reshape(L, 2, HY_WIDTH) * decay[:, None, :]
    return h[:, 0], h[:, 1]


def bidir_long_conv(u, hf, hb, skip):
    L = u.shape[1]
    k = jnp.concatenate([hf, jnp.zeros((1, HY_WIDTH), F32), hb[:0:-1]], axis=0)
    U = jnp.fft.rfft(u, n=2 * L, axis=1)
    K = jnp.fft.rfft(k, axis=0)
    y = jnp.fft.irfft(U * K[None], n=2 * L, axis=1)[:, :L]
    return y + u * skip.astype(F32)


def rope_tables(L):
    inv = ROPE_THETA ** (-jnp.arange(0, ROT_DIM, 2, dtype=F32) / ROT_DIM)
    ang = jnp.arange(L, dtype=F32)[:, None] * inv[None, :]
    return jnp.cos(ang), jnp.sin(ang)


def partial_rope(x, cos, sin):
    c = cos[None, :, None, None, :].astype(x.dtype)
    s = sin[None, :, None, None, :].astype(x.dtype)
    half = ROT_DIM // 2
    x1, x2, xp = x[..., :half], x[..., half:ROT_DIM], x[..., ROT_DIM:]
    return jnp.concatenate([x1 * c - x2 * s, x2 * c + x1 * s, xp], axis=-1)


def diff_attention(q, k, v, lam, subln_g, lambda_init):
    B, L = q.shape[0], q.shape[1]
    nb = L // Q_BLOCK
    scale = ATT_HEAD_DIM ** -0.5
    qb = q.reshape(B, nb, Q_BLOCK, N_ATT_HEADS, 2, ATT_HEAD_DIM).transpose(1, 0, 2, 3, 4, 5)

    def block(qi):
        s = jnp.einsum('bqhcd,bkhcd->bhcqk', qi, k, preferred_element_type=F32) * scale
        p = jax.nn.softmax(s, axis=-1)
        a = p[:, :, 0] - lam * p[:, :, 1]
        return jnp.einsum('bhqk,bkhe->bqhe', a.astype(v.dtype), v)

    o = lax.map(block, qb)
    o = o.transpose(1, 0, 2, 3, 4).reshape(B, L, N_ATT_HEADS, ATT_V_DIM)
    o = rms_norm(o, subln_g, SUBLN_EPS) * (1.0 - lambda_init)
    return o.reshape(B, L, ATT_WIDTH)


def peer(xn, wq, keys, u, v):
    B, L, D = xn.shape
    T = B * L
    xt = xn.reshape(T, D)
    q = (xt @ wq).reshape(T, PEER_HEADS, 2, PEER_HALF)
    s = jnp.einsum('thcd,hcnd->thcn', q, keys, preferred_element_type=F32)
    sv, si = lax.top_k(s, PEER_TOPK)
    cand = sv[:, :, 0, :, None] + sv[:, :, 1, None, :]
    cand_idx = si[:, :, 0, :, None] * PEER_KEYS + si[:, :, 1, None, :]
    kk = PEER_TOPK * PEER_TOPK
    cv, ci = lax.top_k(cand.reshape(T, PEER_HEADS, kk), PEER_TOPK)
    eidx = jnp.take_along_axis(cand_idx.reshape(T, PEER_HEADS, kk), ci, axis=-1)
    g = jax.nn.softmax(cv, axis=-1)
    nb = T // TOK_BLOCK

    def block(args):
        xb, ib, gb = args
        ub = jnp.take(u, ib, axis=0)
        vb = jnp.take(v, ib, axis=0)
        hid = jnp.einsum('td,thkd->thk', xb, ub, preferred_element_type=F32)
        a = (jax.nn.gelu(hid, approximate=False) * gb).astype(xb.dtype)
        return jnp.einsum('thk,thkd->td', a, vb)

    out = lax.map(block, (xt.reshape(nb, TOK_BLOCK, D),
                          eidx.reshape(nb, TOK_BLOCK, PEER_HEADS, PEER_TOPK),
                          g.reshape(nb, TOK_BLOCK, PEER_HEADS, PEER_TOPK)))
    return out.reshape(B, L, D)


def layer(x, layer_idx, norm1_g, w_in, hy_conv_w, hy_conv_b, hf_w1, hf_b1, hf_freq1,
          hf_w2, hf_b2, hf_freq2, hf_w3, hf_b3, hy_skip, lam_q1, lam_k1, lam_q2, lam_k2,
          subln_g, w_out, norm2_g, peer_wq, peer_keys, peer_u, peer_v):
    B, L, _ = x.shape
    h = rms_norm(x, norm1_g)
    z = h @ w_in
    zh, za = z[..., :3 * HY_WIDTH], z[..., 3 * HY_WIDTH:]
    zh = short_conv(zh, hy_conv_w, hy_conv_b)
    x0, x1, vh = jnp.split(zh, 3, axis=-1)
    hf, hb = hyena_filter(L, hf_w1, hf_b1, hf_freq1, hf_w2, hf_b2, hf_freq2, hf_w3, hf_b3)
    yh = x0.astype(F32) * bidir_long_conv((vh * x1).astype(F32), hf, hb, hy_skip)
    yh = yh.astype(x.dtype)
    qa, ka, va = jnp.split(za, 3, axis=-1)
    qa = qa.reshape(B, L, N_ATT_HEADS, 2, ATT_HEAD_DIM)
    ka = ka.reshape(B, L, N_ATT_HEADS, 2, ATT_HEAD_DIM)
    va = va.reshape(B, L, N_ATT_HEADS, ATT_V_DIM)
    cos, sin = rope_tables(L)
    qa = partial_rope(qa, cos, sin)
    ka = partial_rope(ka, cos, sin)
    lambda_init = 0.8 - 0.6 * math.exp(-0.3 * layer_idx)
    lam = (jnp.exp(jnp.sum(lam_q1.astype(F32) * lam_k1.astype(F32)))
           - jnp.exp(jnp.sum(lam_q2.astype(F32) * lam_k2.astype(F32))) + lambda_init)
    ya = diff_attention(qa, ka, va, lam, subln_g, lambda_init)
    x = x + jnp.concatenate([yh, ya], axis=-1) @ w_out
    x = x + peer(rms_norm(x, norm2_g), peer_wq, peer_keys, peer_u, peer_v)
    return x


def trunk(x, norm1_g, w_in, hy_conv_w, hy_conv_b, hf_w1, hf_b1, hf_freq1, hf_w2, hf_b2,
          hf_freq2, hf_w3, hf_b3, hy_skip, lam_q1, lam_k1, lam_q2, lam_k2, subln_g, w_out,
          norm2_g, peer_wq, peer_keys, peer_u, peer_v, normf_g):
    for l in range(DEPTH):
        x = layer(x, l, norm1_g[l], w_in[l], hy_conv_w[l], hy_conv_b[l], hf_w1[l], hf_b1[l],
                  hf_freq1[l], hf_w2[l], hf_b2[l], hf_freq2[l], hf_w3[l], hf_b3[l], hy_skip[l],
                  lam_q1[l], lam_k1[l], lam_q2[l], lam_k2[l], subln_g[l], w_out[l], norm2_g[l],
                  peer_wq[l], peer_keys[l], peer_u[l], peer_v[l])
    return rms_norm(x, normf_g)


def setup_inputs(seed: int = 0) -> dict:
    key = jax.random.key(seed)
    ks = jax.random.split(key, 32)
    n = lambda k, shape, s: jax.random.normal(k, shape, F32) * s
    Dp = DEPTH
    return {
        "x_prompt": n(ks[0], (BATCH, SEQ, D_MODEL), 1.0),
        "x_sample": n(ks[1], (DEC_BATCH, DEC_SEQ, D_MODEL), 1.0),
        "norm1_g": 1.0 + n(ks[2], (Dp, D_MODEL), 0.02),
        "w_in": n(ks[3], (Dp, D_MODEL, IN_WIDTH), D_MODEL ** -0.5),
        "hy_conv_w": n(ks[4], (Dp, SHORT_CONV, 3 * HY_WIDTH), SHORT_CONV ** -0.5),
        "hy_conv_b": n(ks[5], (Dp, 3 * HY_WIDTH), 0.02),
        "hf_w1": n(ks[6], (Dp, FILTER_EMB, FILTER_HIDDEN), FILTER_EMB ** -0.5),
        "hf_b1": n(ks[7], (Dp, FILTER_HIDDEN), 0.02),
        "hf_freq1": 1.0 + n(ks[8], (Dp, FILTER_HIDDEN), 0.02),
        "hf_w2": n(ks[9], (Dp, FILTER_HIDDEN, FILTER_HIDDEN), FILTER_HIDDEN ** -0.5),
        "hf_b2": n(ks[10], (Dp, FILTER_HIDDEN), 0.02),
        "hf_freq2": 1.0 + n(ks[11], (Dp, FILTER_HIDDEN), 0.02),
        "hf_w3": n(ks[12], (Dp, FILTER_HIDDEN, 2 * HY_WIDTH), 0.02 * FILTER_HIDDEN ** -0.5),
        "hf_b3": n(ks[13], (Dp, 2 * HY_WIDTH), 0.002),
        "hy_skip": n(ks[14], (Dp, HY_WIDTH), 0.1),
        "lam_q1": n(ks[15], (Dp, ATT_HEAD_DIM), 0.1),
        "lam_k1": n(ks[16], (Dp, ATT_HEAD_DIM), 0.1),
        "lam_q2": n(ks[17], (Dp, ATT_HEAD_DIM), 0.1),
        "lam_k2": n(ks[18], (Dp, ATT_HEAD_DIM), 0.1),
        "subln_g": 1.0 + n(ks[19], (Dp, ATT_V_DIM), 0.02),
        "w_out": n(ks[20], (Dp, HY_WIDTH + ATT_WIDTH, D_MODEL), (HY_WIDTH + ATT_WIDTH) ** -0.5),
        "norm2_g": 1.0 + n(ks[21], (Dp, D_MODEL), 0.02),
        "peer_wq": n(ks[22], (Dp, D_MODEL, PEER_HEADS * 2 * PEER_HALF), D_MODEL ** -0.5),
        "peer_keys": n(ks[23], (Dp, PEER_HEADS, 2, PEER_KEYS, PEER_HALF), PEER_HALF ** -0.5),
        "peer_u": n(ks[24], (Dp, PEER_EXPERTS, D_MODEL), D_MODEL ** -0.5),
        "peer_v": n(ks[25], (Dp, PEER_EXPERTS, D_MODEL), 0.5),
        "normf_g": 1.0 + n(ks[26], (D_MODEL,), 0.02),
    }


def reference(x_prompt, x_sample, norm1_g, w_in, hy_conv_w, hy_conv_b, hf_w1, hf_b1, hf_freq1,
              hf_w2, hf_b2, hf_freq2, hf_w3, hf_b3, hy_skip, lam_q1, lam_k1, lam_q2, lam_k2,
              subln_g, w_out, norm2_g, peer_wq, peer_keys, peer_u, peer_v, normf_g):
    y_prompt = trunk(x_prompt, norm1_g, w_in, hy_conv_w, hy_conv_b, hf_w1, hf_b1, hf_freq1,
                     hf_w2, hf_b2, hf_freq2, hf_w3, hf_b3, hy_skip, lam_q1, lam_k1, lam_q2,
                     lam_k2, subln_g, w_out, norm2_g, peer_wq, peer_keys, peer_u, peer_v, normf_g)
    y_sample = trunk(x_sample, norm1_g, w_in, hy_conv_w, hy_conv_b, hf_w1, hf_b1, hf_freq1,
                     hf_w2, hf_b2, hf_freq2, hf_w3, hf_b3, hy_skip, lam_q1, lam_k1, lam_q2,
                     lam_k2, subln_g, w_out, norm2_g, peer_wq, peer_keys, peer_u, peer_v, normf_g)
    return (y_prompt, y_sample)
```

```python
import functools
import math

import jax
import jax.numpy as jnp
import numpy as np
from jax import lax
from jax.experimental import pallas as pl
from jax.experimental.pallas import tpu as pltpu

F32 = jnp.float32
BF16 = jnp.bfloat16

D_MODEL = 1024
HY_WIDTH = 512
ATT_WIDTH = 512
N_ATT_HEADS = 4
ATT_HEAD_DIM = 64
ATT_V_DIM = 2 * ATT_HEAD_DIM
ROT_DIM = ATT_HEAD_DIM // 4
ROPE_THETA = 500000.0
IN_WIDTH = 3 * HY_WIDTH + 3 * ATT_WIDTH
FILTER_BANDS = 8
FILTER_EMB = 1 + 2 * FILTER_BANDS
FILTER_HIDDEN = 64
FAST_DECAY_PCT = 0.3
SLOW_DECAY_PCT = 1.5
DECAY_TARGET = 1e-2
PEER_HEADS = 8
PEER_KEYS = 128
PEER_EXPERTS = PEER_KEYS * PEER_KEYS
PEER_TOPK = 16
PEER_HALF = 128
EPS = 1e-6
SUBLN_EPS = 1e-5
LAMBDA_INIT = 0.8 - 0.6 * math.exp(-0.3 * 0)

LANES = 128
SUBLANES = 8
VMEM_LIMIT = 56 * 1024 * 1024
FFT_N2 = 128
NEG_INF = float("-inf")


def _params(semantics, vmem=VMEM_LIMIT):
    return pltpu.CompilerParams(dimension_semantics=semantics, vmem_limit_bytes=vmem)


def _rms(x, g, eps):
    return x * lax.rsqrt(jnp.mean(x * x, axis=-1, keepdims=True) + eps) * g


def _dot_nt(a, b):
    return lax.dot_general(a, b, (((1,), (1,)), ((), ())), preferred_element_type=F32)


def _inproj_kernel(x_ref, g_ref, w_ref, c_ref, sa_ref, sb_ref,
                   zh_ref, q_ref, k1_ref, k2_ref, v_ref):
    h = _rms(x_ref[...], g_ref[...], EPS).astype(BF16)
    z = jnp.dot(h, w_ref[...], preferred_element_type=F32)
    hy3 = 3 * HY_WIDTH
    zh_ref[...] = z[:, :hy3]
    c, sa, sb = c_ref[...], sa_ref[...], sb_ref[...]
    first = lax.broadcasted_iota(jnp.int32, (1, LANES), 1) < ATT_HEAD_DIM
    scale = ATT_HEAD_DIM ** -0.5

    def rope(xb):
        return (xb * c + pltpu.roll(xb, LANES - ROT_DIM // 2, 1) * sa
                + pltpu.roll(xb, ROT_DIM // 2, 1) * sb)

    for hd in range(N_ATT_HEADS):
        lo, hi = hd * LANES, (hd + 1) * LANES
        q_ref[:, lo:hi] = (rope(z[:, hy3 + lo:hy3 + hi]) * scale).astype(BF16)
        kr = rope(z[:, hy3 + ATT_WIDTH + lo:hy3 + ATT_WIDTH + hi])
        k1_ref[:, lo:hi] = jnp.where(first, kr, 0.0).astype(BF16)
        k2_ref[:, lo:hi] = jnp.where(first, 0.0, kr).astype(BF16)
    v_ref[...] = z[:, hy3 + 2 * ATT_WIDTH:].astype(BF16)


def _inproj(x, g, w_bf, rope_c, rope_sa, rope_sb, seq_len, tm=512):
    t = x.shape[0]
    nseq = seq_len // tm
    row = lambda i: (i, 0)
    fixed = lambda i: (0, 0)
    pos = lambda i: (i % nseq, 0)
    hy3 = 3 * HY_WIDTH
    return pl.pallas_call(
        _inproj_kernel,
        grid=(t // tm,),
        in_specs=[pl.BlockSpec((tm, D_MODEL), row),
                  pl.BlockSpec((1, D_MODEL), fixed),
                  pl.BlockSpec((D_MODEL, IN_WIDTH), fixed),
                  pl.BlockSpec((tm, LANES), pos),
                  pl.BlockSpec((tm, LANES), pos),
                  pl.BlockSpec((tm, LANES), pos)],
        out_specs=[pl.BlockSpec((tm, hy3), row)] + [pl.BlockSpec((tm, ATT_WIDTH), row)] * 4,
        out_shape=[jax.ShapeDtypeStruct((t, hy3), F32)]
        + [jax.ShapeDtypeStruct((t, ATT_WIDTH), BF16)] * 4,
        compiler_params=_params(("parallel",)),
        name="inproj",
    )(x, g, w_bf, rope_c, rope_sa, rope_sb)


def _shortconv_kernel(z_ref, zp_ref, zn_ref, w_ref, b_ref, u_ref, x0_ref, *, blocks_per_seq):
    i = pl.program_id(0)
    z = z_ref[...]
    tl = z.shape[0]
    row = lax.broadcasted_iota(jnp.int32, z.shape, 0)
    at_start = (i % blocks_per_seq) == 0
    at_end = (i % blocks_per_seq) == blocks_per_seq - 1
    prev_row = jnp.where(at_start, 0.0, zp_ref[SUBLANES - 1:SUBLANES, :])
    next_row = jnp.where(at_end, 0.0, zn_ref[0:1, :])
    zm = jnp.where(row == 0, prev_row, pltpu.roll(z, 1, 0))
    zp = jnp.where(row == tl - 1, next_row, pltpu.roll(z, tl - 1, 0))
    w = w_ref[...]
    conv = zm * w[0:1] + z * w[1:2] + zp * w[2:3] + b_ref[...]
    x0_ref[...] = conv[:, :HY_WIDTH]
    u_ref[...] = conv[:, 2 * HY_WIDTH:] * conv[:, HY_WIDTH:2 * HY_WIDTH]


def _shortconv(zh, w, b, seq_len, tl=512):
    t, hy3 = zh.shape
    per8 = tl // SUBLANES
    last8 = t // SUBLANES - 1
    return pl.pallas_call(
        functools.partial(_shortconv_kernel, blocks_per_seq=seq_len // tl),
        grid=(t // tl,),
        in_specs=[pl.BlockSpec((tl, hy3), lambda i: (i, 0)),
                  pl.BlockSpec((SUBLANES, hy3), lambda i: (jnp.maximum(i * per8 - 1, 0), 0)),
                  pl.BlockSpec((SUBLANES, hy3), lambda i: (jnp.minimum((i + 1) * per8, last8), 0)),
                  pl.BlockSpec((3, hy3), lambda i: (0, 0)),
                  pl.BlockSpec((1, hy3), lambda i: (0, 0))],
        out_specs=[pl.BlockSpec((tl, HY_WIDTH), lambda i: (i, 0))] * 2,
        out_shape=[jax.ShapeDtypeStruct((t, HY_WIDTH), F32)] * 2,
        compiler_params=_params(("parallel",)),
        name="shortconv",
    )(zh, zh, zh, w, b)


def _filter_kernel(f_ref, w1_ref, b1_ref, f1_ref, w2_ref, b2_ref, f2_ref, w3_ref, b3_ref,
                   dl_ref, hf_ref, hb_ref):
    i = pl.program_id(0)
    hp = lax.Precision.HIGHEST
    feats = f_ref[...]
    tl = feats.shape[0]
    h = jnp.sin(f1_ref[...] * (jnp.dot(feats, w1_ref[...], precision=hp,
                                       preferred_element_type=F32) + b1_ref[...]))
    h = jnp.sin(f2_ref[...] * (jnp.dot(h, w2_ref[...], precision=hp,
                                       preferred_element_type=F32) + b2_ref[...]))
    h3 = jnp.dot(h, w3_ref[...], precision=hp, preferred_element_type=F32) + b3_ref[...]
    decay = jnp.exp(-feats[:, 0:1] * dl_ref[...])
    hf_ref[...] = h3[:, :HY_WIDTH] * decay
    row = lax.broadcasted_iota(jnp.int32, (tl, HY_WIDTH), 0) + i * tl
    hb_ref[...] = jnp.where(row == 0, 0.0, h3[:, HY_WIDTH:] * decay)


def _pad2(a, rows, cols):
    return jnp.pad(a, ((0, rows - a.shape[0]), (0, cols - a.shape[1])))


def _hyena_filters(seq_len, w1, b1, f1, w2, b2, f2, w3, b3, tl=512):
    t = jnp.linspace(0.0, 1.0, seq_len, dtype=F32)[:, None]
    w = 2.0 * math.pi * jnp.arange(seq_len, dtype=F32) / seq_len
    bands = jnp.linspace(1e-4, FILTER_BANDS - 1, FILTER_BANDS, dtype=F32)
    ang = w[:, None] * bands[None, :]
    feats = _pad2(jnp.concatenate([t, jnp.cos(ang), -jnp.sin(ang)], axis=-1), seq_len, LANES)
    max_decay = math.log(DECAY_TARGET) / FAST_DECAY_PCT
    min_decay = math.log(DECAY_TARGET) / SLOW_DECAY_PCT
    deltas = jnp.abs(jnp.linspace(min_decay, max_decay, HY_WIDTH, dtype=F32))[None, :]
    w1p = _pad2(w1, LANES, LANES)
    w2p = _pad2(w2, LANES, LANES)
    w3p = _pad2(w3, LANES, 2 * HY_WIDTH)
    pad1 = lambda v: _pad2(v[None, :], 1, LANES)
    fixed = lambda i: (0, 0)
    vec = pl.BlockSpec((1, LANES), fixed)
    mat = pl.BlockSpec((LANES, LANES), fixed)
    return pl.pallas_call(
        _filter_kernel,
        grid=(seq_len // tl,),
        in_specs=[pl.BlockSpec((tl, LANES), lambda i: (i, 0)), mat, vec, vec, mat, vec, vec,
                  pl.BlockSpec((LANES, 2 * HY_WIDTH), fixed),
                  pl.BlockSpec((1, 2 * HY_WIDTH), fixed),
                  pl.BlockSpec((1, HY_WIDTH), fixed)],
        out_specs=[pl.BlockSpec((tl, HY_WIDTH), lambda i: (i, 0))] * 2,
        out_shape=[jax.ShapeDtypeStruct((seq_len, HY_WIDTH), F32)] * 2,
        compiler_params=_params(("parallel",)),
        name="hyena_filter",
    )(feats, w1p, pad1(b1), pad1(f1), w2p, pad1(b2), pad1(f2), w3p, b3[None, :], deltas)


def _dft_tables(seq_len):
    n = 2 * seq_len
    n2 = FFT_N2
    n1 = n // n2
    n1h = n1 // 2
    a1 = (2.0 * np.pi / n1) * np.outer(np.arange(n1), np.arange(n1))
    f1r, f1i = np.cos(a1), -np.sin(a1)
    fwd1 = np.concatenate([f1r[:, :n1h], f1i[:, :n1h]], axis=0)
    inv1 = np.concatenate([f1r[:n1h, :], f1i[:n1h, :]], axis=1) / n
    a2 = (2.0 * np.pi / n2) * np.outer(np.arange(n2), np.arange(n2))
    f2 = np.concatenate([np.cos(a2), -np.sin(a2)], axis=0)
    at = (2.0 * np.pi / n) * np.outer(np.arange(n1), np.arange(n2))
    twr = np.broadcast_to(np.cos(at)[:, :, None], (n1, n2, LANES))
    twi = np.broadcast_to(-np.sin(at)[:, :, None], (n1, n2, LANES))
    return (jnp.asarray(fwd1, BF16), jnp.asarray(inv1, BF16), jnp.asarray(f2, BF16),
            jnp.asarray(twr, F32), jnp.asarray(twi, F32), n1, n2)


def _fft1_kernel(u_ref, f_ref, ar_ref, ai_ref):
    r = jnp.dot(f_ref[...], u_ref[0].astype(BF16), preferred_element_type=F32)
    n1 = r.shape[0] // 2
    ar_ref[0] = r[:n1]
    ai_ref[0] = r[n1:]


def _fft1(u3, fwd1, tn=2048):
    b, n1h, cols = u3.shape
    n1 = 2 * n1h
    return pl.pallas_call(
        _fft1_kernel,
        grid=(b, cols // tn),
        in_specs=[pl.BlockSpec((1, n1h, tn), lambda bi, j: (bi, 0, j)),
                  pl.BlockSpec((2 * n1, n1h), lambda bi, j: (0, 0))],
        out_specs=[pl.BlockSpec((1, n1, tn), lambda bi, j: (bi, 0, j))] * 2,
        out_shape=[jax.ShapeDtypeStruct((b, n1, cols), F32)] * 2,
        compiler_params=_params(("parallel", "parallel")),
        name="fft_stage1",
    )(u3, fwd1)


def _tile_lanes(x, width):
    return jnp.concatenate([x] * (width // x.shape[1]), axis=1)


def _dft2_forward(ar, ai, twr, twi, f2):
    n2, c = ar.shape
    tr, ti = _tile_lanes(twr, c), _tile_lanes(twi, c)
    cat = jnp.concatenate([ar * tr - ai * ti, ar * ti + ai * tr], axis=1).astype(BF16)
    p = jnp.dot(f2, cat, preferred_element_type=F32)
    return p[:n2, :c] - p[n2:, c:], p[:n2, c:] + p[n2:, :c]


def _fft2_filter_kernel(ar_ref, ai_ref, twr_ref, twi_ref, f2_ref, kr_ref, ki_ref, *, kb):
    f2 = f2_ref[...]
    for j in range(kb):
        fr, fi = _dft2_forward(ar_ref[0, j], ai_ref[0, j], twr_ref[j], twi_ref[j], f2)
        gr, gi = _dft2_forward(ar_ref[1, j], ai_ref[1, j], twr_ref[j], twi_ref[j], f2)
        kr_ref[j] = fr + gr
        ki_ref[j] = fi - gi


def _fft2_filter(ar, ai, twr, twi, f2, kb=8):
    _, n1, n2, c = ar.shape
    a_spec = pl.BlockSpec((2, kb, n2, c), lambda k: (0, k, 0, 0))
    t_spec = pl.BlockSpec((kb, n2, LANES), lambda k: (k, 0, 0))
    o_spec = pl.BlockSpec((kb, n2, c), lambda k: (k, 0, 0))
    return pl.pallas_call(
        functools.partial(_fft2_filter_kernel, kb=kb),
        grid=(n1 // kb,),
        in_specs=[a_spec, a_spec, t_spec, t_spec, pl.BlockSpec((2 * n2, n2), lambda k: (0, 0))],
        out_specs=[o_spec, o_spec],
        out_shape=[jax.ShapeDtypeStruct((n1, n2, c), F32)] * 2,
        compiler_params=_params(("parallel",)),
        name="fft_filter_spectrum",
    )(ar, ai, twr, twi, f2)


def _fft2_kernel(ar_ref, ai_ref, twr_ref, twi_ref, kr_ref, ki_ref, f2_ref, br_ref, bi_ref, *, kb):
    f2 = f2_ref[...]
    for j in range(kb):
        twr, twi = twr_ref[j], twi_ref[j]
        xr, xi = _dft2_forward(ar_ref[0, j], ai_ref[0, j], twr, twi, f2)
        n2, c = xr.shape
        kr, ki = kr_ref[j], ki_ref[j]
        cat = jnp.concatenate([xr * kr - xi * ki, xr * ki + xi * kr], axis=1).astype(BF16)
        p = jnp.dot(f2, cat, preferred_element_type=F32)
        b_r = p[:n2, :c] + p[n2:, c:]
        b_i = p[:n2, c:] - p[n2:, :c]
        tr, ti = _tile_lanes(twr, c), _tile_lanes(twi, c)
        br_ref[0, j] = b_r * tr + b_i * ti
        bi_ref[0, j] = b_i * tr - b_r * ti


def _fft2(ar, ai, twr, twi, kr, ki, f2, kb=8):
    b, n1, n2, c = ar.shape
    a_spec = pl.BlockSpec((1, kb, n2, c), lambda bi, k: (bi, k, 0, 0))
    t_spec = pl.BlockSpec((kb, n2, LANES), lambda bi, k: (k, 0, 0))
    k_spec = pl.BlockSpec((kb, n2, c), lambda bi, k: (k, 0, 0))
    return pl.pallas_call(
        functools.partial(_fft2_kernel, kb=kb),
        grid=(b, n1 // kb),
        in_specs=[a_spec, a_spec, t_spec, t_spec, k_spec, k_spec,
                  pl.BlockSpec((2 * n2, n2), lambda bi, k: (0, 0))],
        out_specs=[a_spec, a_spec],
        out_shape=[jax.ShapeDtypeStruct((b, n1, n2, c), F32)] * 2,
        compiler_params=_params(("parallel", "parallel")),
        name="fft_stage2",
    )(ar, ai, twr, twi, kr, ki, f2)


def _fft3_kernel(br_ref, bi_ref, g_ref, u_ref, x0_ref, skip_ref, y_ref):
    g = g_ref[...]
    n1 = g.shape[1] // 2
    y = (jnp.dot(g[:, :n1], br_ref[0].astype(BF16), preferred_element_type=F32)
         + jnp.dot(g[:, n1:], bi_ref[0].astype(BF16), preferred_element_type=F32))
    y_ref[0] = (x0_ref[0] * (y + u_ref[0] * skip_ref[...])).astype(BF16)


def _fft3(br, bi, inv1, u3, x03, skip_t, tn=2048):
    b, n1, cols = br.shape
    n1h = n1 // 2
    big = pl.BlockSpec((1, n1, tn), lambda bi_, j: (bi_, 0, j))
    half = pl.BlockSpec((1, n1h, tn), lambda bi_, j: (bi_, 0, j))
    return pl.pallas_call(
        _fft3_kernel,
        grid=(b, cols // tn),
        in_specs=[big, big, pl.BlockSpec((n1h, 2 * n1), lambda bi_, j: (0, 0)), half, half,
                  pl.BlockSpec((1, tn), lambda bi_, j: (0, 0))],
        out_specs=half,
        out_shape=jax.ShapeDtypeStruct((b, n1h, cols), BF16),
        compiler_params=_params(("parallel", "parallel")),
        name="fft_stage3",
    )(br, bi, inv1, u3, x03, skip_t)


def _hyena_long_conv(u, x0, hf, hb, skip, batch, seq_len):
    c = HY_WIDTH
    fwd1, inv1, f2, twr, twi, n1, n2 = _dft_tables(seq_len)
    n1h = n1 // 2
    fa_r, fa_i = _fft1(jnp.stack([hf, hb]).reshape(2, n1h, n2 * c), fwd1)
    kr, ki = _fft2_filter(fa_r.reshape(2, n1, n2, c), fa_i.reshape(2, n1, n2, c), twr, twi, f2)
    u3 = u.reshape(batch, n1h, n2 * c)
    a_r, a_i = _fft1(u3, fwd1)
    b_r, b_i = _fft2(a_r.reshape(batch, n1, n2, c), a_i.reshape(batch, n1, n2, c),
                     twr, twi, kr, ki, f2)
    tn = 2048
    skip_t = jnp.tile(skip[None, :], (1, tn // c))
    yh = _fft3(b_r.reshape(batch, n1, n2 * c), b_i.reshape(batch, n1, n2 * c), inv1,
               u3, x0.reshape(batch, n1h, n2 * c), skip_t, tn)
    return yh.reshape(batch * seq_len, c)


def _attn_kernel(q_ref, k1_ref, k2_ref, v_ref, lam_ref, sg_ref, o_ref,
                 m1, l1, a1, m2, l2, a2):
    ki = pl.program_id(3)

    @pl.when(ki == 0)
    def _():
        for m, l, a in ((m1, l1, a1), (m2, l2, a2)):
            m[...] = jnp.full(m.shape, NEG_INF, F32)
            l[...] = jnp.zeros(l.shape, F32)
            a[...] = jnp.zeros(a.shape, F32)

    q = q_ref[...]
    v = v_ref[...]

    def component(k_ref, m, l, a):
        s = _dot_nt(q, k_ref[...])
        reps = s.shape[1] // LANES
        m_prev = m[...]
        m_new = jnp.maximum(m_prev, jnp.max(s, axis=1, keepdims=True))
        alpha = jnp.exp(m_prev - m_new)
        p = jnp.exp(s - pltpu.repeat(m_new, reps, 1))
        l[...] = alpha * l[...] + jnp.sum(p, axis=1, keepdims=True)
        a[...] = alpha * a[...] + jnp.dot(p.astype(BF16), v, preferred_element_type=F32)
        m[...] = m_new

    component(k1_ref, m1, l1, a1)
    component(k2_ref, m2, l2, a2)

    @pl.when(ki == pl.num_programs(3) - 1)
    def _():
        lv = lam_ref[...]
        lam = (jnp.exp(jnp.sum(lv[0:1] * lv[1:2], axis=1, keepdims=True))
               - jnp.exp(jnp.sum(lv[2:3] * lv[3:4], axis=1, keepdims=True)) + LAMBDA_INIT)
        o = a1[...] / l1[...] - lam * (a2[...] / l2[...])
        o_ref[...] = (_rms(o, sg_ref[...], SUBLN_EPS) * (1.0 - LAMBDA_INIT)).astype(BF16)


def _attention(q, k1, k2, v, lam_vecs, subln_g, batch, seq_len, tq=512, tk=512):
    nq, nk = seq_len // tq, seq_len // tk
    q_spec = pl.BlockSpec((tq, LANES), lambda b, h, i, j: (b * nq + i, h))
    k_spec = pl.BlockSpec((tk, LANES), lambda b, h, i, j: (b * nk + j, h))
    acc = pltpu.VMEM((tq, LANES), F32)
    return pl.pallas_call(
        _attn_kernel,
        grid=(batch, N_ATT_HEADS, nq, nk),
        in_specs=[q_spec, k_spec, k_spec, k_spec,
                  pl.BlockSpec((4, ATT_HEAD_DIM), lambda b, h, i, j: (0, 0)),
                  pl.BlockSpec((1, ATT_V_DIM), lambda b, h, i, j: (0, 0))],
        out_specs=q_spec,
        out_shape=jax.ShapeDtypeStruct((batch * seq_len, ATT_WIDTH), BF16),
        scratch_shapes=[acc] * 6,
        compiler_params=_params(("parallel", "parallel", "parallel", "arbitrary")),
        name="diff_attention",
    )(q, k1, k2, v, lam_vecs, subln_g)


def _split_bf16(x):
    hi = x.astype(BF16)
    return hi, (x - hi.astype(F32)).astype(BF16)


def _outproj_kernel(yh_ref, ya_ref, x_ref, wo_ref, g_ref, wqh_ref, wql_ref, kh_ref, kl_ref,
                    x2_ref, xn_ref, st_ref):
    wo = wo_ref[...]
    x2 = (x_ref[...] + jnp.dot(yh_ref[...], wo[:HY_WIDTH], preferred_element_type=F32)
          + jnp.dot(ya_ref[...], wo[HY_WIDTH:], preferred_element_type=F32))
    x2_ref[...] = x2
    xn = _rms(x2, g_ref[...], EPS)
    xh, xl = _split_bf16(xn)
    xn_ref[...] = xh
    q = (jnp.dot(xh, wqh_ref[...], preferred_element_type=F32)
         + jnp.dot(xl, wqh_ref[...], preferred_element_type=F32)
         + jnp.dot(xh, wql_ref[...], preferred_element_type=F32))
    for hc in range(2 * PEER_HEADS):
        qh, ql = _split_bf16(q[:, hc * PEER_HALF:(hc + 1) * PEER_HALF])
        kh, kl = kh_ref[hc], kl_ref[hc]
        st_ref[hc * PEER_KEYS:(hc + 1) * PEER_KEYS, :] = (
            _dot_nt(kh, qh) + _dot_nt(kh, ql) + _dot_nt(kl, qh))


def _outproj(yh, ya, x, wo_bf, g2, wq_hi, wq_lo, keys_hi, keys_lo, tm=512):
    t = x.shape[0]
    qw = 2 * PEER_HEADS * PEER_HALF
    sw = 2 * PEER_HEADS * PEER_KEYS
    row = lambda i: (i, 0)
    fixed = lambda i: (0, 0)
    fixed3 = lambda i: (0, 0, 0)
    return pl.pallas_call(
        _outproj_kernel,
        grid=(t // tm,),
        in_specs=[pl.BlockSpec((tm, HY_WIDTH), row), pl.BlockSpec((tm, ATT_WIDTH), row),
                  pl.BlockSpec((tm, D_MODEL), row),
                  pl.BlockSpec((HY_WIDTH + ATT_WIDTH, D_MODEL), fixed),
                  pl.BlockSpec((1, D_MODEL), fixed),
                  pl.BlockSpec((D_MODEL, qw), fixed), pl.BlockSpec((D_MODEL, qw), fixed),
                  pl.BlockSpec((2 * PEER_HEADS, PEER_KEYS, PEER_HALF), fixed3),
                  pl.BlockSpec((2 * PEER_HEADS, PEER_KEYS, PEER_HALF), fixed3)],
        out_specs=[pl.BlockSpec((tm, D_MODEL), row), pl.BlockSpec((tm, D_MODEL), row),
                   pl.BlockSpec((sw, tm), lambda i: (0, i))],
        out_shape=[jax.ShapeDtypeStruct((t, D_MODEL), F32),
                   jax.ShapeDtypeStruct((t, D_MODEL), BF16),
                   jax.ShapeDtypeStruct((sw, t), F32)],
        compiler_params=_params(("parallel",)),
        name="outproj_peer_query",
    )(yh, ya, x, wo_bf, g2, wq_hi, wq_lo, keys_hi, keys_lo)


def _extract_max(pieces, offsets, total):
    m = pieces[0]
    for p in pieces[1:]:
        m = jnp.maximum(m, p)
    m = jnp.max(m, axis=0, keepdims=True)
    iotas = [lax.broadcasted_iota(jnp.int32, p.shape, 0) + off for p, off in zip(pieces, offsets)]
    cand = [jnp.where(p == m, io, total) for p, io in zip(pieces, iotas)]
    idx = cand[0]
    for c in cand[1:]:
        idx = jnp.minimum(idx, c)
    idx = jnp.min(idx, axis=0, keepdims=True)
    hot = [io == idx for io in iotas]
    return m, hot, [jnp.where(h, NEG_INF, p) for h, p in zip(hot, pieces)]


def _topk_rank(s):
    rank = jnp.full(s.shape, float(PEER_TOPK), F32)
    vals = []
    for r in range(PEER_TOPK):
        m, hot, (s,) = _extract_max([s], [0], s.shape[0])
        vals.append(m)
        rank = jnp.where(hot[0], float(r), rank)
    return vals, rank


def _route_kernel(st_ref, cnt_ref, w0_ref, r1_ref, e1_ref):
    s0 = st_ref[:PEER_KEYS, :]
    s1 = st_ref[PEER_KEYS:, :]
    v0, rank0 = _topk_rank(s0)
    v1, rank1 = _topk_rank(s1)
    sv1 = jnp.concatenate(v1, axis=0)
    pieces = [v0[k] + sv1 for k in range(PEER_TOPK)]
    offsets = [k * PEER_TOPK for k in range(PEER_TOPK)]
    z = None
    cmax = None
    for r in range(PEER_TOPK):
        m, _, pieces = _extract_max(pieces, offsets, PEER_TOPK * PEER_TOPK)
        if r == 0:
            cmax = m
            z = jnp.ones_like(m)
        else:
            z = z + jnp.exp(m - cmax)
    cnt = jnp.zeros(s0.shape, F32)
    for k in range(PEER_TOPK):
        taken = jnp.sum(jnp.where(pieces[k] == NEG_INF, 1.0, 0.0), axis=0, keepdims=True)
        cnt = jnp.where(rank0 == float(k), taken, cnt)
    cnt_ref[0] = cnt
    w0_ref[0] = jnp.exp(s0 - v0[0]) / z
    r1_ref[0] = rank1
    e1_ref[0] = jnp.exp(s1 - v1[0])


def _route(st, tt=512):
    sw, t = st.shape
    o_spec = pl.BlockSpec((1, PEER_KEYS, tt), lambda i, h: (h, 0, i))
    return pl.pallas_call(
        _route_kernel,
        grid=(t // tt, PEER_HEADS),
        in_specs=[pl.BlockSpec((2 * PEER_KEYS, tt), lambda i, h: (h, i))],
        out_specs=[o_spec] * 4,
        out_shape=[jax.ShapeDtypeStruct((PEER_HEADS, PEER_KEYS, t), F32)] * 4,
        compiler_params=_params(("parallel", "parallel")),
        name="peer_route",
    )(st)


def _gelu(x):
    return 0.5 * x * (1.0 + lax.erf(x * (2.0 ** -0.5)))


def _peer_kernel(xn_ref, u_ref, vt_ref, cnt_ref, w0_ref, r1_ref, e1_ref, x2_ref, gf_ref,
                 y_ref, acc_ref, a_ref, *, rows_per_step):
    e = pl.program_id(1)

    @pl.when(e == 0)
    def _():
        acc_ref[...] = jnp.zeros(acc_ref.shape, F32)

    hid = _dot_nt(u_ref[...], xn_ref[...])
    for ii in range(rows_per_step):
        g = None
        for h in range(PEER_HEADS):
            term = jnp.where(r1_ref[h] < cnt_ref[h, ii:ii + 1, :],
                             e1_ref[h] * w0_ref[h, ii:ii + 1, :], 0.0)
            g = term if g is None else g + term
        lo, hi = ii * PEER_KEYS, (ii + 1) * PEER_KEYS
        a_ref[lo:hi, :] = (_gelu(hid[lo:hi]) * g).astype(BF16)
    acc_ref[...] += jnp.dot(vt_ref[...], a_ref[...], preferred_element_type=F32)

    @pl.when(e == pl.num_programs(1) - 1)
    def _():
        x3 = x2_ref[...] + acc_ref[...].T
        y_ref[...] = _rms(x3, gf_ref[...], EPS)


def _peer(xn, u_bf, vt_bf, cnt, w0, r1, e1, x2, gf, tt=512, rows_per_step=8):
    t = xn.shape[0]
    te = rows_per_step * PEER_KEYS
    tok = lambda i, e: (i, 0)
    tab_i = pl.BlockSpec((PEER_HEADS, rows_per_step, tt), lambda i, e: (0, e, i))
    tab_j = pl.BlockSpec((PEER_HEADS, PEER_KEYS, tt), lambda i, e: (0, 0, i))
    return pl.pallas_call(
        functools.partial(_peer_kernel, rows_per_step=rows_per_step),
        grid=(t // tt, PEER_EXPERTS // te),
        in_specs=[pl.BlockSpec((tt, D_MODEL), tok),
                  pl.BlockSpec((te, D_MODEL), lambda i, e: (e, 0)),
                  pl.BlockSpec((D_MODEL, te), lambda i, e: (0, e)),
                  tab_i, tab_i, tab_j, tab_j,
                  pl.BlockSpec((tt, D_MODEL), tok),
                  pl.BlockSpec((1, D_MODEL), lambda i, e: (0, 0))],
        out_specs=pl.BlockSpec((tt, D_MODEL), tok),
        out_shape=jax.ShapeDtypeStruct((t, D_MODEL), F32),
        scratch_shapes=[pltpu.VMEM((D_MODEL, tt), F32), pltpu.VMEM((te, tt), BF16)],
        compiler_params=_params(("parallel", "arbitrary")),
        name="peer_dense",
    )(xn, u_bf, vt_bf, cnt, w0, r1, e1, x2, gf)


def _rope_tables(seq_len):
    inv = ROPE_THETA ** (-jnp.arange(0, ROT_DIM, 2, dtype=F32) / ROT_DIM)
    ang = jnp.arange(seq_len, dtype=F32)[:, None] * inv[None, :]
    cos, sin = jnp.cos(ang), jnp.sin(ang)
    half = ROT_DIM // 2
    one = jnp.ones((seq_len, ATT_HEAD_DIM - ROT_DIM), F32)
    zero = jnp.zeros((seq_len, ATT_HEAD_DIM - half), F32)
    c = jnp.concatenate([cos, cos, one], axis=1)
    sa = jnp.concatenate([-sin, zero], axis=1)
    sb = jnp.concatenate([jnp.zeros((seq_len, half), F32), sin,
                          jnp.zeros((seq_len, ATT_HEAD_DIM - ROT_DIM), F32)], axis=1)
    both = lambda a: jnp.concatenate([a, a], axis=1)
    return both(c), both(sa), both(sb)


def _trunk(x, p):
    batch, seq_len, _ = x.shape
    xf = x.reshape(batch * seq_len, D_MODEL)
    rope_c, rope_sa, rope_sb = _rope_tables(seq_len)
    zh, q, k1, k2, v = _inproj(xf, p["norm1_g"], p["w_in"], rope_c, rope_sa, rope_sb, seq_len)
    u, x0 = _shortconv(zh, p["hy_conv_w"], p["hy_conv_b"], seq_len)
    hf, hb = _hyena_filters(seq_len, *p["filter"])
    yh = _hyena_long_conv(u, x0, hf, hb, p["hy_skip"], batch, seq_len)
    ya = _attention(q, k1, k2, v, p["lam_vecs"], p["subln_g"], batch, seq_len)
    x2, xn, st = _outproj(yh, ya, xf, p["w_out"], p["norm2_g"], p["wq_hi"], p["wq_lo"],
                          p["keys_hi"], p["keys_lo"])
    cnt, w0, r1, e1 = _route(st)
    y = _peer(xn, p["peer_u"], p["peer_vt"], cnt, w0, r1, e1, x2, p["normf_g"])
    return y.reshape(batch, seq_len, D_MODEL)


def kernel(x_prompt, x_sample, norm1_g, w_in, hy_conv_w, hy_conv_b, hf_w1, hf_b1, hf_freq1,
           hf_w2, hf_b2, hf_freq2, hf_w3, hf_b3, hy_skip, lam_q1, lam_k1, lam_q2, lam_k2,
           subln_g, w_out, norm2_g, peer_wq, peer_keys, peer_u, peer_v, normf_g):
    l = 0
    wq = peer_wq[l]
    wq_hi = wq.astype(BF16)
    keys = peer_keys[l].reshape(2 * PEER_HEADS, PEER_KEYS, PEER_HALF)
    keys_hi = keys.astype(BF16)
    p = {
        "norm1_g": norm1_g[l][None, :],
        "w_in": w_in[l].astype(BF16),
        "hy_conv_w": hy_conv_w[l],
        "hy_conv_b": hy_conv_b[l][None, :],
        "filter": (hf_w1[l], hf_b1[l], hf_freq1[l], hf_w2[l], hf_b2[l], hf_freq2[l],
                   hf_w3[l], hf_b3[l]),
        "hy_skip": hy_skip[l],
        "lam_vecs": jnp.stack([lam_q1[l], lam_k1[l], lam_q2[l], lam_k2[l]]),
        "subln_g": subln_g[l][None, :],
        "w_out": w_out[l].astype(BF16),
        "norm2_g": norm2_g[l][None, :],
        "wq_hi": wq_hi,
        "wq_lo": (wq - wq_hi.astype(F32)).astype(BF16),
        "keys_hi": keys_hi,
        "keys_lo": (keys - keys_hi.astype(F32)).astype(BF16),
        "peer_u": peer_u[l].astype(BF16),
        "peer_vt": peer_v[l].astype(BF16).T,
        "normf_g": normf_g[None, :],
    }
    return (_trunk(x_prompt, p), _trunk(x_sample, p))
```

```python
import functools
import math

import jax
import jax.numpy as jnp
import numpy as np
from jax import lax
from jax.experimental import pallas as pl
from jax.experimental.pallas import tpu as pltpu

F32 = jnp.float32
BF16 = jnp.bfloat16

D_MODEL = 1024
HY_WIDTH = 512
ATT_WIDTH = 512
N_ATT_HEADS = 4
ATT_HEAD_DIM = 64
ATT_V_DIM = 2 * ATT_HEAD_DIM
ROT_DIM = ATT_HEAD_DIM // 4
ROPE_THETA = 500000.0
IN_WIDTH = 3 * HY_WIDTH + 3 * ATT_WIDTH
FILTER_BANDS = 8
FILTER_EMB = 1 + 2 * FILTER_BANDS
FILTER_HIDDEN = 64
FAST_DECAY_PCT = 0.3
SLOW_DECAY_PCT = 1.5
DECAY_TARGET = 1e-2
PEER_HEADS = 8
PEER_KEYS = 128
PEER_EXPERTS = PEER_KEYS * PEER_KEYS
PEER_TOPK = 16
PEER_HALF = 128
EPS = 1e-6
SUBLN_EPS = 1e-5
LAMBDA_INIT = 0.8 - 0.6 * math.exp(-0.3 * 0)

LANES = 128
SUBLANES = 8
VMEM_LIMIT = 56 * 1024 * 1024
FFT_N2 = 128
VT_ROWS = ATT_V_DIM + 2 * SUBLANES
NEG_INF = float("-inf")


def _params(semantics, vmem=VMEM_LIMIT):
    return pltpu.CompilerParams(dimension_semantics=semantics, vmem_limit_bytes=vmem)


def _rms(x, g, eps):
    return x * lax.rsqrt(jnp.mean(x * x, axis=-1, keepdims=True) + eps) * g


def _dot_nt(a, b):
    return lax.dot_general(a, b, (((1,), (1,)), ((), ())), preferred_element_type=F32)


def _inproj_kernel(x_ref, g_ref, w_ref, c_ref, sa_ref, sb_ref,
                   zh_ref, qt1_ref, qt2_ref, k_ref, vt_ref):
    h = _rms(x_ref[...], g_ref[...], EPS).astype(BF16)
    z = jnp.dot(h, w_ref[...], preferred_element_type=F32)
    hy3 = 3 * HY_WIDTH
    zh_ref[...] = z[:, :hy3]
    c, sa, sb = c_ref[...], sa_ref[...], sb_ref[...]
    scale = ATT_HEAD_DIM ** -0.5 * math.log2(math.e)

    def rope(xb):
        return (xb * c + pltpu.roll(xb, LANES - ROT_DIM // 2, 1) * sa
                + pltpu.roll(xb, ROT_DIM // 2, 1) * sb)

    first = lax.broadcasted_iota(jnp.int32, (LANES, z.shape[0]), 0) < ATT_HEAD_DIM
    for hd in range(N_ATT_HEADS):
        lo, hi = hd * LANES, (hd + 1) * LANES
        qt = (rope(z[:, hy3 + lo:hy3 + hi]) * scale).T
        qt1_ref[lo:hi, :] = jnp.where(first, qt, 0.0).astype(BF16)
        qt2_ref[lo:hi, :] = jnp.where(first, 0.0, qt).astype(BF16)
        k_ref[:, lo:hi] = rope(z[:, hy3 + ATT_WIDTH + lo:hy3 + ATT_WIDTH + hi]).astype(BF16)
        vlo = hd * VT_ROWS
        vt_ref[vlo:vlo + ATT_V_DIM, :] = (
            z[:, hy3 + 2 * ATT_WIDTH + lo:hy3 + 2 * ATT_WIDTH + hi].T.astype(BF16))
        vt_ref[vlo + ATT_V_DIM:vlo + VT_ROWS, :] = jnp.ones(
            (VT_ROWS - ATT_V_DIM, z.shape[0]), BF16)


def _inproj(x, g, w_bf, rope_c, rope_sa, rope_sb, seq_len, tm=512):
    t = x.shape[0]
    nseq = seq_len // tm
    row = lambda i: (i, 0)
    col = lambda i: (0, i)
    fixed = lambda i: (0, 0)
    pos = lambda i: (i % nseq, 0)
    hy3 = 3 * HY_WIDTH
    tr_spec = pl.BlockSpec((ATT_WIDTH, tm), col)
    tr_shape = jax.ShapeDtypeStruct((ATT_WIDTH, t), BF16)
    return pl.pallas_call(
        _inproj_kernel,
        grid=(t // tm,),
        in_specs=[pl.BlockSpec((tm, D_MODEL), row),
                  pl.BlockSpec((1, D_MODEL), fixed),
                  pl.BlockSpec((D_MODEL, IN_WIDTH), fixed),
                  pl.BlockSpec((tm, LANES), pos),
                  pl.BlockSpec((tm, LANES), pos),
                  pl.BlockSpec((tm, LANES), pos)],
        out_specs=[pl.BlockSpec((tm, hy3), row), tr_spec, tr_spec,
                   pl.BlockSpec((tm, ATT_WIDTH), row),
                   pl.BlockSpec((N_ATT_HEADS * VT_ROWS, tm), col)],
        out_shape=[jax.ShapeDtypeStruct((t, hy3), F32), tr_shape, tr_shape,
                   jax.ShapeDtypeStruct((t, ATT_WIDTH), BF16),
                   jax.ShapeDtypeStruct((N_ATT_HEADS * VT_ROWS, t), BF16)],
        compiler_params=_params(("parallel",)),
        name="inproj",
    )(x, g, w_bf, rope_c, rope_sa, rope_sb)


def _shortconv_kernel(z_ref, zp_ref, zn_ref, w_ref, b_ref, u_ref, x0_ref, *, blocks_per_seq):
    i = pl.program_id(0)
    z = z_ref[...]
    tl = z.shape[0]
    row = lax.broadcasted_iota(jnp.int32, z.shape, 0)
    at_start = (i % blocks_per_seq) == 0
    at_end = (i % blocks_per_seq) == blocks_per_seq - 1
    prev_row = jnp.where(at_start, 0.0, zp_ref[SUBLANES - 1:SUBLANES, :])
    next_row = jnp.where(at_end, 0.0, zn_ref[0:1, :])
    zm = jnp.where(row == 0, prev_row, pltpu.roll(z, 1, 0))
    zp = jnp.where(row == tl - 1, next_row, pltpu.roll(z, tl - 1, 0))
    w = w_ref[...]
    conv = zm * w[0:1] + z * w[1:2] + zp * w[2:3] + b_ref[...]
    x0_ref[...] = conv[:, :HY_WIDTH]
    u_ref[...] = conv[:, 2 * HY_WIDTH:] * conv[:, HY_WIDTH:2 * HY_WIDTH]


def _shortconv(zh, w, b, seq_len, tl=512):
    t, hy3 = zh.shape
    per8 = tl // SUBLANES
    last8 = t // SUBLANES - 1
    return pl.pallas_call(
        functools.partial(_shortconv_kernel, blocks_per_seq=seq_len // tl),
        grid=(t // tl,),
        in_specs=[pl.BlockSpec((tl, hy3), lambda i: (i, 0)),
                  pl.BlockSpec((SUBLANES, hy3), lambda i: (jnp.maximum(i * per8 - 1, 0), 0)),
                  pl.BlockSpec((SUBLANES, hy3), lambda i: (jnp.minimum((i + 1) * per8, last8), 0)),
                  pl.BlockSpec((3, hy3), lambda i: (0, 0)),
                  pl.BlockSpec((1, hy3), lambda i: (0, 0))],
        out_specs=[pl.BlockSpec((tl, HY_WIDTH), lambda i: (i, 0))] * 2,
        out_shape=[jax.ShapeDtypeStruct((t, HY_WIDTH), F32)] * 2,
        compiler_params=_params(("parallel",)),
        name="shortconv",
    )(zh, zh, zh, w, b)


def _filter_kernel(f_ref, w1_ref, b1_ref, f1_ref, w2_ref, b2_ref, f2_ref, w3_ref, b3_ref,
                   dl_ref, hf_ref, hb_ref):
    i = pl.program_id(0)
    hp = lax.Precision.HIGHEST
    feats = f_ref[...]
    tl = feats.shape[0]
    h = jnp.sin(f1_ref[...] * (jnp.dot(feats, w1_ref[...], precision=hp,
                                       preferred_element_type=F32) + b1_ref[...]))
    h = jnp.sin(f2_ref[...] * (jnp.dot(h, w2_ref[...], precision=hp,
                                       preferred_element_type=F32) + b2_ref[...]))
    h3 = jnp.dot(h, w3_ref[...], precision=hp, preferred_element_type=F32) + b3_ref[...]
    decay = jnp.exp(-feats[:, 0:1] * dl_ref[...])
    hf_ref[...] = h3[:, :HY_WIDTH] * decay
    row = lax.broadcasted_iota(jnp.int32, (tl, HY_WIDTH), 0) + i * tl
    hb_ref[...] = jnp.where(row == 0, 0.0, h3[:, HY_WIDTH:] * decay)


def _pad2(a, rows, cols):
    return jnp.pad(a, ((0, rows - a.shape[0]), (0, cols - a.shape[1])))


def _hyena_filters(seq_len, w1, b1, f1, w2, b2, f2, w3, b3, tl=512):
    t = jnp.linspace(0.0, 1.0, seq_len, dtype=F32)[:, None]
    w = 2.0 * math.pi * jnp.arange(seq_len, dtype=F32) / seq_len
    bands = jnp.linspace(1e-4, FILTER_BANDS - 1, FILTER_BANDS, dtype=F32)
    ang = w[:, None] * bands[None, :]
    feats = _pad2(jnp.concatenate([t, jnp.cos(ang), -jnp.sin(ang)], axis=-1), seq_len, LANES)
    max_decay = math.log(DECAY_TARGET) / FAST_DECAY_PCT
    min_decay = math.log(DECAY_TARGET) / SLOW_DECAY_PCT
    deltas = jnp.abs(jnp.linspace(min_decay, max_decay, HY_WIDTH, dtype=F32))[None, :]
    w1p = _pad2(w1, LANES, LANES)
    w2p = _pad2(w2, LANES, LANES)
    w3p = _pad2(w3, LANES, 2 * HY_WIDTH)
    pad1 = lambda v: _pad2(v[None, :], 1, LANES)
    fixed = lambda i: (0, 0)
    vec = pl.BlockSpec((1, LANES), fixed)
    mat = pl.BlockSpec((LANES, LANES), fixed)
    return pl.pallas_call(
        _filter_kernel,
        grid=(seq_len // tl,),
        in_specs=[pl.BlockSpec((tl, LANES), lambda i: (i, 0)), mat, vec, vec, mat, vec, vec,
                  pl.BlockSpec((LANES, 2 * HY_WIDTH), fixed),
                  pl.BlockSpec((1, 2 * HY_WIDTH), fixed),
                  pl.BlockSpec((1, HY_WIDTH), fixed)],
        out_specs=[pl.BlockSpec((tl, HY_WIDTH), lambda i: (i, 0))] * 2,
        out_shape=[jax.ShapeDtypeStruct((seq_len, HY_WIDTH), F32)] * 2,
        compiler_params=_params(("parallel",)),
        name="hyena_filter",
    )(feats, w1p, pad1(b1), pad1(f1), w2p, pad1(b2), pad1(f2), w3p, b3[None, :], deltas)


def _dft_tables(seq_len):
    n = 2 * seq_len
    n2 = FFT_N2
    n1 = n // n2
    n1h = n1 // 2
    a1 = (2.0 * np.pi / n1) * np.outer(np.arange(n1), np.arange(n1))
    f1r, f1i = np.cos(a1), -np.sin(a1)
    fwd1 = np.concatenate([f1r[:, :n1h], f1i[:, :n1h]], axis=0)
    inv1 = np.concatenate([f1r[:n1h, :], f1i[:n1h, :]], axis=1) / n
    a2 = (2.0 * np.pi / n2) * np.outer(np.arange(n2), np.arange(n2))
    f2 = np.concatenate([np.cos(a2), -np.sin(a2)], axis=0)
    at = (2.0 * np.pi / n) * np.outer(np.arange(n1), np.arange(n2))
    twr = np.broadcast_to(np.cos(at)[:, :, None], (n1, n2, LANES))
    twi = np.broadcast_to(-np.sin(at)[:, :, None], (n1, n2, LANES))
    return (jnp.asarray(fwd1, BF16), jnp.asarray(inv1, BF16), jnp.asarray(f2, BF16),
            jnp.asarray(twr, F32), jnp.asarray(twi, F32), n1, n2)


def _fft1_kernel(u_ref, f_ref, ar_ref, ai_ref):
    r = jnp.dot(f_ref[...], u_ref[0].astype(BF16), preferred_element_type=F32)
    n1 = r.shape[0] // 2
    ar_ref[0] = r[:n1]
    ai_ref[0] = r[n1:]


def _fft1(u3, fwd1, tn=2048):
    b, n1h, cols = u3.shape
    n1 = 2 * n1h
    return pl.pallas_call(
        _fft1_kernel,
        grid=(b, cols // tn),
        in_specs=[pl.BlockSpec((1, n1h, tn), lambda bi, j: (bi, 0, j)),
                  pl.BlockSpec((2 * n1, n1h), lambda bi, j: (0, 0))],
        out_specs=[pl.BlockSpec((1, n1, tn), lambda bi, j: (bi, 0, j))] * 2,
        out_shape=[jax.ShapeDtypeStruct((b, n1, cols), F32)] * 2,
        compiler_params=_params(("parallel", "parallel")),
        name="fft_stage1",
    )(u3, fwd1)


def _tile_lanes(x, width):
    return jnp.concatenate([x] * (width // x.shape[1]), axis=1)


def _dft2_forward(ar, ai, twr, twi, f2):
    n2, c = ar.shape
    tr, ti = _tile_lanes(twr, c), _tile_lanes(twi, c)
    cat = jnp.concatenate([ar * tr - ai * ti, ar * ti + ai * tr], axis=1).astype(BF16)
    p = jnp.dot(f2, cat, preferred_element_type=F32)
    return p[:n2, :c] - p[n2:, c:], p[:n2, c:] + p[n2:, :c]


def _fft2_filter_kernel(ar_ref, ai_ref, twr_ref, twi_ref, f2_ref, kr_ref, ki_ref, *, kb):
    f2 = f2_ref[...]
    for j in range(kb):
        fr, fi = _dft2_forward(ar_ref[0, j], ai_ref[0, j], twr_ref[j], twi_ref[j], f2)
        gr, gi = _dft2_forward(ar_ref[1, j], ai_ref[1, j], twr_ref[j], twi_ref[j], f2)
        kr_ref[j] = fr + gr
        ki_ref[j] = fi - gi


def _fft2_filter(ar, ai, twr, twi, f2, kb=8):
    _, n1, n2, c = ar.shape
    a_spec = pl.BlockSpec((2, kb, n2, c), lambda k: (0, k, 0, 0))
    t_spec = pl.BlockSpec((kb, n2, LANES), lambda k: (k, 0, 0))
    o_spec = pl.BlockSpec((kb, n2, c), lambda k: (k, 0, 0))
    return pl.pallas_call(
        functools.partial(_fft2_filter_kernel, kb=kb),
        grid=(n1 // kb,),
        in_specs=[a_spec, a_spec, t_spec, t_spec, pl.BlockSpec((2 * n2, n2), lambda k: (0, 0))],
        out_specs=[o_spec, o_spec],
        out_shape=[jax.ShapeDtypeStruct((n1, n2, c), F32)] * 2,
        compiler_params=_params(("parallel",)),
        name="fft_filter_spectrum",
    )(ar, ai, twr, twi, f2)


def _fft2_kernel(ar_ref, ai_ref, twr_ref, twi_ref, kr_ref, ki_ref, f2_ref, br_ref, bi_ref, *, kb):
    f2 = f2_ref[...]
    for j in range(kb):
        twr, twi = twr_ref[j], twi_ref[j]
        xr, xi = _dft2_forward(ar_ref[0, j], ai_ref[0, j], twr, twi, f2)
        n2, c = xr.shape
        kr, ki = kr_ref[j], ki_ref[j]
        cat = jnp.concatenate([xr * kr - xi * ki, xr * ki + xi * kr], axis=1).astype(BF16)
        p = jnp.dot(f2, cat, preferred_element_type=F32)
        b_r = p[:n2, :c] + p[n2:, c:]
        b_i = p[:n2, c:] - p[n2:, :c]
        tr, ti = _tile_lanes(twr, c), _tile_lanes(twi, c)
        br_ref[0, j] = b_r * tr + b_i * ti
        bi_ref[0, j] = b_i * tr - b_r * ti


def _fft2(ar, ai, twr, twi, kr, ki, f2, kb=8):
    b, n1, n2, c = ar.shape
    a_spec = pl.BlockSpec((1, kb, n2, c), lambda bi, k: (bi, k, 0, 0))
    t_spec = pl.BlockSpec((kb, n2, LANES), lambda bi, k: (k, 0, 0))
    k_spec = pl.BlockSpec((kb, n2, c), lambda bi, k: (k, 0, 0))
    return pl.pallas_call(
        functools.partial(_fft2_kernel, kb=kb),
        grid=(b, n1 // kb),
        in_specs=[a_spec, a_spec, t_spec, t_spec, k_spec, k_spec,
                  pl.BlockSpec((2 * n2, n2), lambda bi, k: (0, 0))],
        out_specs=[a_spec, a_spec],
        out_shape=[jax.ShapeDtypeStruct((b, n1, n2, c), F32)] * 2,
        compiler_params=_params(("parallel", "parallel")),
        name="fft_stage2",
    )(ar, ai, twr, twi, kr, ki, f2)


def _fft3_kernel(br_ref, bi_ref, g_ref, u_ref, x0_ref, skip_ref, y_ref):
    g = g_ref[...]
    n1 = g.shape[1] // 2
    y = (jnp.dot(g[:, :n1], br_ref[0].astype(BF16), preferred_element_type=F32)
         + jnp.dot(g[:, n1:], bi_ref[0].astype(BF16), preferred_element_type=F32))
    y_ref[0] = (x0_ref[0] * (y + u_ref[0] * skip_ref[...])).astype(BF16)


def _fft3(br, bi, inv1, u3, x03, skip_t, tn=2048):
    b, n1, cols = br.shape
    n1h = n1 // 2
    big = pl.BlockSpec((1, n1, tn), lambda bi_, j: (bi_, 0, j))
    half = pl.BlockSpec((1, n1h, tn), lambda bi_, j: (bi_, 0, j))
    return pl.pallas_call(
        _fft3_kernel,
        grid=(b, cols // tn),
        in_specs=[big, big, pl.BlockSpec((n1h, 2 * n1), lambda bi_, j: (0, 0)), half, half,
                  pl.BlockSpec((1, tn), lambda bi_, j: (0, 0))],
        out_specs=half,
        out_shape=jax.ShapeDtypeStruct((b, n1h, cols), BF16),
        compiler_params=_params(("parallel", "parallel")),
        name="fft_stage3",
    )(br, bi, inv1, u3, x03, skip_t)


def _hyena_long_conv(u, x0, hf, hb, skip, batch, seq_len):
    c = HY_WIDTH
    fwd1, inv1, f2, twr, twi, n1, n2 = _dft_tables(seq_len)
    n1h = n1 // 2
    fa_r, fa_i = _fft1(jnp.stack([hf, hb]).reshape(2, n1h, n2 * c), fwd1)
    kr, ki = _fft2_filter(fa_r.reshape(2, n1, n2, c), fa_i.reshape(2, n1, n2, c), twr, twi, f2)
    u3 = u.reshape(batch, n1h, n2 * c)
    a_r, a_i = _fft1(u3, fwd1)
    b_r, b_i = _fft2(a_r.reshape(batch, n1, n2, c), a_i.reshape(batch, n1, n2, c),
                     twr, twi, kr, ki, f2)
    tn = 2048
    skip_t = jnp.tile(skip[None, :], (1, tn // c))
    yh = _fft3(b_r.reshape(batch, n1, n2 * c), b_i.reshape(batch, n1, n2 * c), inv1,
               u3, x0.reshape(batch, n1h, n2 * c), skip_t, tn)
    return yh.reshape(batch * seq_len, c)


def _attn_kernel(q1_ref, q2_ref, k_ref, vt_ref, lam_ref, sg_ref, o_ref,
                 m1, a1, m2, a2, *, chunk):
    ki = pl.program_id(3)

    @pl.when(ki == 0)
    def _():
        for m, a in ((m1, a1), (m2, a2)):
            m[...] = jnp.full(m.shape, NEG_INF, F32)
            a[...] = jnp.zeros(a.shape, F32)

    q_refs = (q1_ref, q2_ref)
    stats = [[m1[...], a1[...]], [m2[...], a2[...]]]
    n_chunks = k_ref.shape[0] // chunk
    units = [(c, comp) for c in range(n_chunks) for comp in (0, 1)]

    def scores(unit):
        c, comp = unit
        return jnp.dot(k_ref[c * chunk:(c + 1) * chunk, :], q_refs[comp][...],
                       preferred_element_type=F32)

    def softmax(unit, s):
        m_prev = stats[unit[1]][0]
        m_new = jnp.maximum(m_prev, jnp.max(s, axis=0, keepdims=True))
        stats[unit[1]][0] = m_new
        return jnp.exp2(s - m_new).astype(BF16), jnp.exp2(m_prev - m_new)

    def values(unit, p, alpha):
        c, comp = unit
        stats[comp][1] = alpha * stats[comp][1] + jnp.dot(
            vt_ref[:, c * chunk:(c + 1) * chunk], p, preferred_element_type=F32)

    s_next = scores(units[0])
    pending = None
    for i, unit in enumerate(units):
        s_cur = s_next
        if i + 1 < len(units):
            s_next = scores(units[i + 1])
        if pending is not None:
            values(*pending)
        pending = (unit,) + softmax(unit, s_cur)
    values(*pending)
    for (m, a), (mv, av) in zip(((m1, a1), (m2, a2)), stats):
        m[...] = mv
        a[...] = av

    @pl.when(ki == pl.num_programs(3) - 1)
    def _():
        lv = lam_ref[...]
        lam = (jnp.exp(jnp.sum(lv[0:1] * lv[1:2], axis=1, keepdims=True))
               - jnp.exp(jnp.sum(lv[2:3] * lv[3:4], axis=1, keepdims=True)) + LAMBDA_INIT)
        d = ATT_V_DIM
        o = (a1[:d, :] / a1[d:d + 1, :] - lam * (a2[:d, :] / a2[d:d + 1, :]))
        inv = lax.rsqrt(jnp.mean(o * o, axis=0, keepdims=True) + SUBLN_EPS)
        g = pltpu.repeat(sg_ref[...], o.shape[1] // LANES, 1)
        o_ref[...] = (o * inv * g * (1.0 - LAMBDA_INIT)).astype(BF16)


def _attention(qt1, qt2, k, vt, lam_vecs, subln_g, batch, seq_len, tq=512, tk=2048, chunk=512):
    tk = min(tk, seq_len)
    nq, nk = seq_len // tq, seq_len // tk
    q_spec = pl.BlockSpec((LANES, tq), lambda b, h, i, j: (h, b * nq + i))
    sg = jnp.broadcast_to(subln_g[:, None], (ATT_V_DIM, LANES))
    return pl.pallas_call(
        functools.partial(_attn_kernel, chunk=chunk),
        grid=(batch, N_ATT_HEADS, nq, nk),
        in_specs=[q_spec, q_spec,
                  pl.BlockSpec((tk, LANES), lambda b, h, i, j: (b * nk + j, h)),
                  pl.BlockSpec((VT_ROWS, tk), lambda b, h, i, j: (h, b * nk + j)),
                  pl.BlockSpec((4, ATT_HEAD_DIM), lambda b, h, i, j: (0, 0)),
                  pl.BlockSpec((ATT_V_DIM, LANES), lambda b, h, i, j: (0, 0))],
        out_specs=q_spec,
        out_shape=jax.ShapeDtypeStruct((ATT_WIDTH, batch * seq_len), BF16),
        scratch_shapes=[pltpu.VMEM((1, tq), F32), pltpu.VMEM((VT_ROWS, tq), F32)] * 2,
        compiler_params=_params(("parallel", "parallel", "parallel", "arbitrary")),
        name="diff_attention",
    )(qt1, qt2, k, vt, lam_vecs, sg)


def _split_bf16(x):
    hi = x.astype(BF16)
    return hi, (x - hi.astype(F32)).astype(BF16)


def _outproj_kernel(yh_ref, ya_ref, x_ref, wo_ref, g_ref, wqh_ref, wql_ref, kh_ref, kl_ref,
                    x2_ref, xn_ref, st_ref):
    wo = wo_ref[...]
    x2 = (x_ref[...] + jnp.dot(yh_ref[...], wo[:HY_WIDTH], preferred_element_type=F32)
          + lax.dot_general(ya_ref[...], wo[HY_WIDTH:], (((0,), (0,)), ((), ())),
                            preferred_element_type=F32))
    x2_ref[...] = x2
    xn = _rms(x2, g_ref[...], EPS)
    xh, xl = _split_bf16(xn)
    xn_ref[...] = xh
    q = (jnp.dot(xh, wqh_ref[...], preferred_element_type=F32)
         + jnp.dot(xl, wqh_ref[...], preferred_element_type=F32)
         + jnp.dot(xh, wql_ref[...], preferred_element_type=F32))
    for hc in range(2 * PEER_HEADS):
        qh, ql = _split_bf16(q[:, hc * PEER_HALF:(hc + 1) * PEER_HALF])
        kh, kl = kh_ref[hc], kl_ref[hc]
        st_ref[hc * PEER_KEYS:(hc + 1) * PEER_KEYS, :] = (
            _dot_nt(kh, qh) + _dot_nt(kh, ql) + _dot_nt(kl, qh))


def _outproj(yh, ya, x, wo_bf, g2, wq_hi, wq_lo, keys_hi, keys_lo, tm=512):
    t = x.shape[0]
    qw = 2 * PEER_HEADS * PEER_HALF
    sw = 2 * PEER_HEADS * PEER_KEYS
    row = lambda i: (i, 0)
    fixed = lambda i: (0, 0)
    fixed3 = lambda i: (0, 0, 0)
    return pl.pallas_call(
        _outproj_kernel,
        grid=(t // tm,),
        in_specs=[pl.BlockSpec((tm, HY_WIDTH), row),
                  pl.BlockSpec((ATT_WIDTH, tm), lambda i: (0, i)),
                  pl.BlockSpec((tm, D_MODEL), row),
                  pl.BlockSpec((HY_WIDTH + ATT_WIDTH, D_MODEL), fixed),
                  pl.BlockSpec((1, D_MODEL), fixed),
                  pl.BlockSpec((D_MODEL, qw), fixed), pl.BlockSpec((D_MODEL, qw), fixed),
                  pl.BlockSpec((2 * PEER_HEADS, PEER_KEYS, PEER_HALF), fixed3),
                  pl.BlockSpec((2 * PEER_HEADS, PEER_KEYS, PEER_HALF), fixed3)],
        out_specs=[pl.BlockSpec((tm, D_MODEL), row), pl.BlockSpec((tm, D_MODEL), row),
                   pl.BlockSpec((sw, tm), lambda i: (0, i))],
        out_shape=[jax.ShapeDtypeStruct((t, D_MODEL), F32),
                   jax.ShapeDtypeStruct((t, D_MODEL), BF16),
                   jax.ShapeDtypeStruct((sw, t), F32)],
        compiler_params=_params(("parallel",)),
        name="outproj_peer_query",
    )(yh, ya, x, wo_bf, g2, wq_hi, wq_lo, keys_hi, keys_lo)


def _extract_max(pieces, iotas, total, exact):
    m = pieces[0]
    for p in pieces[1:]:
        m = jnp.maximum(m, p)
    m = jnp.max(m, axis=0, keepdims=True)
    hot = [p == m for p in pieces]
    if exact:
        cand = [jnp.where(h, io, total) for h, io in zip(hot, iotas)]
        idx = cand[0]
        for c in cand[1:]:
            idx = jnp.minimum(idx, c)
        idx = jnp.min(idx, axis=0, keepdims=True)
        hot = [io == idx for io in iotas]
    return m, hot, [jnp.where(h, NEG_INF, p) for h, p in zip(hot, pieces)]


def _topk_rank(s, exact):
    rank = jnp.full(s.shape, float(PEER_TOPK), F32)
    iota = [lax.broadcasted_iota(jnp.int32, s.shape, 0)]
    vals = []
    for r in range(PEER_TOPK):
        m, hot, (s,) = _extract_max([s], iota, s.shape[0], exact)
        vals.append(m)
        rank = jnp.where(hot[0], float(r), rank)
    return vals, rank


_PAIRS = [(k0, k1) for k0 in range(PEER_TOPK) for k1 in range(PEER_TOPK // (k0 + 1))]
_PAIR_ROWS = -(-len(_PAIRS) // SUBLANES) * SUBLANES


def _rows_to_tiles(rows, width):
    sub = lax.broadcasted_iota(jnp.int32, (SUBLANES, width), 0)
    tiles = []
    for t0 in range(0, len(rows), SUBLANES):
        tile = jnp.full((SUBLANES, width), NEG_INF, F32)
        for r in range(SUBLANES):
            if rows[t0 + r] is not None:
                tile = jnp.where(sub == r, rows[t0 + r], tile)
        tiles.append(tile)
    return tiles


def _route_tables(s0, s1, exact):
    tt = s0.shape[1]
    v0, rank0 = _topk_rank(s0, exact)
    v1, rank1 = _topk_rank(s1, exact)
    pad = [None] * (_PAIR_ROWS - len(_PAIRS))
    tiles = [a + b for a, b in zip(_rows_to_tiles([v0[k0] for k0, _ in _PAIRS] + pad, tt),
                                   _rows_to_tiles([v1[k1] for _, k1 in _PAIRS] + pad, tt))]
    iotas = [lax.broadcasted_iota(jnp.int32, (SUBLANES, tt), 0) + t0
             for t0 in range(0, _PAIR_ROWS, SUBLANES)]
    cmax = v0[0] + v1[0]
    z = None
    for r in range(PEER_TOPK):
        m, _, tiles = _extract_max(tiles, iotas, _PAIR_ROWS, exact)
        z = jnp.exp(m - cmax) if z is None else z + jnp.exp(m - cmax)
    cnt = jnp.zeros(s0.shape, F32)
    total = None
    for k0 in range(PEER_TOPK):
        rows = [r for r, (a, _) in enumerate(_PAIRS) if a == k0]
        taken = None
        for ti in sorted({r // SUBLANES for r in rows}):
            inside = (iotas[ti] >= rows[0]) & (iotas[ti] <= rows[-1])
            part = jnp.sum(jnp.where(inside & (tiles[ti] == NEG_INF), 1.0, 0.0),
                           axis=0, keepdims=True)
            taken = part if taken is None else taken + part
        total = taken if total is None else total + taken
        cnt = jnp.where(rank0 == float(k0), taken, cnt)
    k = float(PEER_TOPK)
    ranked = lambda rank: jnp.sum(jnp.where(rank < k, 1.0, 0.0), axis=0, keepdims=True)
    ok = (total == k) & (ranked(rank0) == k) & (ranked(rank1) == k)
    return cnt, jnp.exp(s0 - v0[0]) / z, rank1, jnp.exp(s1 - v1[0]), ok


def _route_kernel(st_ref, cnt_ref, w0_ref, r1_ref, e1_ref):
    def emit(exact):
        cnt, w0, rank1, e1, ok = _route_tables(st_ref[:PEER_KEYS, :], st_ref[PEER_KEYS:, :], exact)
        cnt_ref[0] = cnt
        w0_ref[0] = w0
        r1_ref[0] = rank1.astype(BF16)
        e1_ref[0] = e1.astype(BF16)
        return ok

    ok = emit(exact=False)
    clean = jnp.min(jnp.where(ok, 1, 0))

    @pl.when(clean == 0)
    def _():
        emit(exact=True)


def _route(st, tt=512):
    sw, t = st.shape
    o_spec = pl.BlockSpec((1, PEER_KEYS, tt), lambda i, h: (h, 0, i))
    tab = lambda dt: jax.ShapeDtypeStruct((PEER_HEADS, PEER_KEYS, t), dt)
    return pl.pallas_call(
        _route_kernel,
        grid=(t // tt, PEER_HEADS),
        in_specs=[pl.BlockSpec((2 * PEER_KEYS, tt), lambda i, h: (h, i))],
        out_specs=[o_spec] * 4,
        out_shape=[tab(F32), tab(F32), tab(BF16), tab(BF16)],
        compiler_params=_params(("parallel", "parallel")),
        name="peer_route",
    )(st)


def _gelu(x):
    return 0.5 * x * (1.0 + lax.erf(x * (2.0 ** -0.5)))


def _peer_kernel(xn_ref, u_ref, vt_ref, cnt_ref, w0_ref, r1_ref, e1_ref, x2_ref, gf_ref,
                 y_ref, acc_ref, a_ref, *, rows_per_step):
    e = pl.program_id(1)

    @pl.when(e == 0)
    def _():
        acc_ref[...] = jnp.zeros(acc_ref.shape, F32)

    hid = _dot_nt(u_ref[...], xn_ref[...])
    tt = hid.shape[1]
    for ii in range(rows_per_step):
        g = None
        for h in range(PEER_HEADS):
            cnt_row = jnp.broadcast_to(cnt_ref[h, ii:ii + 1, :], (PEER_KEYS, tt)).astype(BF16)
            w_row = jnp.broadcast_to(w0_ref[h, ii:ii + 1, :], (PEER_KEYS, tt)).astype(BF16)
            term = jnp.where(r1_ref[h] < cnt_row, e1_ref[h] * w_row, jnp.zeros((), BF16))
            g = term if g is None else g + term
        lo, hi = ii * PEER_KEYS, (ii + 1) * PEER_KEYS
        a_ref[lo:hi, :] = _gelu(hid[lo:hi]).astype(BF16) * g
    acc_ref[...] += jnp.dot(vt_ref[...], a_ref[...], preferred_element_type=F32)

    @pl.when(e == pl.num_programs(1) - 1)
    def _():
        x3 = x2_ref[...] + acc_ref[...].T
        y_ref[...] = _rms(x3, gf_ref[...], EPS)


def _peer(xn, u_bf, vt_bf, cnt, w0, r1, e1, x2, gf, tt=512, rows_per_step=8):
    t = xn.shape[0]
    te = rows_per_step * PEER_KEYS
    tok = lambda i, e: (i, 0)
    tab_i = pl.BlockSpec((PEER_HEADS, rows_per_step, tt), lambda i, e: (0, e, i))
    tab_j = pl.BlockSpec((PEER_HEADS, PEER_KEYS, tt), lambda i, e: (0, 0, i))
    return pl.pallas_call(
        functools.partial(_peer_kernel, rows_per_step=rows_per_step),
        grid=(t // tt, PEER_EXPERTS // te),
        in_specs=[pl.BlockSpec((tt, D_MODEL), tok),
                  pl.BlockSpec((te, D_MODEL), lambda i, e: (e, 0)),
                  pl.BlockSpec((D_MODEL, te), lambda i, e: (0, e)),
                  tab_i, tab_i, tab_j, tab_j,
                  pl.BlockSpec((tt, D_MODEL), tok),
                  pl.BlockSpec((1, D_MODEL), lambda i, e: (0, 0))],
        out_specs=pl.BlockSpec((tt, D_MODEL), tok),
        out_shape=jax.ShapeDtypeStruct((t, D_MODEL), F32),
        scratch_shapes=[pltpu.VMEM((D_MODEL, tt), F32), pltpu.VMEM((te, tt), BF16)],
        compiler_params=_params(("parallel", "arbitrary")),
        name="peer_dense",
    )(xn, u_bf, vt_bf, cnt, w0, r1, e1, x2, gf)


def _rope_tables(seq_len):
    inv = ROPE_THETA ** (-jnp.arange(0, ROT_DIM, 2, dtype=F32) / ROT_DIM)
    ang = jnp.arange(seq_len, dtype=F32)[:, None] * inv[None, :]
    cos, sin = jnp.cos(ang), jnp.sin(ang)
    half = ROT_DIM // 2
    one = jnp.ones((seq_len, ATT_HEAD_DIM - ROT_DIM), F32)
    zero = jnp.zeros((seq_len, ATT_HEAD_DIM - half), F32)
    c = jnp.concatenate([cos, cos, one], axis=1)
    sa = jnp.concatenate([-sin, zero], axis=1)
    sb = jnp.concatenate([jnp.zeros((seq_len, half), F32), sin,
                          jnp.zeros((seq_len, ATT_HEAD_DIM - ROT_DIM), F32)], axis=1)
    both = lambda a: jnp.concatenate([a, a], axis=1)
    return both(c), both(sa), both(sb)


def _trunk(x, p):
    batch, seq_len, _ = x.shape
    xf = x.reshape(batch * seq_len, D_MODEL)
    rope_c, rope_sa, rope_sb = _rope_tables(seq_len)
    zh, qt1, qt2, k, vt = _inproj(xf, p["norm1_g"], p["w_in"], rope_c, rope_sa, rope_sb, seq_len)
    u, x0 = _shortconv(zh, p["hy_conv_w"], p["hy_conv_b"], seq_len)
    hf, hb = _hyena_filters(seq_len, *p["filter"])
    yh = _hyena_long_conv(u, x0, hf, hb, p["hy_skip"], batch, seq_len)
    ya = _attention(qt1, qt2, k, vt, p["lam_vecs"], p["subln_g"], batch, seq_len)
    x2, xn, st = _outproj(yh, ya, xf, p["w_out"], p["norm2_g"], p["wq_hi"], p["wq_lo"],
                          p["keys_hi"], p["keys_lo"])
    cnt, w0, r1, e1 = _route(st)
    y = _peer(xn, p["peer_u"], p["peer_vt"], cnt, w0, r1, e1, x2, p["normf_g"])
    return y.reshape(batch, seq_len, D_MODEL)


def kernel(x_prompt, x_sample, norm1_g, w_in, hy_conv_w, hy_conv_b, hf_w1, hf_b1, hf_freq1,
           hf_w2, hf_b2, hf_freq2, hf_w3, hf_b3, hy_skip, lam_q1, lam_k1, lam_q2, lam_k2,
           subln_g, w_out, norm2_g, peer_wq, peer_keys, peer_u, peer_v, normf_g):
    l = 0
    wq = peer_wq[l]
    wq_hi = wq.astype(BF16)
    keys = peer_keys[l].reshape(2 * PEER_HEADS, PEER_KEYS, PEER_HALF)
    keys_hi = keys.astype(BF16)
    p = {
        "norm1_g": norm1_g[l][None, :],
        "w_in": w_in[l].astype(BF16),
        "hy_conv_w": hy_conv_w[l],
        "hy_conv_b": hy_conv_b[l][None, :],
        "filter": (hf_w1[l], hf_b1[l], hf_freq1[l], hf_w2[l], hf_b2[l], hf_freq2[l],
                   hf_w3[l], hf_b3[l]),
        "hy_skip": hy_skip[l],
        "lam_vecs": jnp.stack([lam_q1[l], lam_k1[l], lam_q2[l], lam_k2[l]]),
        "subln_g": subln_g[l],
        "w_out": w_out[l].astype(BF16),
        "norm2_g": norm2_g[l][None, :],
        "wq_hi": wq_hi,
        "wq_lo": (wq - wq_hi.astype(F32)).astype(BF16),
        "keys_hi": keys_hi,
        "keys_lo": (keys - keys_hi.astype(F32)).astype(BF16),
        "peer_u": peer_u[l].astype(BF16),
        "peer_vt": peer_v[l].astype(BF16).T,
        "normf_g": normf_g[None, :],
    }
    return (_trunk(x_prompt, p), _trunk(x_sample, p))
```

```python
import functools
import math

import jax
import jax.numpy as jnp
import numpy as np
from jax import lax
from jax.experimental import pallas as pl
from jax.experimental.pallas import tpu as pltpu

F32 = jnp.float32
BF16 = jnp.bfloat16

D_MODEL = 1024
HY_WIDTH = 512
ATT_WIDTH = 512
N_ATT_HEADS = 4
ATT_HEAD_DIM = 64
ATT_V_DIM = 2 * ATT_HEAD_DIM
ROT_DIM = ATT_HEAD_DIM // 4
ROPE_THETA = 500000.0
IN_WIDTH = 3 * HY_WIDTH + 3 * ATT_WIDTH
FILTER_BANDS = 8
FILTER_EMB = 1 + 2 * FILTER_BANDS
FILTER_HIDDEN = 64
FAST_DECAY_PCT = 0.3
SLOW_DECAY_PCT = 1.5
DECAY_TARGET = 1e-2
PEER_HEADS = 8
PEER_KEYS = 128
PEER_EXPERTS = PEER_KEYS * PEER_KEYS
PEER_TOPK = 16
PEER_HALF = 128
EPS = 1e-6
SUBLN_EPS = 1e-5
LAMBDA_INIT = 0.8 - 0.6 * math.exp(-0.3 * 0)

LANES = 128
SUBLANES = 8
VMEM_LIMIT = 56 * 1024 * 1024
FFT_N2 = 128
VT_ROWS = ATT_V_DIM + 2 * SUBLANES
NEG_INF = float("-inf")


def _params(semantics, vmem=VMEM_LIMIT):
    return pltpu.CompilerParams(dimension_semantics=semantics, vmem_limit_bytes=vmem)


def _rms(x, g, eps):
    return x * lax.rsqrt(jnp.mean(x * x, axis=-1, keepdims=True) + eps) * g


def _dot_nt(a, b):
    return lax.dot_general(a, b, (((1,), (1,)), ((), ())), preferred_element_type=F32)


def _inproj_kernel(x_ref, g_ref, w_ref, c_ref, sa_ref, sb_ref,
                   zh_ref, qt1_ref, qt2_ref, k_ref, vt_ref):
    h = _rms(x_ref[...], g_ref[...], EPS).astype(BF16)
    z = jnp.dot(h, w_ref[...], preferred_element_type=F32)
    hy3 = 3 * HY_WIDTH
    zh_ref[...] = z[:, :hy3]
    c, sa, sb = c_ref[...], sa_ref[...], sb_ref[...]
    scale = ATT_HEAD_DIM ** -0.5 * math.log2(math.e)

    def rope(xb):
        return (xb * c + pltpu.roll(xb, LANES - ROT_DIM // 2, 1) * sa
                + pltpu.roll(xb, ROT_DIM // 2, 1) * sb)

    first = lax.broadcasted_iota(jnp.int32, (LANES, z.shape[0]), 0) < ATT_HEAD_DIM
    for hd in range(N_ATT_HEADS):
        lo, hi = hd * LANES, (hd + 1) * LANES
        qt = (rope(z[:, hy3 + lo:hy3 + hi]) * scale).T
        qt1_ref[lo:hi, :] = jnp.where(first, qt, 0.0).astype(BF16)
        qt2_ref[lo:hi, :] = jnp.where(first, 0.0, qt).astype(BF16)
        k_ref[:, lo:hi] = rope(z[:, hy3 + ATT_WIDTH + lo:hy3 + ATT_WIDTH + hi]).astype(BF16)
        vlo = hd * VT_ROWS
        vt_ref[vlo:vlo + ATT_V_DIM, :] = (
            z[:, hy3 + 2 * ATT_WIDTH + lo:hy3 + 2 * ATT_WIDTH + hi].T.astype(BF16))
        vt_ref[vlo + ATT_V_DIM:vlo + VT_ROWS, :] = jnp.ones(
            (VT_ROWS - ATT_V_DIM, z.shape[0]), BF16)


def _inproj(x, g, w_bf, rope_c, rope_sa, rope_sb, seq_len, tm=512):
    t = x.shape[0]
    nseq = seq_len // tm
    row = lambda i: (i, 0)
    col = lambda i: (0, i)
    fixed = lambda i: (0, 0)
    pos = lambda i: (i % nseq, 0)
    hy3 = 3 * HY_WIDTH
    tr_spec = pl.BlockSpec((ATT_WIDTH, tm), col)
    tr_shape = jax.ShapeDtypeStruct((ATT_WIDTH, t), BF16)
    return pl.pallas_call(
        _inproj_kernel,
        grid=(t // tm,),
        in_specs=[pl.BlockSpec((tm, D_MODEL), row),
                  pl.BlockSpec((1, D_MODEL), fixed),
                  pl.BlockSpec((D_MODEL, IN_WIDTH), fixed),
                  pl.BlockSpec((tm, LANES), pos),
                  pl.BlockSpec((tm, LANES), pos),
                  pl.BlockSpec((tm, LANES), pos)],
        out_specs=[pl.BlockSpec((tm, hy3), row), tr_spec, tr_spec,
                   pl.BlockSpec((tm, ATT_WIDTH), row),
                   pl.BlockSpec((N_ATT_HEADS * VT_ROWS, tm), col)],
        out_shape=[jax.ShapeDtypeStruct((t, hy3), F32), tr_shape, tr_shape,
                   jax.ShapeDtypeStruct((t, ATT_WIDTH), BF16),
                   jax.ShapeDtypeStruct((N_ATT_HEADS * VT_ROWS, t), BF16)],
        compiler_params=_params(("parallel",)),
        name="inproj",
    )(x, g, w_bf, rope_c, rope_sa, rope_sb)


def _shortconv_kernel(z_ref, zp_ref, zn_ref, w_ref, b_ref, u_ref, x0_ref, *, blocks_per_seq):
    i = pl.program_id(0)
    z = z_ref[...]
    tl = z.shape[0]
    row = lax.broadcasted_iota(jnp.int32, z.shape, 0)
    at_start = (i % blocks_per_seq) == 0
    at_end = (i % blocks_per_seq) == blocks_per_seq - 1
    prev_row = jnp.where(at_start, 0.0, zp_ref[SUBLANES - 1:SUBLANES, :])
    next_row = jnp.where(at_end, 0.0, zn_ref[0:1, :])
    zm = jnp.where(row == 0, prev_row, pltpu.roll(z, 1, 0))
    zp = jnp.where(row == tl - 1, next_row, pltpu.roll(z, tl - 1, 0))
    w = w_ref[...]
    conv = zm * w[0:1] + z * w[1:2] + zp * w[2:3] + b_ref[...]
    x0_ref[...] = conv[:, :HY_WIDTH]
    u_ref[...] = conv[:, 2 * HY_WIDTH:] * conv[:, HY_WIDTH:2 * HY_WIDTH]


def _shortconv(zh, w, b, seq_len, tl=512):
    t, hy3 = zh.shape
    per8 = tl // SUBLANES
    last8 = t // SUBLANES - 1
    return pl.pallas_call(
        functools.partial(_shortconv_kernel, blocks_per_seq=seq_len // tl),
        grid=(t // tl,),
        in_specs=[pl.BlockSpec((tl, hy3), lambda i: (i, 0)),
                  pl.BlockSpec((SUBLANES, hy3), lambda i: (jnp.maximum(i * per8 - 1, 0), 0)),
                  pl.BlockSpec((SUBLANES, hy3), lambda i: (jnp.minimum((i + 1) * per8, last8), 0)),
                  pl.BlockSpec((3, hy3), lambda i: (0, 0)),
                  pl.BlockSpec((1, hy3), lambda i: (0, 0))],
        out_specs=[pl.BlockSpec((tl, HY_WIDTH), lambda i: (i, 0))] * 2,
        out_shape=[jax.ShapeDtypeStruct((t, HY_WIDTH), F32)] * 2,
        compiler_params=_params(("parallel",)),
        name="shortconv",
    )(zh, zh, zh, w, b)


def _filter_kernel(f_ref, w1_ref, b1_ref, f1_ref, w2_ref, b2_ref, f2_ref, w3_ref, b3_ref,
                   dl_ref, h_ref):
    i = pl.program_id(0)
    hp = lax.Precision.HIGHEST
    feats = f_ref[...]
    tl = feats.shape[0]
    h = jnp.sin(f1_ref[...] * (jnp.dot(feats, w1_ref[...], precision=hp,
                                       preferred_element_type=F32) + b1_ref[...]))
    h = jnp.sin(f2_ref[...] * (jnp.dot(h, w2_ref[...], precision=hp,
                                       preferred_element_type=F32) + b2_ref[...]))
    h3 = jnp.dot(h, w3_ref[...], precision=hp, preferred_element_type=F32) + b3_ref[...]
    decay = jnp.exp(-feats[:, 0:1] * dl_ref[...])
    h_ref[0] = h3[:, :HY_WIDTH] * decay
    row = lax.broadcasted_iota(jnp.int32, (tl, HY_WIDTH), 0) + i * tl
    h_ref[1] = jnp.where(row == 0, 0.0, h3[:, HY_WIDTH:] * decay)


def _pad2(a, rows, cols):
    return jnp.pad(a, ((0, rows - a.shape[0]), (0, cols - a.shape[1])))


def _hyena_filters(seq_len, w1, b1, f1, w2, b2, f2, w3, b3, tl=512):
    t = jnp.linspace(0.0, 1.0, seq_len, dtype=F32)[:, None]
    w = 2.0 * math.pi * jnp.arange(seq_len, dtype=F32) / seq_len
    bands = jnp.linspace(1e-4, FILTER_BANDS - 1, FILTER_BANDS, dtype=F32)
    ang = w[:, None] * bands[None, :]
    feats = _pad2(jnp.concatenate([t, jnp.cos(ang), -jnp.sin(ang)], axis=-1), seq_len, LANES)
    max_decay = math.log(DECAY_TARGET) / FAST_DECAY_PCT
    min_decay = math.log(DECAY_TARGET) / SLOW_DECAY_PCT
    deltas = jnp.abs(jnp.linspace(min_decay, max_decay, HY_WIDTH, dtype=F32))[None, :]
    w1p = _pad2(w1, LANES, LANES)
    w2p = _pad2(w2, LANES, LANES)
    w3p = _pad2(w3, LANES, 2 * HY_WIDTH)
    pad1 = lambda v: _pad2(v[None, :], 1, LANES)
    fixed = lambda i: (0, 0)
    vec = pl.BlockSpec((1, LANES), fixed)
    mat = pl.BlockSpec((LANES, LANES), fixed)
    return pl.pallas_call(
        _filter_kernel,
        grid=(seq_len // tl,),
        in_specs=[pl.BlockSpec((tl, LANES), lambda i: (i, 0)), mat, vec, vec, mat, vec, vec,
                  pl.BlockSpec((LANES, 2 * HY_WIDTH), fixed),
                  pl.BlockSpec((1, 2 * HY_WIDTH), fixed),
                  pl.BlockSpec((1, HY_WIDTH), fixed)],
        out_specs=pl.BlockSpec((2, tl, HY_WIDTH), lambda i: (0, i, 0)),
        out_shape=jax.ShapeDtypeStruct((2, seq_len, HY_WIDTH), F32),
        compiler_params=_params(("parallel",)),
        name="hyena_filter",
    )(feats, w1p, pad1(b1), pad1(f1), w2p, pad1(b2), pad1(f2), w3p, b3[None, :], deltas)


def _dft_tables(seq_len):
    n = 2 * seq_len
    n2 = FFT_N2
    n1 = n // n2
    n1h = n1 // 2
    a1 = (2.0 * np.pi / n1) * np.outer(np.arange(n1), np.arange(n1))
    f1r, f1i = np.cos(a1), -np.sin(a1)
    fwd1 = np.concatenate([f1r[:, :n1h], f1i[:, :n1h]], axis=0)
    inv1 = np.concatenate([f1r[:n1h, :], f1i[:n1h, :]], axis=1) / n
    a2 = (2.0 * np.pi / n2) * np.outer(np.arange(n2), np.arange(n2))
    f2 = np.concatenate([np.cos(a2), -np.sin(a2)], axis=0)
    at = (2.0 * np.pi / n) * np.outer(np.arange(n1), np.arange(n2))
    twr = np.broadcast_to(np.cos(at)[:, :, None], (n1, n2, LANES))
    twi = np.broadcast_to(-np.sin(at)[:, :, None], (n1, n2, LANES))
    return (jnp.asarray(fwd1, BF16), jnp.asarray(inv1, BF16), jnp.asarray(f2, BF16),
            jnp.asarray(twr, F32), jnp.asarray(twi, F32), n1, n2)


def _fft1_kernel(u_ref, f_ref, ar_ref, ai_ref):
    r = jnp.dot(f_ref[...], u_ref[0].astype(BF16), preferred_element_type=F32)
    n1 = r.shape[0] // 2
    ar_ref[0] = r[:n1].astype(BF16)
    ai_ref[0] = r[n1:].astype(BF16)


def _fft1(u3, fwd1, tn=2048):
    b, n1h, cols = u3.shape
    n1 = 2 * n1h
    return pl.pallas_call(
        _fft1_kernel,
        grid=(b, cols // tn),
        in_specs=[pl.BlockSpec((1, n1h, tn), lambda bi, j: (bi, 0, j)),
                  pl.BlockSpec((2 * n1, n1h), lambda bi, j: (0, 0))],
        out_specs=[pl.BlockSpec((1, n1, tn), lambda bi, j: (bi, 0, j))] * 2,
        out_shape=[jax.ShapeDtypeStruct((b, n1, cols), BF16)] * 2,
        compiler_params=_params(("parallel", "parallel")),
        name="fft_stage1",
    )(u3, fwd1)


def _tile_lanes(x, width):
    return jnp.concatenate([x] * (width // x.shape[1]), axis=1)


def _dft2_forward(ar, ai, twr, twi, f2):
    n2, c = ar.shape
    ar, ai = ar.astype(F32), ai.astype(F32)
    tr, ti = _tile_lanes(twr, c), _tile_lanes(twi, c)
    cat = jnp.concatenate([ar * tr - ai * ti, ar * ti + ai * tr], axis=1).astype(BF16)
    p = jnp.dot(f2, cat, preferred_element_type=F32)
    return p[:n2, :c] - p[n2:, c:], p[:n2, c:] + p[n2:, :c]


def _fft2_filter_kernel(ar_ref, ai_ref, twr_ref, twi_ref, f2_ref, kr_ref, ki_ref, *, kb):
    f2 = f2_ref[...]
    for j in range(kb):
        fr, fi = _dft2_forward(ar_ref[0, j], ai_ref[0, j], twr_ref[j], twi_ref[j], f2)
        gr, gi = _dft2_forward(ar_ref[1, j], ai_ref[1, j], twr_ref[j], twi_ref[j], f2)
        kr_ref[j] = fr + gr
        ki_ref[j] = fi - gi


def _fft2_filter(ar, ai, twr, twi, f2, kb=8):
    _, n1, n2, c = ar.shape
    a_spec = pl.BlockSpec((2, kb, n2, c), lambda k: (0, k, 0, 0))
    t_spec = pl.BlockSpec((kb, n2, LANES), lambda k: (k, 0, 0))
    o_spec = pl.BlockSpec((kb, n2, c), lambda k: (k, 0, 0))
    return pl.pallas_call(
        functools.partial(_fft2_filter_kernel, kb=kb),
        grid=(n1 // kb,),
        in_specs=[a_spec, a_spec, t_spec, t_spec, pl.BlockSpec((2 * n2, n2), lambda k: (0, 0))],
        out_specs=[o_spec, o_spec],
        out_shape=[jax.ShapeDtypeStruct((n1, n2, c), F32)] * 2,
        compiler_params=_params(("parallel",)),
        name="fft_filter_spectrum",
    )(ar, ai, twr, twi, f2)


def _fft2_kernel(ar_ref, ai_ref, twr_ref, twi_ref, kr_ref, ki_ref, f2_ref, br_ref, bi_ref, *, kb):
    f2 = f2_ref[...]
    for j in range(kb):
        twr, twi = twr_ref[j], twi_ref[j]
        xr, xi = _dft2_forward(ar_ref[0, j], ai_ref[0, j], twr, twi, f2)
        n2, c = xr.shape
        kr, ki = kr_ref[j], ki_ref[j]
        cat = jnp.concatenate([xr * kr - xi * ki, xr * ki + xi * kr], axis=1).astype(BF16)
        p = jnp.dot(f2, cat, preferred_element_type=F32)
        b_r = p[:n2, :c] + p[n2:, c:]
        b_i = p[:n2, c:] - p[n2:, :c]
        tr, ti = _tile_lanes(twr, c), _tile_lanes(twi, c)
        br_ref[0, j] = (b_r * tr + b_i * ti).astype(BF16)
        bi_ref[0, j] = (b_i * tr - b_r * ti).astype(BF16)


def _fft2(ar, ai, twr, twi, kr, ki, f2, kb=8):
    b, n1, n2, c = ar.shape
    a_spec = pl.BlockSpec((1, kb, n2, c), lambda bi, k: (bi, k, 0, 0))
    t_spec = pl.BlockSpec((kb, n2, LANES), lambda bi, k: (k, 0, 0))
    k_spec = pl.BlockSpec((kb, n2, c), lambda bi, k: (k, 0, 0))
    return pl.pallas_call(
        functools.partial(_fft2_kernel, kb=kb),
        grid=(b, n1 // kb),
        in_specs=[a_spec, a_spec, t_spec, t_spec, k_spec, k_spec,
                  pl.BlockSpec((2 * n2, n2), lambda bi, k: (0, 0))],
        out_specs=[a_spec, a_spec],
        out_shape=[jax.ShapeDtypeStruct((b, n1, n2, c), BF16)] * 2,
        compiler_params=_params(("parallel", "parallel")),
        name="fft_stage2",
    )(ar, ai, twr, twi, kr, ki, f2)


def _fft3_kernel(br_ref, bi_ref, g_ref, u_ref, x0_ref, skip_ref, y_ref):
    g = g_ref[...]
    n1 = g.shape[1] // 2
    y = (jnp.dot(g[:, :n1], br_ref[0], preferred_element_type=F32)
         + jnp.dot(g[:, n1:], bi_ref[0], preferred_element_type=F32))
    y_ref[0] = (x0_ref[0] * (y + u_ref[0] * skip_ref[...])).astype(BF16)


def _fft3(br, bi, inv1, u3, x03, skip_t, tn=2048):
    b, n1, cols = br.shape
    n1h = n1 // 2
    big = pl.BlockSpec((1, n1, tn), lambda bi_, j: (bi_, 0, j))
    half = pl.BlockSpec((1, n1h, tn), lambda bi_, j: (bi_, 0, j))
    return pl.pallas_call(
        _fft3_kernel,
        grid=(b, cols // tn),
        in_specs=[big, big, pl.BlockSpec((n1h, 2 * n1), lambda bi_, j: (0, 0)), half, half,
                  pl.BlockSpec((1, tn), lambda bi_, j: (0, 0))],
        out_specs=half,
        out_shape=jax.ShapeDtypeStruct((b, n1h, cols), BF16),
        compiler_params=_params(("parallel", "parallel")),
        name="fft_stage3",
    )(br, bi, inv1, u3, x03, skip_t)


def _hyena_long_conv(u, x0, filters, skip, batch, seq_len):
    c = HY_WIDTH
    fwd1, inv1, f2, twr, twi, n1, n2 = _dft_tables(seq_len)
    n1h = n1 // 2
    fa_r, fa_i = _fft1(filters.reshape(2, n1h, n2 * c), fwd1)
    kr, ki = _fft2_filter(fa_r.reshape(2, n1, n2, c), fa_i.reshape(2, n1, n2, c), twr, twi, f2)
    u3 = u.reshape(batch, n1h, n2 * c)
    a_r, a_i = _fft1(u3, fwd1)
    b_r, b_i = _fft2(a_r.reshape(batch, n1, n2, c), a_i.reshape(batch, n1, n2, c),
                     twr, twi, kr, ki, f2)
    tn = 2048
    skip_t = jnp.tile(skip[None, :], (1, tn // c))
    yh = _fft3(b_r.reshape(batch, n1, n2 * c), b_i.reshape(batch, n1, n2 * c), inv1,
               u3, x0.reshape(batch, n1h, n2 * c), skip_t, tn)
    return yh.reshape(batch * seq_len, c)


def _attn_kernel(q1_ref, q2_ref, k_ref, vt_ref, lam_ref, sg_ref, o_ref,
                 m1, a1, m2, a2, *, chunk):
    ki = pl.program_id(3)

    @pl.when(ki == 0)
    def _():
        for m, a in ((m1, a1), (m2, a2)):
            m[...] = jnp.full(m.shape, NEG_INF, F32)
            a[...] = jnp.zeros(a.shape, F32)

    q_refs = (q1_ref, q2_ref)
    stats = [[m1[...], a1[...]], [m2[...], a2[...]]]
    n_chunks = k_ref.shape[0] // chunk
    units = [(c, comp) for c in range(n_chunks) for comp in (0, 1)]

    def scores(unit):
        c, comp = unit
        return jnp.dot(k_ref[c * chunk:(c + 1) * chunk, :], q_refs[comp][...],
                       preferred_element_type=F32)

    def softmax(unit, s):
        m_prev = stats[unit[1]][0]
        m_new = jnp.maximum(m_prev, jnp.max(s, axis=0, keepdims=True))
        stats[unit[1]][0] = m_new
        return jnp.exp2(s - m_new).astype(BF16), jnp.exp2(m_prev - m_new)

    def values(unit, p, alpha):
        c, comp = unit
        stats[comp][1] = alpha * stats[comp][1] + jnp.dot(
            vt_ref[:, c * chunk:(c + 1) * chunk], p, preferred_element_type=F32)

    s_next = scores(units[0])
    pending = None
    for i, unit in enumerate(units):
        s_cur = s_next
        if i + 1 < len(units):
            s_next = scores(units[i + 1])
        if pending is not None:
            values(*pending)
        pending = (unit,) + softmax(unit, s_cur)
    values(*pending)
    for (m, a), (mv, av) in zip(((m1, a1), (m2, a2)), stats):
        m[...] = mv
        a[...] = av

    @pl.when(ki == pl.num_programs(3) - 1)
    def _():
        lv = lam_ref[...]
        lam = (jnp.exp(jnp.sum(lv[0:1] * lv[1:2], axis=1, keepdims=True))
               - jnp.exp(jnp.sum(lv[2:3] * lv[3:4], axis=1, keepdims=True)) + LAMBDA_INIT)
        d = ATT_V_DIM
        o = (a1[:d, :] / a1[d:d + 1, :] - lam * (a2[:d, :] / a2[d:d + 1, :]))
        inv = lax.rsqrt(jnp.mean(o * o, axis=0, keepdims=True) + SUBLN_EPS)
        g = pltpu.repeat(sg_ref[...], o.shape[1] // LANES, 1)
        o_ref[...] = (o * inv * g * (1.0 - LAMBDA_INIT)).astype(BF16)


def _attention(qt1, qt2, k, vt, lam_vecs, subln_g, batch, seq_len, tq=512, tk=2048, chunk=512):
    tk = min(tk, seq_len)
    nq, nk = seq_len // tq, seq_len // tk
    q_spec = pl.BlockSpec((LANES, tq), lambda b, h, i, j: (h, b * nq + i))
    sg = jnp.broadcast_to(subln_g[:, None], (ATT_V_DIM, LANES))
    return pl.pallas_call(
        functools.partial(_attn_kernel, chunk=chunk),
        grid=(batch, N_ATT_HEADS, nq, nk),
        in_specs=[q_spec, q_spec,
                  pl.BlockSpec((tk, LANES), lambda b, h, i, j: (b * nk + j, h)),
                  pl.BlockSpec((VT_ROWS, tk), lambda b, h, i, j: (h, b * nk + j)),
                  pl.BlockSpec((4, ATT_HEAD_DIM), lambda b, h, i, j: (0, 0)),
                  pl.BlockSpec((ATT_V_DIM, LANES), lambda b, h, i, j: (0, 0))],
        out_specs=q_spec,
        out_shape=jax.ShapeDtypeStruct((ATT_WIDTH, batch * seq_len), BF16),
        scratch_shapes=[pltpu.VMEM((1, tq), F32), pltpu.VMEM((VT_ROWS, tq), F32)] * 2,
        compiler_params=_params(("parallel", "parallel", "parallel", "arbitrary")),
        name="diff_attention",
    )(qt1, qt2, k, vt, lam_vecs, sg)


def _split_bf16(x):
    hi = x.astype(BF16)
    return hi, (x - hi.astype(F32)).astype(BF16)


def _outproj_kernel(yh_ref, ya_ref, x_ref, wo_ref, g_ref, wqh_ref, wql_ref, kh_ref, kl_ref,
                    x2_ref, xnt_ref, st_ref):
    wo = wo_ref[...]
    x2 = (x_ref[...] + jnp.dot(yh_ref[...], wo[:HY_WIDTH], preferred_element_type=F32)
          + lax.dot_general(ya_ref[...], wo[HY_WIDTH:], (((0,), (0,)), ((), ())),
                            preferred_element_type=F32))
    x2_ref[...] = x2
    xn = _rms(x2, g_ref[...], EPS)
    xh, xl = _split_bf16(xn)
    xnt_ref[...] = xn.T.astype(BF16)
    q = (jnp.dot(xh, wqh_ref[...], preferred_element_type=F32)
         + jnp.dot(xl, wqh_ref[...], preferred_element_type=F32)
         + jnp.dot(xh, wql_ref[...], preferred_element_type=F32))
    for hc in range(2 * PEER_HEADS):
        qh, ql = _split_bf16(q[:, hc * PEER_HALF:(hc + 1) * PEER_HALF])
        kh, kl = kh_ref[hc], kl_ref[hc]
        st_ref[hc * PEER_KEYS:(hc + 1) * PEER_KEYS, :] = (
            _dot_nt(kh, qh) + _dot_nt(kh, ql) + _dot_nt(kl, qh))


def _outproj(yh, ya, x, wo_bf, g2, wq_hi, wq_lo, keys_hi, keys_lo, tm=512):
    t = x.shape[0]
    qw = 2 * PEER_HEADS * PEER_HALF
    sw = 2 * PEER_HEADS * PEER_KEYS
    row = lambda i: (i, 0)
    fixed = lambda i: (0, 0)
    fixed3 = lambda i: (0, 0, 0)
    return pl.pallas_call(
        _outproj_kernel,
        grid=(t // tm,),
        in_specs=[pl.BlockSpec((tm, HY_WIDTH), row),
                  pl.BlockSpec((ATT_WIDTH, tm), lambda i: (0, i)),
                  pl.BlockSpec((tm, D_MODEL), row),
                  pl.BlockSpec((HY_WIDTH + ATT_WIDTH, D_MODEL), fixed),
                  pl.BlockSpec((1, D_MODEL), fixed),
                  pl.BlockSpec((D_MODEL, qw), fixed), pl.BlockSpec((D_MODEL, qw), fixed),
                  pl.BlockSpec((2 * PEER_HEADS, PEER_KEYS, PEER_HALF), fixed3),
                  pl.BlockSpec((2 * PEER_HEADS, PEER_KEYS, PEER_HALF), fixed3)],
        out_specs=[pl.BlockSpec((tm, D_MODEL), row), pl.BlockSpec((D_MODEL, tm), lambda i: (0, i)),
                   pl.BlockSpec((sw, tm), lambda i: (0, i))],
        out_shape=[jax.ShapeDtypeStruct((t, D_MODEL), F32),
                   jax.ShapeDtypeStruct((D_MODEL, t), BF16),
                   jax.ShapeDtypeStruct((sw, t), F32)],
        compiler_params=_params(("parallel",)),
        name="outproj_peer_query",
    )(yh, ya, x, wo_bf, g2, wq_hi, wq_lo, keys_hi, keys_lo)


def _extract_max(pieces, iotas, total, exact):
    m = pieces[0]
    for p in pieces[1:]:
        m = jnp.maximum(m, p)
    m = jnp.max(m, axis=0, keepdims=True)
    hot = [p == m for p in pieces]
    if exact:
        cand = [jnp.where(h, io, total) for h, io in zip(hot, iotas)]
        idx = cand[0]
        for c in cand[1:]:
            idx = jnp.minimum(idx, c)
        idx = jnp.min(idx, axis=0, keepdims=True)
        hot = [io == idx for io in iotas]
    return m, hot, [jnp.where(h, NEG_INF, p) for h, p in zip(hot, pieces)]


def _topk_rank(s, exact):
    rank = jnp.full(s.shape, float(PEER_TOPK), F32)
    iota = [lax.broadcasted_iota(jnp.int32, s.shape, 0)]
    vals = []
    for r in range(PEER_TOPK):
        m, hot, (s,) = _extract_max([s], iota, s.shape[0], exact)
        vals.append(m)
        rank = jnp.where(hot[0], float(r), rank)
    return vals, rank


_PAIRS = [(k0, k1) for k0 in range(PEER_TOPK) for k1 in range(PEER_TOPK // (k0 + 1))]
_PAIR_ROWS = -(-len(_PAIRS) // SUBLANES) * SUBLANES


def _rows_to_tiles(rows, width):
    sub = lax.broadcasted_iota(jnp.int32, (SUBLANES, width), 0)
    tiles = []
    for t0 in range(0, len(rows), SUBLANES):
        tile = jnp.full((SUBLANES, width), NEG_INF, F32)
        for r in range(SUBLANES):
            if rows[t0 + r] is not None:
                tile = jnp.where(sub == r, rows[t0 + r], tile)
        tiles.append(tile)
    return tiles


def _route_tables(s0, s1, exact):
    tt = s0.shape[1]
    v0, rank0 = _topk_rank(s0, exact)
    v1, rank1 = _topk_rank(s1, exact)
    pad = [None] * (_PAIR_ROWS - len(_PAIRS))
    tiles = [a + b for a, b in zip(_rows_to_tiles([v0[k0] for k0, _ in _PAIRS] + pad, tt),
                                   _rows_to_tiles([v1[k1] for _, k1 in _PAIRS] + pad, tt))]
    iotas = [lax.broadcasted_iota(jnp.int32, (SUBLANES, tt), 0) + t0
             for t0 in range(0, _PAIR_ROWS, SUBLANES)]
    cmax = v0[0] + v1[0]
    z = None
    for r in range(PEER_TOPK):
        m, _, tiles = _extract_max(tiles, iotas, _PAIR_ROWS, exact)
        z = jnp.exp(m - cmax) if z is None else z + jnp.exp(m - cmax)
    cnt = jnp.zeros(s0.shape, F32)
    total = None
    for k0 in range(PEER_TOPK):
        rows = [r for r, (a, _) in enumerate(_PAIRS) if a == k0]
        taken = None
        for ti in sorted({r // SUBLANES for r in rows}):
            inside = (iotas[ti] >= rows[0]) & (iotas[ti] <= rows[-1])
            part = jnp.sum(jnp.where(inside & (tiles[ti] == NEG_INF), 1.0, 0.0),
                           axis=0, keepdims=True)
            taken = part if taken is None else taken + part
        total = taken if total is None else total + taken
        cnt = jnp.where(rank0 == float(k0), taken, cnt)
    k = float(PEER_TOPK)
    ranked = lambda rank: jnp.sum(jnp.where(rank < k, 1.0, 0.0), axis=0, keepdims=True)
    ok = (total == k) & (ranked(rank0) == k) & (ranked(rank1) == k)
    return cnt, jnp.exp(s0 - v0[0]) / z, rank1, jnp.exp(s1 - v1[0]), ok


def _route_kernel(st_ref, cnt_ref, w0_ref, r1_ref, e1_ref):
    def emit(exact):
        cnt, w0, rank1, e1, ok = _route_tables(st_ref[:PEER_KEYS, :], st_ref[PEER_KEYS:, :], exact)
        cnt_ref[0] = cnt
        w0_ref[0] = w0 * GELU_OUT_SCALE
        r1_ref[0] = rank1.astype(BF16)
        e1_ref[0] = e1.astype(BF16)
        return ok

    ok = emit(exact=False)
    clean = jnp.min(jnp.where(ok, 1, 0))

    @pl.when(clean == 0)
    def _():
        emit(exact=True)


def _route(st, tt=512):
    sw, t = st.shape
    o_spec = pl.BlockSpec((1, PEER_KEYS, tt), lambda i, h: (h, 0, i))
    tab = lambda dt: jax.ShapeDtypeStruct((PEER_HEADS, PEER_KEYS, t), dt)
    return pl.pallas_call(
        _route_kernel,
        grid=(t // tt, PEER_HEADS),
        in_specs=[pl.BlockSpec((2 * PEER_KEYS, tt), lambda i, h: (h, i))],
        out_specs=[o_spec] * 4,
        out_shape=[tab(F32), tab(F32), tab(BF16), tab(BF16)],
        compiler_params=_params(("parallel", "parallel")),
        name="peer_route",
    )(st)


GELU_IN_SCALE = 2.0 ** -0.5
GELU_OUT_SCALE = 0.5 * 2.0 ** 0.5


def _gelu_core(y):
    return y * (1.0 + lax.erf(y))


def _peer_kernel(xnt_ref, u_ref, vt_ref, cnt_ref, w0_ref, r1_ref, e1_ref, x2_ref, gf_ref,
                 y_ref, acc_ref, *, rows_per_step, rows_per_unit):
    e = pl.program_id(1)

    @pl.when(e == 0)
    def _():
        acc_ref[...] = jnp.zeros(acc_ref.shape, F32)

    tt = xnt_ref.shape[1]
    unit = rows_per_unit * PEER_KEYS
    n_units = rows_per_step // rows_per_unit

    def hidden(s):
        return jnp.dot(u_ref[s * unit:(s + 1) * unit, :], xnt_ref[...],
                       preferred_element_type=F32)

    def activation(s, hid):
        gates = []
        for ii in range(s * rows_per_unit, (s + 1) * rows_per_unit):
            g = None
            for h in range(PEER_HEADS):
                cnt_row = jnp.broadcast_to(cnt_ref[h, ii:ii + 1, :], (PEER_KEYS, tt)).astype(BF16)
                w_row = jnp.broadcast_to(w0_ref[h, ii:ii + 1, :], (PEER_KEYS, tt)).astype(BF16)
                term = jnp.where(r1_ref[h] < cnt_row, e1_ref[h] * w_row, jnp.zeros((), BF16))
                g = term if g is None else g + term
            gates.append(g)
        return _gelu_core(hid).astype(BF16) * jnp.concatenate(gates, axis=0)

    def mix(s, act):
        return jnp.dot(vt_ref[:, s * unit:(s + 1) * unit], act, preferred_element_type=F32)

    acc = acc_ref[...]
    h_next = hidden(0)
    pending = None
    for s in range(n_units):
        h_cur = h_next
        if s + 1 < n_units:
            h_next = hidden(s + 1)
        if pending is not None:
            acc = acc + mix(*pending)
        pending = (s, activation(s, h_cur))
    acc_ref[...] = acc + mix(*pending)

    @pl.when(e == pl.num_programs(1) - 1)
    def _():
        x3 = x2_ref[...] + acc_ref[...].T
        y_ref[...] = _rms(x3, gf_ref[...], EPS)


def _peer(xnt, u_bf, vt_bf, cnt, w0, r1, e1, x2, gf, tt=512, rows_per_step=8, rows_per_unit=2):
    t = xnt.shape[1]
    te = rows_per_step * PEER_KEYS
    tok = lambda i, e: (i, 0)
    tab_i = pl.BlockSpec((PEER_HEADS, rows_per_step, tt), lambda i, e: (0, e, i))
    tab_j = pl.BlockSpec((PEER_HEADS, PEER_KEYS, tt), lambda i, e: (0, 0, i))
    return pl.pallas_call(
        functools.partial(_peer_kernel, rows_per_step=rows_per_step,
                          rows_per_unit=rows_per_unit),
        grid=(t // tt, PEER_EXPERTS // te),
        in_specs=[pl.BlockSpec((D_MODEL, tt), lambda i, e: (0, i)),
                  pl.BlockSpec((te, D_MODEL), lambda i, e: (e, 0)),
                  pl.BlockSpec((D_MODEL, te), lambda i, e: (0, e)),
                  tab_i, tab_i, tab_j, tab_j,
                  pl.BlockSpec((tt, D_MODEL), tok),
                  pl.BlockSpec((1, D_MODEL), lambda i, e: (0, 0))],
        out_specs=pl.BlockSpec((tt, D_MODEL), tok),
        out_shape=jax.ShapeDtypeStruct((t, D_MODEL), F32),
        scratch_shapes=[pltpu.VMEM((D_MODEL, tt), F32)],
        compiler_params=_params(("parallel", "arbitrary")),
        name="peer_dense",
    )(xnt, u_bf, vt_bf, cnt, w0, r1, e1, x2, gf)


def _rope_tables(seq_len):
    inv = ROPE_THETA ** (-jnp.arange(0, ROT_DIM, 2, dtype=F32) / ROT_DIM)
    ang = jnp.arange(seq_len, dtype=F32)[:, None] * inv[None, :]
    cos, sin = jnp.cos(ang), jnp.sin(ang)
    half = ROT_DIM // 2
    one = jnp.ones((seq_len, ATT_HEAD_DIM - ROT_DIM), F32)
    zero = jnp.zeros((seq_len, ATT_HEAD_DIM - half), F32)
    c = jnp.concatenate([cos, cos, one], axis=1)
    sa = jnp.concatenate([-sin, zero], axis=1)
    sb = jnp.concatenate([jnp.zeros((seq_len, half), F32), sin,
                          jnp.zeros((seq_len, ATT_HEAD_DIM - ROT_DIM), F32)], axis=1)
    both = lambda a: jnp.concatenate([a, a], axis=1)
    return both(c), both(sa), both(sb)


def _trunk(x, p):
    batch, seq_len, _ = x.shape
    xf = x.reshape(batch * seq_len, D_MODEL)
    rope_c, rope_sa, rope_sb = _rope_tables(seq_len)
    zh, qt1, qt2, k, vt = _inproj(xf, p["norm1_g"], p["w_in"], rope_c, rope_sa, rope_sb, seq_len)
    u, x0 = _shortconv(zh, p["hy_conv_w"], p["hy_conv_b"], seq_len)
    filters = _hyena_filters(seq_len, *p["filter"])
    yh = _hyena_long_conv(u, x0, filters, p["hy_skip"], batch, seq_len)
    ya = _attention(qt1, qt2, k, vt, p["lam_vecs"], p["subln_g"], batch, seq_len)
    x2, xnt, st = _outproj(yh, ya, xf, p["w_out"], p["norm2_g"], p["wq_hi"], p["wq_lo"],
                          p["keys_hi"], p["keys_lo"])
    cnt, w0, r1, e1 = _route(st)
    y = _peer(xnt, p["peer_u"], p["peer_vt"], cnt, w0, r1, e1, x2, p["normf_g"])
    return y.reshape(batch, seq_len, D_MODEL)


def kernel(x_prompt, x_sample, norm1_g, w_in, hy_conv_w, hy_conv_b, hf_w1, hf_b1, hf_freq1,
           hf_w2, hf_b2, hf_freq2, hf_w3, hf_b3, hy_skip, lam_q1, lam_k1, lam_q2, lam_k2,
           subln_g, w_out, norm2_g, peer_wq, peer_keys, peer_u, peer_v, normf_g):
    l = 0
    wq = peer_wq[l]
    wq_hi = wq.astype(BF16)
    keys = peer_keys[l].reshape(2 * PEER_HEADS, PEER_KEYS, PEER_HALF)
    keys_hi = keys.astype(BF16)
    p = {
        "norm1_g": norm1_g[l][None, :],
        "w_in": w_in[l].astype(BF16),
        "hy_conv_w": hy_conv_w[l],
        "hy_conv_b": hy_conv_b[l][None, :],
        "filter": (hf_w1[l], hf_b1[l], hf_freq1[l], hf_w2[l], hf_b2[l], hf_freq2[l],
                   hf_w3[l], hf_b3[l]),
        "hy_skip": hy_skip[l],
        "lam_vecs": jnp.stack([lam_q1[l], lam_k1[l], lam_q2[l], lam_k2[l]]),
        "subln_g": subln_g[l],
        "w_out": w_out[l].astype(BF16),
        "norm2_g": norm2_g[l][None, :],
        "wq_hi": wq_hi,
        "wq_lo": (wq - wq_hi.astype(F32)).astype(BF16),
        "keys_hi": keys_hi,
        "keys_lo": (keys - keys_hi.astype(F32)).astype(BF16),
        "peer_u": (peer_u[l] * GELU_IN_SCALE).astype(BF16),
        "peer_vt": peer_v[l].astype(BF16).T,
        "normf_g": normf_g[None, :],
    }
    return (_trunk(x_prompt, p), _trunk(x_sample, p))
```

```python
import functools
import math

import jax
import jax.numpy as jnp
import numpy as np
from jax import lax
from jax.experimental import pallas as pl
from jax.experimental.pallas import tpu as pltpu

F32 = jnp.float32
BF16 = jnp.bfloat16

D_MODEL = 1024
HY_WIDTH = 512
ATT_WIDTH = 512
N_ATT_HEADS = 4
ATT_HEAD_DIM = 64
ATT_V_DIM = 2 * ATT_HEAD_DIM
ROT_DIM = ATT_HEAD_DIM // 4
ROPE_THETA = 500000.0
IN_WIDTH = 3 * HY_WIDTH + 3 * ATT_WIDTH
FILTER_BANDS = 8
FILTER_EMB = 1 + 2 * FILTER_BANDS
FILTER_HIDDEN = 64
FAST_DECAY_PCT = 0.3
SLOW_DECAY_PCT = 1.5
DECAY_TARGET = 1e-2
PEER_HEADS = 8
PEER_KEYS = 128
PEER_EXPERTS = PEER_KEYS * PEER_KEYS
PEER_TOPK = 16
PEER_HALF = 128
EPS = 1e-6
SUBLN_EPS = 1e-5
LAMBDA_INIT = 0.8 - 0.6 * math.exp(-0.3 * 0)

LANES = 128
SUBLANES = 8
VMEM_LIMIT = 56 * 1024 * 1024
FFT_N2 = 128
VT_ROWS = ATT_V_DIM + 2 * SUBLANES
NEG_INF = float("-inf")


def _params(semantics, vmem=VMEM_LIMIT):
    return pltpu.CompilerParams(dimension_semantics=semantics, vmem_limit_bytes=vmem)


def _rms(x, g, eps):
    return x * lax.rsqrt(jnp.mean(x * x, axis=-1, keepdims=True) + eps) * g


def _dot_nt(a, b):
    return lax.dot_general(a, b, (((1,), (1,)), ((), ())), preferred_element_type=F32)


def _inproj_kernel(x_ref, g_ref, w_ref, c_ref, sa_ref, sb_ref,
                   zh_ref, qt1_ref, qt2_ref, k_ref, vt_ref):
    h = _rms(x_ref[...], g_ref[...], EPS).astype(BF16)
    z = jnp.dot(h, w_ref[...], preferred_element_type=F32)
    hy3 = 3 * HY_WIDTH
    zh_ref[...] = z[:, :hy3]
    c, sa, sb = c_ref[...], sa_ref[...], sb_ref[...]
    scale = ATT_HEAD_DIM ** -0.5 * math.log2(math.e)

    def rope(xb):
        return (xb * c + pltpu.roll(xb, LANES - ROT_DIM // 2, 1) * sa
                + pltpu.roll(xb, ROT_DIM // 2, 1) * sb)

    first = lax.broadcasted_iota(jnp.int32, (LANES, z.shape[0]), 0) < ATT_HEAD_DIM
    for hd in range(N_ATT_HEADS):
        lo, hi = hd * LANES, (hd + 1) * LANES
        qt = (rope(z[:, hy3 + lo:hy3 + hi]) * scale).T
        qt1_ref[lo:hi, :] = jnp.where(first, qt, 0.0).astype(BF16)
        qt2_ref[lo:hi, :] = jnp.where(first, 0.0, qt).astype(BF16)
        k_ref[:, lo:hi] = rope(z[:, hy3 + ATT_WIDTH + lo:hy3 + ATT_WIDTH + hi]).astype(BF16)
        vlo = hd * VT_ROWS
        vt_ref[vlo:vlo + ATT_V_DIM, :] = (
            z[:, hy3 + 2 * ATT_WIDTH + lo:hy3 + 2 * ATT_WIDTH + hi].T.astype(BF16))
        vt_ref[vlo + ATT_V_DIM:vlo + VT_ROWS, :] = jnp.ones(
            (VT_ROWS - ATT_V_DIM, z.shape[0]), BF16)


def _inproj(x, g, w_bf, rope_c, rope_sa, rope_sb, seq_len, tm=512):
    t = x.shape[0]
    nseq = seq_len // tm
    row = lambda i: (i, 0)
    col = lambda i: (0, i)
    fixed = lambda i: (0, 0)
    pos = lambda i: (i % nseq, 0)
    hy3 = 3 * HY_WIDTH
    tr_spec = pl.BlockSpec((ATT_WIDTH, tm), col)
    tr_shape = jax.ShapeDtypeStruct((ATT_WIDTH, t), BF16)
    return pl.pallas_call(
        _inproj_kernel,
        grid=(t // tm,),
        in_specs=[pl.BlockSpec((tm, D_MODEL), row),
                  pl.BlockSpec((1, D_MODEL), fixed),
                  pl.BlockSpec((D_MODEL, IN_WIDTH), fixed),
                  pl.BlockSpec((tm, LANES), pos),
                  pl.BlockSpec((tm, LANES), pos),
                  pl.BlockSpec((tm, LANES), pos)],
        out_specs=[pl.BlockSpec((tm, hy3), row), tr_spec, tr_spec,
                   pl.BlockSpec((tm, ATT_WIDTH), row),
                   pl.BlockSpec((N_ATT_HEADS * VT_ROWS, tm), col)],
        out_shape=[jax.ShapeDtypeStruct((t, hy3), F32), tr_shape, tr_shape,
                   jax.ShapeDtypeStruct((t, ATT_WIDTH), BF16),
                   jax.ShapeDtypeStruct((N_ATT_HEADS * VT_ROWS, t), BF16)],
        compiler_params=_params(("parallel",)),
        name="inproj",
    )(x, g, w_bf, rope_c, rope_sa, rope_sb)


def _shortconv_kernel(z_ref, zp_ref, zn_ref, w_ref, b_ref, u_ref, x0_ref, *, blocks_per_seq):
    i = pl.program_id(0)
    z = z_ref[...]
    tl = z.shape[0]
    row = lax.broadcasted_iota(jnp.int32, z.shape, 0)
    at_start = (i % blocks_per_seq) == 0
    at_end = (i % blocks_per_seq) == blocks_per_seq - 1
    prev_row = jnp.where(at_start, 0.0, zp_ref[SUBLANES - 1:SUBLANES, :])
    next_row = jnp.where(at_end, 0.0, zn_ref[0:1, :])
    zm = jnp.where(row == 0, prev_row, pltpu.roll(z, 1, 0))
    zp = jnp.where(row == tl - 1, next_row, pltpu.roll(z, tl - 1, 0))
    w = w_ref[...]
    conv = zm * w[0:1] + z * w[1:2] + zp * w[2:3] + b_ref[...]
    x0_ref[...] = conv[:, :HY_WIDTH]
    u_ref[...] = conv[:, 2 * HY_WIDTH:] * conv[:, HY_WIDTH:2 * HY_WIDTH]


def _shortconv(zh, w, b, seq_len, tl=512):
    t, hy3 = zh.shape
    per8 = tl // SUBLANES
    last8 = t // SUBLANES - 1
    return pl.pallas_call(
        functools.partial(_shortconv_kernel, blocks_per_seq=seq_len // tl),
        grid=(t // tl,),
        in_specs=[pl.BlockSpec((tl, hy3), lambda i: (i, 0)),
                  pl.BlockSpec((SUBLANES, hy3), lambda i: (jnp.maximum(i * per8 - 1, 0), 0)),
                  pl.BlockSpec((SUBLANES, hy3), lambda i: (jnp.minimum((i + 1) * per8, last8), 0)),
                  pl.BlockSpec((3, hy3), lambda i: (0, 0)),
                  pl.BlockSpec((1, hy3), lambda i: (0, 0))],
        out_specs=[pl.BlockSpec((tl, HY_WIDTH), lambda i: (i, 0))] * 2,
        out_shape=[jax.ShapeDtypeStruct((t, HY_WIDTH), F32)] * 2,
        compiler_params=_params(("parallel",)),
        name="shortconv",
    )(zh, zh, zh, w, b)


def _filter_kernel(f_ref, w1_ref, b1_ref, f1_ref, w2_ref, b2_ref, f2_ref, w3_ref, b3_ref,
                   dl_ref, h_ref):
    i = pl.program_id(0)
    hp = lax.Precision.HIGHEST
    feats = f_ref[...]
    tl = feats.shape[0]
    h = jnp.sin(f1_ref[...] * (jnp.dot(feats, w1_ref[...], precision=hp,
                                       preferred_element_type=F32) + b1_ref[...]))
    h = jnp.sin(f2_ref[...] * (jnp.dot(h, w2_ref[...], precision=hp,
                                       preferred_element_type=F32) + b2_ref[...]))
    h3 = jnp.dot(h, w3_ref[...], precision=hp, preferred_element_type=F32) + b3_ref[...]
    decay = jnp.exp(-feats[:, 0:1] * dl_ref[...])
    h_ref[0] = h3[:, :HY_WIDTH] * decay
    row = lax.broadcasted_iota(jnp.int32, (tl, HY_WIDTH), 0) + i * tl
    h_ref[1] = jnp.where(row == 0, 0.0, h3[:, HY_WIDTH:] * decay)


def _pad2(a, rows, cols):
    return jnp.pad(a, ((0, rows - a.shape[0]), (0, cols - a.shape[1])))


def _hyena_filters(seq_len, w1, b1, f1, w2, b2, f2, w3, b3, tl=512):
    t = jnp.linspace(0.0, 1.0, seq_len, dtype=F32)[:, None]
    w = 2.0 * math.pi * jnp.arange(seq_len, dtype=F32) / seq_len
    bands = jnp.linspace(1e-4, FILTER_BANDS - 1, FILTER_BANDS, dtype=F32)
    ang = w[:, None] * bands[None, :]
    feats = _pad2(jnp.concatenate([t, jnp.cos(ang), -jnp.sin(ang)], axis=-1), seq_len, LANES)
    max_decay = math.log(DECAY_TARGET) / FAST_DECAY_PCT
    min_decay = math.log(DECAY_TARGET) / SLOW_DECAY_PCT
    deltas = jnp.abs(jnp.linspace(min_decay, max_decay, HY_WIDTH, dtype=F32))[None, :]
    w1p = _pad2(w1, LANES, LANES)
    w2p = _pad2(w2, LANES, LANES)
    w3p = _pad2(w3, LANES, 2 * HY_WIDTH)
    pad1 = lambda v: _pad2(v[None, :], 1, LANES)
    fixed = lambda i: (0, 0)
    vec = pl.BlockSpec((1, LANES), fixed)
    mat = pl.BlockSpec((LANES, LANES), fixed)
    return pl.pallas_call(
        _filter_kernel,
        grid=(seq_len // tl,),
        in_specs=[pl.BlockSpec((tl, LANES), lambda i: (i, 0)), mat, vec, vec, mat, vec, vec,
                  pl.BlockSpec((LANES, 2 * HY_WIDTH), fixed),
                  pl.BlockSpec((1, 2 * HY_WIDTH), fixed),
                  pl.BlockSpec((1, HY_WIDTH), fixed)],
        out_specs=pl.BlockSpec((2, tl, HY_WIDTH), lambda i: (0, i, 0)),
        out_shape=jax.ShapeDtypeStruct((2, seq_len, HY_WIDTH), F32),
        compiler_params=_params(("parallel",)),
        name="hyena_filter",
    )(feats, w1p, pad1(b1), pad1(f1), w2p, pad1(b2), pad1(f2), w3p, b3[None, :], deltas)


def _dft_tables(seq_len):
    n = 2 * seq_len
    n2 = FFT_N2
    n1 = n // n2
    n1h = n1 // 2
    a1 = (2.0 * np.pi / n1) * np.outer(np.arange(n1), np.arange(n1))
    f1r, f1i = np.cos(a1), -np.sin(a1)
    fwd1 = np.concatenate([f1r[:, :n1h], f1i[:, :n1h]], axis=0)
    inv1 = np.concatenate([f1r[:n1h, :], f1i[:n1h, :]], axis=1) / n
    a2 = (2.0 * np.pi / n2) * np.outer(np.arange(n2), np.arange(n2))
    f2 = np.concatenate([np.cos(a2), -np.sin(a2)], axis=0)
    at = (2.0 * np.pi / n) * np.outer(np.arange(n1), np.arange(n2))
    twr = np.broadcast_to(np.cos(at)[:, :, None], (n1, n2, LANES))
    twi = np.broadcast_to(-np.sin(at)[:, :, None], (n1, n2, LANES))
    return (jnp.asarray(fwd1, BF16), jnp.asarray(inv1, BF16), jnp.asarray(f2, BF16),
            jnp.asarray(twr, F32), jnp.asarray(twi, F32), n1, n2)


def _fft1_kernel(u_ref, f_ref, ar_ref, ai_ref):
    r = jnp.dot(f_ref[...], u_ref[0].astype(BF16), preferred_element_type=F32)
    n1 = r.shape[0] // 2
    ar_ref[0] = r[:n1].astype(BF16)
    ai_ref[0] = r[n1:].astype(BF16)


def _fft1(u3, fwd1, tn=2048):
    b, n1h, cols = u3.shape
    n1 = 2 * n1h
    return pl.pallas_call(
        _fft1_kernel,
        grid=(b, cols // tn),
        in_specs=[pl.BlockSpec((1, n1h, tn), lambda bi, j: (bi, 0, j)),
                  pl.BlockSpec((2 * n1, n1h), lambda bi, j: (0, 0))],
        out_specs=[pl.BlockSpec((1, n1, tn), lambda bi, j: (bi, 0, j))] * 2,
        out_shape=[jax.ShapeDtypeStruct((b, n1, cols), BF16)] * 2,
        compiler_params=_params(("parallel", "parallel")),
        name="fft_stage1",
    )(u3, fwd1)


def _tile_lanes(x, width):
    return jnp.concatenate([x] * (width // x.shape[1]), axis=1)


def _dft2_forward(ar, ai, twr, twi, f2):
    n2, c = ar.shape
    ar, ai = ar.astype(F32), ai.astype(F32)
    tr, ti = _tile_lanes(twr, c), _tile_lanes(twi, c)
    cat = jnp.concatenate([ar * tr - ai * ti, ar * ti + ai * tr], axis=1).astype(BF16)
    p = jnp.dot(f2, cat, preferred_element_type=F32)
    return p[:n2, :c] - p[n2:, c:], p[:n2, c:] + p[n2:, :c]


def _fft2_filter_kernel(ar_ref, ai_ref, twr_ref, twi_ref, f2_ref, kr_ref, ki_ref, *, kb):
    f2 = f2_ref[...]
    for j in range(kb):
        fr, fi = _dft2_forward(ar_ref[0, j], ai_ref[0, j], twr_ref[j], twi_ref[j], f2)
        gr, gi = _dft2_forward(ar_ref[1, j], ai_ref[1, j], twr_ref[j], twi_ref[j], f2)
        kr_ref[j] = fr + gr
        ki_ref[j] = fi - gi


def _fft2_filter(ar, ai, twr, twi, f2, kb=8):
    _, n1, n2, c = ar.shape
    a_spec = pl.BlockSpec((2, kb, n2, c), lambda k: (0, k, 0, 0))
    t_spec = pl.BlockSpec((kb, n2, LANES), lambda k: (k, 0, 0))
    o_spec = pl.BlockSpec((kb, n2, c), lambda k: (k, 0, 0))
    return pl.pallas_call(
        functools.partial(_fft2_filter_kernel, kb=kb),
        grid=(n1 // kb,),
        in_specs=[a_spec, a_spec, t_spec, t_spec, pl.BlockSpec((2 * n2, n2), lambda k: (0, 0))],
        out_specs=[o_spec, o_spec],
        out_shape=[jax.ShapeDtypeStruct((n1, n2, c), F32)] * 2,
        compiler_params=_params(("parallel",)),
        name="fft_filter_spectrum",
    )(ar, ai, twr, twi, f2)


def _fft2_kernel(ar_ref, ai_ref, twr_ref, twi_ref, kr_ref, ki_ref, f2_ref, br_ref, bi_ref, *, kb):
    f2 = f2_ref[...]
    for j in range(kb):
        twr, twi = twr_ref[j], twi_ref[j]
        xr, xi = _dft2_forward(ar_ref[0, j], ai_ref[0, j], twr, twi, f2)
        n2, c = xr.shape
        kr, ki = kr_ref[j], ki_ref[j]
        cat = jnp.concatenate([xr * kr - xi * ki, xr * ki + xi * kr], axis=1).astype(BF16)
        p = jnp.dot(f2, cat, preferred_element_type=F32)
        b_r = p[:n2, :c] + p[n2:, c:]
        b_i = p[:n2, c:] - p[n2:, :c]
        tr, ti = _tile_lanes(twr, c), _tile_lanes(twi, c)
        br_ref[0, j] = (b_r * tr + b_i * ti).astype(BF16)
        bi_ref[0, j] = (b_i * tr - b_r * ti).astype(BF16)


def _fft2(ar, ai, twr, twi, kr, ki, f2, kb=8):
    b, n1, n2, c = ar.shape
    a_spec = pl.BlockSpec((1, kb, n2, c), lambda bi, k: (bi, k, 0, 0))
    t_spec = pl.BlockSpec((kb, n2, LANES), lambda bi, k: (k, 0, 0))
    k_spec = pl.BlockSpec((kb, n2, c), lambda bi, k: (k, 0, 0))
    return pl.pallas_call(
        functools.partial(_fft2_kernel, kb=kb),
        grid=(b, n1 // kb),
        in_specs=[a_spec, a_spec, t_spec, t_spec, k_spec, k_spec,
                  pl.BlockSpec((2 * n2, n2), lambda bi, k: (0, 0))],
        out_specs=[a_spec, a_spec],
        out_shape=[jax.ShapeDtypeStruct((b, n1, n2, c), BF16)] * 2,
        compiler_params=_params(("parallel", "parallel")),
        name="fft_stage2",
    )(ar, ai, twr, twi, kr, ki, f2)


def _fft3_kernel(br_ref, bi_ref, g_ref, u_ref, x0_ref, skip_ref, y_ref):
    g = g_ref[...]
    n1 = g.shape[1] // 2
    y = (jnp.dot(g[:, :n1], br_ref[0], preferred_element_type=F32)
         + jnp.dot(g[:, n1:], bi_ref[0], preferred_element_type=F32))
    y_ref[0] = (x0_ref[0] * (y + u_ref[0] * skip_ref[...])).astype(BF16)


def _fft3(br, bi, inv1, u3, x03, skip_t, tn=2048):
    b, n1, cols = br.shape
    n1h = n1 // 2
    big = pl.BlockSpec((1, n1, tn), lambda bi_, j: (bi_, 0, j))
    half = pl.BlockSpec((1, n1h, tn), lambda bi_, j: (bi_, 0, j))
    return pl.pallas_call(
        _fft3_kernel,
        grid=(b, cols // tn),
        in_specs=[big, big, pl.BlockSpec((n1h, 2 * n1), lambda bi_, j: (0, 0)), half, half,
                  pl.BlockSpec((1, tn), lambda bi_, j: (0, 0))],
        out_specs=half,
        out_shape=jax.ShapeDtypeStruct((b, n1h, cols), BF16),
        compiler_params=_params(("parallel", "parallel")),
        name="fft_stage3",
    )(br, bi, inv1, u3, x03, skip_t)


def _hyena_long_conv(u, x0, filters, skip, batch, seq_len):
    c = HY_WIDTH
    fwd1, inv1, f2, twr, twi, n1, n2 = _dft_tables(seq_len)
    n1h = n1 // 2
    fa_r, fa_i = _fft1(filters.reshape(2, n1h, n2 * c), fwd1)
    kr, ki = _fft2_filter(fa_r.reshape(2, n1, n2, c), fa_i.reshape(2, n1, n2, c), twr, twi, f2)
    u3 = u.reshape(batch, n1h, n2 * c)
    a_r, a_i = _fft1(u3, fwd1)
    b_r, b_i = _fft2(a_r.reshape(batch, n1, n2, c), a_i.reshape(batch, n1, n2, c),
                     twr, twi, kr, ki, f2)
    tn = 2048
    skip_t = jnp.tile(skip[None, :], (1, tn // c))
    yh = _fft3(b_r.reshape(batch, n1, n2 * c), b_i.reshape(batch, n1, n2 * c), inv1,
               u3, x0.reshape(batch, n1h, n2 * c), skip_t, tn)
    return yh.reshape(batch * seq_len, c)


def _attn_kernel(q1_ref, q2_ref, k_ref, vt_ref, lam_ref, sg_ref, o_ref,
                 m1, a1, m2, a2, *, chunk):
    ki = pl.program_id(3)

    @pl.when(ki == 0)
    def _():
        for m, a in ((m1, a1), (m2, a2)):
            m[...] = jnp.full(m.shape, NEG_INF, F32)
            a[...] = jnp.zeros(a.shape, F32)

    q_refs = (q1_ref, q2_ref)
    stats = [[m1[...], a1[...]], [m2[...], a2[...]]]
    n_chunks = k_ref.shape[0] // chunk
    units = [(c, comp) for c in range(n_chunks) for comp in (0, 1)]

    def scores(unit):
        c, comp = unit
        return jnp.dot(k_ref[c * chunk:(c + 1) * chunk, :], q_refs[comp][...],
                       preferred_element_type=F32)

    def softmax(unit, s):
        m_prev = stats[unit[1]][0]
        m_new = jnp.maximum(m_prev, jnp.max(s, axis=0, keepdims=True))
        stats[unit[1]][0] = m_new
        return jnp.exp2(s - m_new).astype(BF16), jnp.exp2(m_prev - m_new)

    def values(unit, p, alpha):
        c, comp = unit
        stats[comp][1] = alpha * stats[comp][1] + jnp.dot(
            vt_ref[:, c * chunk:(c + 1) * chunk], p, preferred_element_type=F32)

    s_next = scores(units[0])
    pending = None
    for i, unit in enumerate(units):
        s_cur = s_next
        if i + 1 < len(units):
            s_next = scores(units[i + 1])
        if pending is not None:
            values(*pending)
        pending = (unit,) + softmax(unit, s_cur)
    values(*pending)
    for (m, a), (mv, av) in zip(((m1, a1), (m2, a2)), stats):
        m[...] = mv
        a[...] = av

    @pl.when(ki == pl.num_programs(3) - 1)
    def _():
        lv = lam_ref[...]
        lam = (jnp.exp(jnp.sum(lv[0:1] * lv[1:2], axis=1, keepdims=True))
               - jnp.exp(jnp.sum(lv[2:3] * lv[3:4], axis=1, keepdims=True)) + LAMBDA_INIT)
        d = ATT_V_DIM
        o = (a1[:d, :] / a1[d:d + 1, :] - lam * (a2[:d, :] / a2[d:d + 1, :]))
        inv = lax.rsqrt(jnp.mean(o * o, axis=0, keepdims=True) + SUBLN_EPS)
        g = pltpu.repeat(sg_ref[...], o.shape[1] // LANES, 1)
        o_ref[...] = (o * inv * g * (1.0 - LAMBDA_INIT)).astype(BF16)


def _attention(qt1, qt2, k, vt, lam_vecs, subln_g, batch, seq_len, tq=512, tk=2048, chunk=512):
    tk = min(tk, seq_len)
    nq, nk = seq_len // tq, seq_len // tk
    q_spec = pl.BlockSpec((LANES, tq), lambda b, h, i, j: (h, b * nq + i))
    sg = jnp.broadcast_to(subln_g[:, None], (ATT_V_DIM, LANES))
    return pl.pallas_call(
        functools.partial(_attn_kernel, chunk=chunk),
        grid=(batch, N_ATT_HEADS, nq, nk),
        in_specs=[q_spec, q_spec,
                  pl.BlockSpec((tk, LANES), lambda b, h, i, j: (b * nk + j, h)),
                  pl.BlockSpec((VT_ROWS, tk), lambda b, h, i, j: (h, b * nk + j)),
                  pl.BlockSpec((4, ATT_HEAD_DIM), lambda b, h, i, j: (0, 0)),
                  pl.BlockSpec((ATT_V_DIM, LANES), lambda b, h, i, j: (0, 0))],
        out_specs=q_spec,
        out_shape=jax.ShapeDtypeStruct((ATT_WIDTH, batch * seq_len), BF16),
        scratch_shapes=[pltpu.VMEM((1, tq), F32), pltpu.VMEM((VT_ROWS, tq), F32)] * 2,
        compiler_params=_params(("parallel", "parallel", "parallel", "arbitrary")),
        name="diff_attention",
    )(qt1, qt2, k, vt, lam_vecs, sg)


def _split_bf16(x):
    hi = x.astype(BF16)
    return hi, (x - hi.astype(F32)).astype(BF16)


def _outproj_kernel(yh_ref, ya_ref, x_ref, wo_ref, g_ref, wq_ref, kh_ref, kl_ref,
                    x2_ref, xnt_ref, st_ref):
    wo = wo_ref[...]
    x2 = (x_ref[...] + jnp.dot(yh_ref[...], wo[:HY_WIDTH], preferred_element_type=F32)
          + lax.dot_general(ya_ref[...], wo[HY_WIDTH:], (((0,), (0,)), ((), ())),
                            preferred_element_type=F32))
    x2_ref[...] = x2
    xn = _rms(x2, g_ref[...], EPS)
    xnt_ref[...] = xn.T.astype(BF16)
    q = jnp.dot(xn.astype(BF16), wq_ref[...], preferred_element_type=F32)
    for hc in range(2 * PEER_HEADS):
        qh, ql = _split_bf16(q[:, hc * PEER_HALF:(hc + 1) * PEER_HALF])
        kh, kl = kh_ref[hc], kl_ref[hc]
        st_ref[hc * PEER_KEYS:(hc + 1) * PEER_KEYS, :] = (
            _dot_nt(kh, qh) + _dot_nt(kh, ql) + _dot_nt(kl, qh))


def _outproj(yh, ya, x, wo_bf, g2, wq_bf, keys_hi, keys_lo, tm=512):
    t = x.shape[0]
    qw = 2 * PEER_HEADS * PEER_HALF
    sw = 2 * PEER_HEADS * PEER_KEYS
    row = lambda i: (i, 0)
    fixed = lambda i: (0, 0)
    fixed3 = lambda i: (0, 0, 0)
    return pl.pallas_call(
        _outproj_kernel,
        grid=(t // tm,),
        in_specs=[pl.BlockSpec((tm, HY_WIDTH), row),
                  pl.BlockSpec((ATT_WIDTH, tm), lambda i: (0, i)),
                  pl.BlockSpec((tm, D_MODEL), row),
                  pl.BlockSpec((HY_WIDTH + ATT_WIDTH, D_MODEL), fixed),
                  pl.BlockSpec((1, D_MODEL), fixed),
                  pl.BlockSpec((D_MODEL, qw), fixed),
                  pl.BlockSpec((2 * PEER_HEADS, PEER_KEYS, PEER_HALF), fixed3),
                  pl.BlockSpec((2 * PEER_HEADS, PEER_KEYS, PEER_HALF), fixed3)],
        out_specs=[pl.BlockSpec((tm, D_MODEL), row), pl.BlockSpec((D_MODEL, tm), lambda i: (0, i)),
                   pl.BlockSpec((sw, tm), lambda i: (0, i))],
        out_shape=[jax.ShapeDtypeStruct((t, D_MODEL), F32),
                   jax.ShapeDtypeStruct((D_MODEL, t), BF16),
                   jax.ShapeDtypeStruct((sw, t), F32)],
        compiler_params=_params(("parallel",)),
        name="outproj_peer_query",
    )(yh, ya, x, wo_bf, g2, wq_bf, keys_hi, keys_lo)


def _extract_max(pieces, iotas, total, exact):
    m = pieces[0]
    for p in pieces[1:]:
        m = jnp.maximum(m, p)
    m = jnp.max(m, axis=0, keepdims=True)
    hot = [p == m for p in pieces]
    if exact:
        cand = [jnp.where(h, io, total) for h, io in zip(hot, iotas)]
        idx = cand[0]
        for c in cand[1:]:
            idx = jnp.minimum(idx, c)
        idx = jnp.min(idx, axis=0, keepdims=True)
        hot = [io == idx for io in iotas]
    return m, hot, [jnp.where(h, NEG_INF, p) for h, p in zip(hot, pieces)]


def _topk_rank(s, exact, want_rank):
    rank = jnp.full(s.shape, float(PEER_TOPK), F32) if want_rank else None
    iota = [lax.broadcasted_iota(jnp.int32, s.shape, 0)]
    vals = []
    for r in range(PEER_TOPK):
        m, hot, (s,) = _extract_max([s], iota, s.shape[0], exact)
        vals.append(m)
        if want_rank:
            rank = jnp.where(hot[0], float(r), rank)
    return vals, rank


_PAIRS = [(k0, k1) for k0 in range(PEER_TOPK) for k1 in range(PEER_TOPK // (k0 + 1))]
_PAIR_ROWS = -(-len(_PAIRS) // SUBLANES) * SUBLANES


def _rows_to_tiles(rows, width):
    sub = lax.broadcasted_iota(jnp.int32, (SUBLANES, width), 0)
    tiles = []
    for t0 in range(0, len(rows), SUBLANES):
        tile = jnp.full((SUBLANES, width), NEG_INF, F32)
        for r in range(SUBLANES):
            if rows[t0 + r] is not None:
                tile = jnp.where(sub == r, rows[t0 + r], tile)
        tiles.append(tile)
    return tiles


def _route_tables(s0, s1, exact):
    tt = s0.shape[1]
    v0, rank0 = _topk_rank(s0, exact, want_rank=exact)
    v1, rank1 = _topk_rank(s1, exact, want_rank=True)
    pad = [None] * (_PAIR_ROWS - len(_PAIRS))
    tiles = [a + b for a, b in zip(_rows_to_tiles([v0[k0] for k0, _ in _PAIRS] + pad, tt),
                                   _rows_to_tiles([v1[k1] for _, k1 in _PAIRS] + pad, tt))]
    iotas = [lax.broadcasted_iota(jnp.int32, (SUBLANES, tt), 0) + t0
             for t0 in range(0, _PAIR_ROWS, SUBLANES)]
    cmax = v0[0] + v1[0]
    z = None
    for r in range(PEER_TOPK):
        m, _, tiles = _extract_max(tiles, iotas, _PAIR_ROWS, exact)
        z = jnp.exp(m - cmax) if z is None else z + jnp.exp(m - cmax)
    cnt = jnp.zeros(s0.shape, F32)
    total = None
    for k0 in range(PEER_TOPK):
        rows = [r for r, (a, _) in enumerate(_PAIRS) if a == k0]
        taken = None
        for ti in sorted({r // SUBLANES for r in rows}):
            inside = (iotas[ti] >= rows[0]) & (iotas[ti] <= rows[-1])
            part = jnp.sum(jnp.where(inside & (tiles[ti] == NEG_INF), 1.0, 0.0),
                           axis=0, keepdims=True)
            taken = part if taken is None else taken + part
        total = taken if total is None else total + taken
        cnt = jnp.where((rank0 == float(k0)) if exact else (s0 == v0[k0]), taken, cnt)
    k = float(PEER_TOPK)
    count = lambda mask: jnp.sum(jnp.where(mask, 1.0, 0.0), axis=0, keepdims=True)
    top0 = (rank0 < k) if exact else (s0 >= v0[PEER_TOPK - 1])
    ok = (total == k) & (count(top0) == k) & (count(rank1 < k) == k)
    return cnt, jnp.exp(s0 - v0[0]) / z, rank1, jnp.exp(s1 - v1[0]), ok


def _route_kernel(st_ref, cnt_ref, w0_ref, r1_ref, e1_ref):
    def emit(exact):
        cnt, w0, rank1, e1, ok = _route_tables(st_ref[:PEER_KEYS, :], st_ref[PEER_KEYS:, :], exact)
        cnt_ref[0] = cnt
        w0_ref[0] = w0 * GELU_OUT_SCALE
        r1_ref[0] = rank1.astype(BF16)
        e1_ref[0] = e1.astype(BF16)
        return ok

    ok = emit(exact=False)
    clean = jnp.min(jnp.where(ok, 1, 0))

    @pl.when(clean == 0)
    def _():
        emit(exact=True)


def _route(st, tt=512):
    sw, t = st.shape
    o_spec = pl.BlockSpec((1, PEER_KEYS, tt), lambda i, h: (h, 0, i))
    tab = lambda dt: jax.ShapeDtypeStruct((PEER_HEADS, PEER_KEYS, t), dt)
    return pl.pallas_call(
        _route_kernel,
        grid=(t // tt, PEER_HEADS),
        in_specs=[pl.BlockSpec((2 * PEER_KEYS, tt), lambda i, h: (h, i))],
        out_specs=[o_spec] * 4,
        out_shape=[tab(F32), tab(F32), tab(BF16), tab(BF16)],
        compiler_params=_params(("parallel", "parallel")),
        name="peer_route",
    )(st)


GELU_IN_SCALE = 2.0 ** -0.5
GELU_OUT_SCALE = 0.5 * 2.0 ** 0.5


def _gelu_core(y):
    return y * (1.0 + lax.erf(y))


def _peer_kernel(xnt_ref, u_ref, vt_ref, cnt_ref, w0_ref, r1_ref, e1_ref, x2_ref, gf_ref,
                 y_ref, acc_ref, *, rows_per_step, rows_per_unit):
    e = pl.program_id(1)

    @pl.when(e == 0)
    def _():
        acc_ref[...] = jnp.zeros(acc_ref.shape, F32)

    tt = xnt_ref.shape[1]
    unit = rows_per_unit * PEER_KEYS
    n_units = rows_per_step // rows_per_unit

    def hidden(s):
        return jnp.dot(u_ref[s * unit:(s + 1) * unit, :], xnt_ref[...],
                       preferred_element_type=F32)

    def activation(s, hid):
        gates = []
        for ii in range(s * rows_per_unit, (s + 1) * rows_per_unit):
            g = None
            for h in range(PEER_HEADS):
                cnt_row = jnp.broadcast_to(cnt_ref[h, ii:ii + 1, :], (PEER_KEYS, tt)).astype(BF16)
                w_row = jnp.broadcast_to(w0_ref[h, ii:ii + 1, :], (PEER_KEYS, tt)).astype(BF16)
                term = jnp.where(r1_ref[h] < cnt_row, e1_ref[h] * w_row, jnp.zeros((), BF16))
                g = term if g is None else g + term
            gates.append(g)
        return _gelu_core(hid).astype(BF16) * jnp.concatenate(gates, axis=0)

    def mix(s, act):
        return jnp.dot(vt_ref[:, s * unit:(s + 1) * unit], act, preferred_element_type=F32)

    acc = acc_ref[...]
    h_next = hidden(0)
    pending = None
    for s in range(n_units):
        h_cur = h_next
        if s + 1 < n_units:
            h_next = hidden(s + 1)
        if pending is not None:
            acc = acc + mix(*pending)
        pending = (s, activation(s, h_cur))
    acc_ref[...] = acc + mix(*pending)

    @pl.when(e == pl.num_programs(1) - 1)
    def _():
        x3 = x2_ref[...] + acc_ref[...].T
        y_ref[...] = _rms(x3, gf_ref[...], EPS)


def _peer(xnt, u_bf, vt_bf, cnt, w0, r1, e1, x2, gf, tt=512, rows_per_step=16, rows_per_unit=2):
    t = xnt.shape[1]
    te = rows_per_step * PEER_KEYS
    tok = lambda i, e: (i, 0)
    tab_i = pl.BlockSpec((PEER_HEADS, rows_per_step, tt), lambda i, e: (0, e, i))
    tab_j = pl.BlockSpec((PEER_HEADS, PEER_KEYS, tt), lambda i, e: (0, 0, i))
    return pl.pallas_call(
        functools.partial(_peer_kernel, rows_per_step=rows_per_step,
                          rows_per_unit=rows_per_unit),
        grid=(t // tt, PEER_EXPERTS // te),
        in_specs=[pl.BlockSpec((D_MODEL, tt), lambda i, e: (0, i)),
                  pl.BlockSpec((te, D_MODEL), lambda i, e: (e, 0)),
                  pl.BlockSpec((D_MODEL, te), lambda i, e: (0, e)),
                  tab_i, tab_i, tab_j, tab_j,
                  pl.BlockSpec((tt, D_MODEL), tok),
                  pl.BlockSpec((1, D_MODEL), lambda i, e: (0, 0))],
        out_specs=pl.BlockSpec((tt, D_MODEL), tok),
        out_shape=jax.ShapeDtypeStruct((t, D_MODEL), F32),
        scratch_shapes=[pltpu.VMEM((D_MODEL, tt), F32)],
        compiler_params=_params(("parallel", "arbitrary")),
        name="peer_dense",
    )(xnt, u_bf, vt_bf, cnt, w0, r1, e1, x2, gf)


def _rope_tables(seq_len):
    inv = ROPE_THETA ** (-jnp.arange(0, ROT_DIM, 2, dtype=F32) / ROT_DIM)
    ang = jnp.arange(seq_len, dtype=F32)[:, None] * inv[None, :]
    cos, sin = jnp.cos(ang), jnp.sin(ang)
    half = ROT_DIM // 2
    one = jnp.ones((seq_len, ATT_HEAD_DIM - ROT_DIM), F32)
    zero = jnp.zeros((seq_len, ATT_HEAD_DIM - half), F32)
    c = jnp.concatenate([cos, cos, one], axis=1)
    sa = jnp.concatenate([-sin, zero], axis=1)
    sb = jnp.concatenate([jnp.zeros((seq_len, half), F32), sin,
                          jnp.zeros((seq_len, ATT_HEAD_DIM - ROT_DIM), F32)], axis=1)
    both = lambda a: jnp.concatenate([a, a], axis=1)
    return both(c), both(sa), both(sb)


def _trunk(x, p):
    batch, seq_len, _ = x.shape
    xf = x.reshape(batch * seq_len, D_MODEL)
    rope_c, rope_sa, rope_sb = _rope_tables(seq_len)
    zh, qt1, qt2, k, vt = _inproj(xf, p["norm1_g"], p["w_in"], rope_c, rope_sa, rope_sb, seq_len)
    u, x0 = _shortconv(zh, p["hy_conv_w"], p["hy_conv_b"], seq_len)
    filters = _hyena_filters(seq_len, *p["filter"])
    yh = _hyena_long_conv(u, x0, filters, p["hy_skip"], batch, seq_len)
    ya = _attention(qt1, qt2, k, vt, p["lam_vecs"], p["subln_g"], batch, seq_len)
    x2, xnt, st = _outproj(yh, ya, xf, p["w_out"], p["norm2_g"], p["wq"],
                          p["keys_hi"], p["keys_lo"])
    cnt, w0, r1, e1 = _route(st)
    y = _peer(xnt, p["peer_u"], p["peer_vt"], cnt, w0, r1, e1, x2, p["normf_g"])
    return y.reshape(batch, seq_len, D_MODEL)


def kernel(x_prompt, x_sample, norm1_g, w_in, hy_conv_w, hy_conv_b, hf_w1, hf_b1, hf_freq1,
           hf_w2, hf_b2, hf_freq2, hf_w3, hf_b3, hy_skip, lam_q1, lam_k1, lam_q2, lam_k2,
           subln_g, w_out, norm2_g, peer_wq, peer_keys, peer_u, peer_v, normf_g):
    l = 0
    keys = peer_keys[l].reshape(2 * PEER_HEADS, PEER_KEYS, PEER_HALF)
    keys_hi = keys.astype(BF16)
    p = {
        "norm1_g": norm1_g[l][None, :],
        "w_in": w_in[l].astype(BF16),
        "hy_conv_w": hy_conv_w[l],
        "hy_conv_b": hy_conv_b[l][None, :],
        "filter": (hf_w1[l], hf_b1[l], hf_freq1[l], hf_w2[l], hf_b2[l], hf_freq2[l],
                   hf_w3[l], hf_b3[l]),
        "hy_skip": hy_skip[l],
        "lam_vecs": jnp.stack([lam_q1[l], lam_k1[l], lam_q2[l], lam_k2[l]]),
        "subln_g": subln_g[l],
        "w_out": w_out[l].astype(BF16),
        "norm2_g": norm2_g[l][None, :],
        "wq": peer_wq[l].astype(BF16),
        "keys_hi": keys_hi,
        "keys_lo": (keys - keys_hi.astype(F32)).astype(BF16),
        "peer_u": (peer_u[l] * GELU_IN_SCALE).astype(BF16),
        "peer_vt": peer_v[l].astype(BF16).T,
        "normf_g": normf_g[None, :],
    }
    return (_trunk(x_prompt, p), _trunk(x_sample, p))
```

```python
import functools
import math

import jax
import jax.numpy as jnp
import numpy as np
from jax import lax
from jax.experimental import pallas as pl
from jax.experimental.pallas import tpu as pltpu

F32 = jnp.float32
BF16 = jnp.bfloat16

D_MODEL = 1024
HY_WIDTH = 512
ATT_WIDTH = 512
N_ATT_HEADS = 4
ATT_HEAD_DIM = 64
ATT_V_DIM = 2 * ATT_HEAD_DIM
ROT_DIM = ATT_HEAD_DIM // 4
ROPE_THETA = 500000.0
IN_WIDTH = 3 * HY_WIDTH + 3 * ATT_WIDTH
FILTER_BANDS = 8
FILTER_EMB = 1 + 2 * FILTER_BANDS
FILTER_HIDDEN = 64
FAST_DECAY_PCT = 0.3
SLOW_DECAY_PCT = 1.5
DECAY_TARGET = 1e-2
PEER_HEADS = 8
PEER_KEYS = 128
PEER_EXPERTS = PEER_KEYS * PEER_KEYS
PEER_TOPK = 16
PEER_HALF = 128
EPS = 1e-6
SUBLN_EPS = 1e-5
LAMBDA_INIT = 0.8 - 0.6 * math.exp(-0.3 * 0)

LANES = 128
SUBLANES = 8
VMEM_LIMIT = 56 * 1024 * 1024
FFT_N2 = 128
VT_ROWS = ATT_V_DIM + 2 * SUBLANES
NEG_INF = float("-inf")


def _params(semantics, vmem=VMEM_LIMIT):
    return pltpu.CompilerParams(dimension_semantics=semantics, vmem_limit_bytes=vmem)


def _rms(x, g, eps):
    return x * lax.rsqrt(jnp.mean(x * x, axis=-1, keepdims=True) + eps) * g


def _dot_nt(a, b):
    return lax.dot_general(a, b, (((1,), (1,)), ((), ())), preferred_element_type=F32)


def _inproj_kernel(x_ref, g_ref, w_ref, c_ref, sa_ref, sb_ref,
                   zh_ref, qt1_ref, qt2_ref, k_ref, vt_ref):
    h = _rms(x_ref[...], g_ref[...], EPS).astype(BF16)
    z = jnp.dot(h, w_ref[...], preferred_element_type=F32)
    hy3 = 3 * HY_WIDTH
    zh_ref[...] = z[:, :hy3]
    c, sa, sb = c_ref[...], sa_ref[...], sb_ref[...]
    scale = ATT_HEAD_DIM ** -0.5 * math.log2(math.e)

    def rope(xb):
        return (xb * c + pltpu.roll(xb, LANES - ROT_DIM // 2, 1) * sa
                + pltpu.roll(xb, ROT_DIM // 2, 1) * sb)

    first = lax.broadcasted_iota(jnp.int32, (LANES, z.shape[0]), 0) < ATT_HEAD_DIM
    for hd in range(N_ATT_HEADS):
        lo, hi = hd * LANES, (hd + 1) * LANES
        qt = (rope(z[:, hy3 + lo:hy3 + hi]) * scale).T
        qt1_ref[lo:hi, :] = jnp.where(first, qt, 0.0).astype(BF16)
        qt2_ref[lo:hi, :] = jnp.where(first, 0.0, qt).astype(BF16)
        k_ref[:, lo:hi] = rope(z[:, hy3 + ATT_WIDTH + lo:hy3 + ATT_WIDTH + hi]).astype(BF16)
        vlo = hd * VT_ROWS
        vt_ref[vlo:vlo + ATT_V_DIM, :] = (
            z[:, hy3 + 2 * ATT_WIDTH + lo:hy3 + 2 * ATT_WIDTH + hi].T.astype(BF16))
        vt_ref[vlo + ATT_V_DIM:vlo + VT_ROWS, :] = jnp.ones(
            (VT_ROWS - ATT_V_DIM, z.shape[0]), BF16)


def _inproj(x, g, w_bf, rope_c, rope_sa, rope_sb, seq_len, tm=512):
    t = x.shape[0]
    nseq = seq_len // tm
    row = lambda i: (i, 0)
    col = lambda i: (0, i)
    fixed = lambda i: (0, 0)
    pos = lambda i: (i % nseq, 0)
    hy3 = 3 * HY_WIDTH
    tr_spec = pl.BlockSpec((ATT_WIDTH, tm), col)
    tr_shape = jax.ShapeDtypeStruct((ATT_WIDTH, t), BF16)
    return pl.pallas_call(
        _inproj_kernel,
        grid=(t // tm,),
        in_specs=[pl.BlockSpec((tm, D_MODEL), row),
                  pl.BlockSpec((1, D_MODEL), fixed),
                  pl.BlockSpec((D_MODEL, IN_WIDTH), fixed),
                  pl.BlockSpec((tm, LANES), pos),
                  pl.BlockSpec((tm, LANES), pos),
                  pl.BlockSpec((tm, LANES), pos)],
        out_specs=[pl.BlockSpec((tm, hy3), row), tr_spec, tr_spec,
                   pl.BlockSpec((tm, ATT_WIDTH), row),
                   pl.BlockSpec((N_ATT_HEADS * VT_ROWS, tm), col)],
        out_shape=[jax.ShapeDtypeStruct((t, hy3), F32), tr_shape, tr_shape,
                   jax.ShapeDtypeStruct((t, ATT_WIDTH), BF16),
                   jax.ShapeDtypeStruct((N_ATT_HEADS * VT_ROWS, t), BF16)],
        compiler_params=_params(("parallel",)),
        name="inproj",
    )(x, g, w_bf, rope_c, rope_sa, rope_sb)


def _shortconv_kernel(z_ref, zp_ref, zn_ref, w_ref, b_ref, u_ref, x0_ref, *, blocks_per_seq):
    i = pl.program_id(0)
    z = z_ref[...]
    tl = z.shape[0]
    row = lax.broadcasted_iota(jnp.int32, z.shape, 0)
    at_start = (i % blocks_per_seq) == 0
    at_end = (i % blocks_per_seq) == blocks_per_seq - 1
    prev_row = jnp.where(at_start, 0.0, zp_ref[SUBLANES - 1:SUBLANES, :])
    next_row = jnp.where(at_end, 0.0, zn_ref[0:1, :])
    zm = jnp.where(row == 0, prev_row, pltpu.roll(z, 1, 0))
    zp = jnp.where(row == tl - 1, next_row, pltpu.roll(z, tl - 1, 0))
    w = w_ref[...]
    conv = zm * w[0:1] + z * w[1:2] + zp * w[2:3] + b_ref[...]
    x0_ref[...] = conv[:, :HY_WIDTH]
    u_ref[...] = conv[:, 2 * HY_WIDTH:] * conv[:, HY_WIDTH:2 * HY_WIDTH]


def _shortconv(zh, w, b, seq_len, tl=512):
    t, hy3 = zh.shape
    per8 = tl // SUBLANES
    last8 = t // SUBLANES - 1
    return pl.pallas_call(
        functools.partial(_shortconv_kernel, blocks_per_seq=seq_len // tl),
        grid=(t // tl,),
        in_specs=[pl.BlockSpec((tl, hy3), lambda i: (i, 0)),
                  pl.BlockSpec((SUBLANES, hy3), lambda i: (jnp.maximum(i * per8 - 1, 0), 0)),
                  pl.BlockSpec((SUBLANES, hy3), lambda i: (jnp.minimum((i + 1) * per8, last8), 0)),
                  pl.BlockSpec((3, hy3), lambda i: (0, 0)),
                  pl.BlockSpec((1, hy3), lambda i: (0, 0))],
        out_specs=[pl.BlockSpec((tl, HY_WIDTH), lambda i: (i, 0))] * 2,
        out_shape=[jax.ShapeDtypeStruct((t, HY_WIDTH), F32)] * 2,
        compiler_params=_params(("parallel",)),
        name="shortconv",
    )(zh, zh, zh, w, b)


def _filter_kernel(f_ref, w1_ref, b1_ref, f1_ref, w2_ref, b2_ref, f2_ref, w3_ref, b3_ref,
                   dl_ref, h_ref):
    i = pl.program_id(0)
    hp = lax.Precision.HIGHEST
    feats = f_ref[...]
    tl = feats.shape[0]
    h = jnp.sin(f1_ref[...] * (jnp.dot(feats, w1_ref[...], precision=hp,
                                       preferred_element_type=F32) + b1_ref[...]))
    h = jnp.sin(f2_ref[...] * (jnp.dot(h, w2_ref[...], precision=hp,
                                       preferred_element_type=F32) + b2_ref[...]))
    h3 = jnp.dot(h, w3_ref[...], precision=hp, preferred_element_type=F32) + b3_ref[...]
    decay = jnp.exp(-feats[:, 0:1] * dl_ref[...])
    h_ref[0] = h3[:, :HY_WIDTH] * decay
    row = lax.broadcasted_iota(jnp.int32, (tl, HY_WIDTH), 0) + i * tl
    h_ref[1] = jnp.where(row == 0, 0.0, h3[:, HY_WIDTH:] * decay)


def _pad2(a, rows, cols):
    return jnp.pad(a, ((0, rows - a.shape[0]), (0, cols - a.shape[1])))


def _hyena_filters(seq_len, w1, b1, f1, w2, b2, f2, w3, b3, tl=512):
    f32 = np.float32
    t = np.linspace(0.0, 1.0, seq_len).astype(f32)[:, None]
    w = f32(2.0 * math.pi) * np.arange(seq_len, dtype=f32) / f32(seq_len)
    bands = np.linspace(1e-4, FILTER_BANDS - 1, FILTER_BANDS).astype(f32)
    ang = (w[:, None] * bands[None, :]).astype(np.float64)
    feats_np = np.zeros((seq_len, LANES), f32)
    feats_np[:, :FILTER_EMB] = np.concatenate([t, np.cos(ang), -np.sin(ang)], axis=-1)
    feats = jnp.asarray(feats_np)
    max_decay = math.log(DECAY_TARGET) / FAST_DECAY_PCT
    min_decay = math.log(DECAY_TARGET) / SLOW_DECAY_PCT
    deltas = jnp.abs(jnp.linspace(min_decay, max_decay, HY_WIDTH, dtype=F32))[None, :]
    w1p = _pad2(w1, LANES, LANES)
    w2p = _pad2(w2, LANES, LANES)
    w3p = _pad2(w3, LANES, 2 * HY_WIDTH)
    pad1 = lambda v: _pad2(v[None, :], 1, LANES)
    fixed = lambda i: (0, 0)
    vec = pl.BlockSpec((1, LANES), fixed)
    mat = pl.BlockSpec((LANES, LANES), fixed)
    return pl.pallas_call(
        _filter_kernel,
        grid=(seq_len // tl,),
        in_specs=[pl.BlockSpec((tl, LANES), lambda i: (i, 0)), mat, vec, vec, mat, vec, vec,
                  pl.BlockSpec((LANES, 2 * HY_WIDTH), fixed),
                  pl.BlockSpec((1, 2 * HY_WIDTH), fixed),
                  pl.BlockSpec((1, HY_WIDTH), fixed)],
        out_specs=pl.BlockSpec((2, tl, HY_WIDTH), lambda i: (0, i, 0)),
        out_shape=jax.ShapeDtypeStruct((2, seq_len, HY_WIDTH), F32),
        compiler_params=_params(("parallel",)),
        name="hyena_filter",
    )(feats, w1p, pad1(b1), pad1(f1), w2p, pad1(b2), pad1(f2), w3p, b3[None, :], deltas)


def _dft_tables(seq_len):
    n = 2 * seq_len
    n2 = FFT_N2
    n1 = n // n2
    n1h = n1 // 2
    a1 = (2.0 * np.pi / n1) * np.outer(np.arange(n1), np.arange(n1))
    f1r, f1i = np.cos(a1), -np.sin(a1)
    fwd1 = np.concatenate([f1r[:, :n1h], f1i[:, :n1h]], axis=0)
    inv1 = np.concatenate([f1r[:n1h, :], f1i[:n1h, :]], axis=1) / n
    a2 = (2.0 * np.pi / n2) * np.outer(np.arange(n2), np.arange(n2))
    f2 = np.concatenate([np.cos(a2), -np.sin(a2)], axis=0)
    at = (2.0 * np.pi / n) * np.outer(np.arange(n1), np.arange(n2))
    twr = np.broadcast_to(np.cos(at)[:, :, None], (n1, n2, LANES))
    twi = np.broadcast_to(-np.sin(at)[:, :, None], (n1, n2, LANES))
    return (jnp.asarray(fwd1, BF16), jnp.asarray(inv1, BF16), jnp.asarray(f2, BF16),
            jnp.asarray(twr, F32), jnp.asarray(twi, F32), n1, n2)


def _fft1_kernel(u_ref, f_ref, ar_ref, ai_ref):
    r = jnp.dot(f_ref[...], u_ref[0].astype(BF16), preferred_element_type=F32)
    n1 = r.shape[0] // 2
    ar_ref[0] = r[:n1].astype(BF16)
    ai_ref[0] = r[n1:].astype(BF16)


def _fft1(u3, fwd1, tn=2048):
    b, n1h, cols = u3.shape
    n1 = 2 * n1h
    return pl.pallas_call(
        _fft1_kernel,
        grid=(b, cols // tn),
        in_specs=[pl.BlockSpec((1, n1h, tn), lambda bi, j: (bi, 0, j)),
                  pl.BlockSpec((2 * n1, n1h), lambda bi, j: (0, 0))],
        out_specs=[pl.BlockSpec((1, n1, tn), lambda bi, j: (bi, 0, j))] * 2,
        out_shape=[jax.ShapeDtypeStruct((b, n1, cols), BF16)] * 2,
        compiler_params=_params(("parallel", "parallel")),
        name="fft_stage1",
    )(u3, fwd1)


def _tile_lanes(x, width):
    return jnp.concatenate([x] * (width // x.shape[1]), axis=1)


def _dft2_forward(ar, ai, twr, twi, f2):
    n2, c = ar.shape
    ar, ai = ar.astype(F32), ai.astype(F32)
    tr, ti = _tile_lanes(twr, c), _tile_lanes(twi, c)
    cat = jnp.concatenate([ar * tr - ai * ti, ar * ti + ai * tr], axis=1).astype(BF16)
    p = jnp.dot(f2, cat, preferred_element_type=F32)
    return p[:n2, :c] - p[n2:, c:], p[:n2, c:] + p[n2:, :c]


def _fft2_filter_kernel(ar_ref, ai_ref, twr_ref, twi_ref, f2_ref, kr_ref, ki_ref, *, kb):
    f2 = f2_ref[...]
    for j in range(kb):
        fr, fi = _dft2_forward(ar_ref[0, j], ai_ref[0, j], twr_ref[j], twi_ref[j], f2)
        gr, gi = _dft2_forward(ar_ref[1, j], ai_ref[1, j], twr_ref[j], twi_ref[j], f2)
        kr_ref[j] = fr + gr
        ki_ref[j] = fi - gi


def _fft2_filter(ar, ai, twr, twi, f2, kb=8):
    _, n1, n2, c = ar.shape
    a_spec = pl.BlockSpec((2, kb, n2, c), lambda k: (0, k, 0, 0))
    t_spec = pl.BlockSpec((kb, n2, LANES), lambda k: (k, 0, 0))
    o_spec = pl.BlockSpec((kb, n2, c), lambda k: (k, 0, 0))
    return pl.pallas_call(
        functools.partial(_fft2_filter_kernel, kb=kb),
        grid=(n1 // kb,),
        in_specs=[a_spec, a_spec, t_spec, t_spec, pl.BlockSpec((2 * n2, n2), lambda k: (0, 0))],
        out_specs=[o_spec, o_spec],
        out_shape=[jax.ShapeDtypeStruct((n1, n2, c), F32)] * 2,
        compiler_params=_params(("parallel",)),
        name="fft_filter_spectrum",
    )(ar, ai, twr, twi, f2)


def _fft2_kernel(ar_ref, ai_ref, twr_ref, twi_ref, kr_ref, ki_ref, f2_ref, br_ref, bi_ref, *, kb):
    f2 = f2_ref[...]
    for j in range(kb):
        twr, twi = twr_ref[j], twi_ref[j]
        xr, xi = _dft2_forward(ar_ref[0, j], ai_ref[0, j], twr, twi, f2)
        n2, c = xr.shape
        kr, ki = kr_ref[j], ki_ref[j]
        cat = jnp.concatenate([xr * kr - xi * ki, xr * ki + xi * kr], axis=1).astype(BF16)
        p = jnp.dot(f2, cat, preferred_element_type=F32)
        b_r = p[:n2, :c] + p[n2:, c:]
        b_i = p[:n2, c:] - p[n2:, :c]
        tr, ti = _tile_lanes(twr, c), _tile_lanes(twi, c)
        br_ref[0, j] = (b_r * tr + b_i * ti).astype(BF16)
        bi_ref[0, j] = (b_i * tr - b_r * ti).astype(BF16)


def _fft2(ar, ai, twr, twi, kr, ki, f2, kb=8):
    b, n1, n2, c = ar.shape
    a_spec = pl.BlockSpec((1, kb, n2, c), lambda bi, k: (bi, k, 0, 0))
    t_spec = pl.BlockSpec((kb, n2, LANES), lambda bi, k: (k, 0, 0))
    k_spec = pl.BlockSpec((kb, n2, c), lambda bi, k: (k, 0, 0))
    return pl.pallas_call(
        functools.partial(_fft2_kernel, kb=kb),
        grid=(b, n1 // kb),
        in_specs=[a_spec, a_spec, t_spec, t_spec, k_spec, k_spec,
                  pl.BlockSpec((2 * n2, n2), lambda bi, k: (0, 0))],
        out_specs=[a_spec, a_spec],
        out_shape=[jax.ShapeDtypeStruct((b, n1, n2, c), BF16)] * 2,
        compiler_params=_params(("parallel", "parallel")),
        name="fft_stage2",
    )(ar, ai, twr, twi, kr, ki, f2)


def _fft3_kernel(br_ref, bi_ref, g_ref, u_ref, x0_ref, skip_ref, y_ref):
    g = g_ref[...]
    n1 = g.shape[1] // 2
    y = (jnp.dot(g[:, :n1], br_ref[0], preferred_element_type=F32)
         + jnp.dot(g[:, n1:], bi_ref[0], preferred_element_type=F32))
    y_ref[0] = (x0_ref[0] * (y + u_ref[0] * skip_ref[...])).astype(BF16)


def _fft3(br, bi, inv1, u3, x03, skip_t, tn=2048):
    b, n1, cols = br.shape
    n1h = n1 // 2
    big = pl.BlockSpec((1, n1, tn), lambda bi_, j: (bi_, 0, j))
    half = pl.BlockSpec((1, n1h, tn), lambda bi_, j: (bi_, 0, j))
    return pl.pallas_call(
        _fft3_kernel,
        grid=(b, cols // tn),
        in_specs=[big, big, pl.BlockSpec((n1h, 2 * n1), lambda bi_, j: (0, 0)), half, half,
                  pl.BlockSpec((1, tn), lambda bi_, j: (0, 0))],
        out_specs=half,
        out_shape=jax.ShapeDtypeStruct((b, n1h, cols), BF16),
        compiler_params=_params(("parallel", "parallel")),
        name="fft_stage3",
    )(br, bi, inv1, u3, x03, skip_t)


def _hyena_long_conv(u, x0, filters, skip, batch, seq_len):
    c = HY_WIDTH
    fwd1, inv1, f2, twr, twi, n1, n2 = _dft_tables(seq_len)
    n1h = n1 // 2
    fa_r, fa_i = _fft1(filters.reshape(2, n1h, n2 * c), fwd1)
    kr, ki = _fft2_filter(fa_r.reshape(2, n1, n2, c), fa_i.reshape(2, n1, n2, c), twr, twi, f2)
    u3 = u.reshape(batch, n1h, n2 * c)
    a_r, a_i = _fft1(u3, fwd1)
    b_r, b_i = _fft2(a_r.reshape(batch, n1, n2, c), a_i.reshape(batch, n1, n2, c),
                     twr, twi, kr, ki, f2)
    tn = 2048
    skip_t = jnp.tile(skip[None, :], (1, tn // c))
    yh = _fft3(b_r.reshape(batch, n1, n2 * c), b_i.reshape(batch, n1, n2 * c), inv1,
               u3, x0.reshape(batch, n1h, n2 * c), skip_t, tn)
    return yh.reshape(batch * seq_len, c)


def _attn_kernel(q1_ref, q2_ref, k_ref, vt_ref, lam_ref, sg_ref, o_ref,
                 m1, a1, m2, a2, *, chunk):
    ki = pl.program_id(3)

    @pl.when(ki == 0)
    def _():
        for m, a in ((m1, a1), (m2, a2)):
            m[...] = jnp.full(m.shape, NEG_INF, F32)
            a[...] = jnp.zeros(a.shape, F32)

    q_refs = (q1_ref, q2_ref)
    stats = [[m1[...], a1[...]], [m2[...], a2[...]]]
    n_chunks = k_ref.shape[0] // chunk
    units = [(c, comp) for c in range(n_chunks) for comp in (0, 1)]

    def scores(unit):
        c, comp = unit
        return jnp.dot(k_ref[c * chunk:(c + 1) * chunk, :], q_refs[comp][...],
                       preferred_element_type=F32)

    def softmax(unit, s):
        m_prev = stats[unit[1]][0]
        m_new = jnp.maximum(m_prev, jnp.max(s, axis=0, keepdims=True))
        stats[unit[1]][0] = m_new
        return jnp.exp2(s - m_new).astype(BF16), jnp.exp2(m_prev - m_new)

    def values(unit, p, alpha):
        c, comp = unit
        stats[comp][1] = alpha * stats[comp][1] + jnp.dot(
            vt_ref[:, c * chunk:(c + 1) * chunk], p, preferred_element_type=F32)

    s_next = scores(units[0])
    pending = None
    for i, unit in enumerate(units):
        s_cur = s_next
        if i + 1 < len(units):
            s_next = scores(units[i + 1])
        if pending is not None:
            values(*pending)
        pending = (unit,) + softmax(unit, s_cur)
    values(*pending)
    for (m, a), (mv, av) in zip(((m1, a1), (m2, a2)), stats):
        m[...] = mv
        a[...] = av

    @pl.when(ki == pl.num_programs(3) - 1)
    def _():
        lv = lam_ref[...]
        lam = (jnp.exp(jnp.sum(lv[0:1] * lv[1:2], axis=1, keepdims=True))
               - jnp.exp(jnp.sum(lv[2:3] * lv[3:4], axis=1, keepdims=True)) + LAMBDA_INIT)
        d = ATT_V_DIM
        o = (a1[:d, :] / a1[d:d + 1, :] - lam * (a2[:d, :] / a2[d:d + 1, :]))
        inv = lax.rsqrt(jnp.mean(o * o, axis=0, keepdims=True) + SUBLN_EPS)
        g = pltpu.repeat(sg_ref[...], o.shape[1] // LANES, 1)
        o_ref[...] = (o * inv * g * (1.0 - LAMBDA_INIT)).astype(BF16)


def _attention(qt1, qt2, k, vt, lam_vecs, subln_g, batch, seq_len, tq=1024, tk=2048, chunk=512):
    tk = min(tk, seq_len)
    nq, nk = seq_len // tq, seq_len // tk
    q_spec = pl.BlockSpec((LANES, tq), lambda b, h, i, j: (h, b * nq + i))
    sg = jnp.broadcast_to(subln_g[:, None], (ATT_V_DIM, LANES))
    return pl.pallas_call(
        functools.partial(_attn_kernel, chunk=chunk),
        grid=(batch, N_ATT_HEADS, nq, nk),
        in_specs=[q_spec, q_spec,
                  pl.BlockSpec((tk, LANES), lambda b, h, i, j: (b * nk + j, h)),
                  pl.BlockSpec((VT_ROWS, tk), lambda b, h, i, j: (h, b * nk + j)),
                  pl.BlockSpec((4, ATT_HEAD_DIM), lambda b, h, i, j: (0, 0)),
                  pl.BlockSpec((ATT_V_DIM, LANES), lambda b, h, i, j: (0, 0))],
        out_specs=q_spec,
        out_shape=jax.ShapeDtypeStruct((ATT_WIDTH, batch * seq_len), BF16),
        scratch_shapes=[pltpu.VMEM((1, tq), F32), pltpu.VMEM((VT_ROWS, tq), F32)] * 2,
        compiler_params=_params(("parallel", "parallel", "parallel", "arbitrary")),
        name="diff_attention",
    )(qt1, qt2, k, vt, lam_vecs, sg)


def _split_bf16(x):
    hi = x.astype(BF16)
    return hi, (x - hi.astype(F32)).astype(BF16)


def _outproj_kernel(yh_ref, ya_ref, x_ref, wo_ref, g_ref, wq_ref, kh_ref, kl_ref,
                    x2_ref, xnt_ref, st_ref):
    wo = wo_ref[...]
    x2 = (x_ref[...] + jnp.dot(yh_ref[...], wo[:HY_WIDTH], preferred_element_type=F32)
          + lax.dot_general(ya_ref[...], wo[HY_WIDTH:], (((0,), (0,)), ((), ())),
                            preferred_element_type=F32))
    x2_ref[...] = x2
    xn = _rms(x2, g_ref[...], EPS)
    xnt_ref[...] = xn.T.astype(BF16)
    q = jnp.dot(xn.astype(BF16), wq_ref[...], preferred_element_type=F32)
    for hc in range(2 * PEER_HEADS):
        qh, ql = _split_bf16(q[:, hc * PEER_HALF:(hc + 1) * PEER_HALF])
        kh, kl = kh_ref[hc], kl_ref[hc]
        st_ref[hc * PEER_KEYS:(hc + 1) * PEER_KEYS, :] = (
            _dot_nt(kh, qh) + _dot_nt(kh, ql) + _dot_nt(kl, qh))


def _outproj(yh, ya, x, wo_bf, g2, wq_bf, keys_hi, keys_lo, tm=512):
    t = x.shape[0]
    qw = 2 * PEER_HEADS * PEER_HALF
    sw = 2 * PEER_HEADS * PEER_KEYS
    row = lambda i: (i, 0)
    fixed = lambda i: (0, 0)
    fixed3 = lambda i: (0, 0, 0)
    return pl.pallas_call(
        _outproj_kernel,
        grid=(t // tm,),
        in_specs=[pl.BlockSpec((tm, HY_WIDTH), row),
                  pl.BlockSpec((ATT_WIDTH, tm), lambda i: (0, i)),
                  pl.BlockSpec((tm, D_MODEL), row),
                  pl.BlockSpec((HY_WIDTH + ATT_WIDTH, D_MODEL), fixed),
                  pl.BlockSpec((1, D_MODEL), fixed),
                  pl.BlockSpec((D_MODEL, qw), fixed),
                  pl.BlockSpec((2 * PEER_HEADS, PEER_KEYS, PEER_HALF), fixed3),
                  pl.BlockSpec((2 * PEER_HEADS, PEER_KEYS, PEER_HALF), fixed3)],
        out_specs=[pl.BlockSpec((tm, D_MODEL), row), pl.BlockSpec((D_MODEL, tm), lambda i: (0, i)),
                   pl.BlockSpec((sw, tm), lambda i: (0, i))],
        out_shape=[jax.ShapeDtypeStruct((t, D_MODEL), F32),
                   jax.ShapeDtypeStruct((D_MODEL, t), BF16),
                   jax.ShapeDtypeStruct((sw, t), F32)],
        compiler_params=_params(("parallel",)),
        name="outproj_peer_query",
    )(yh, ya, x, wo_bf, g2, wq_bf, keys_hi, keys_lo)


def _extract_max(pieces, iotas, total, exact):
    m = pieces[0]
    for p in pieces[1:]:
        m = jnp.maximum(m, p)
    m = jnp.max(m, axis=0, keepdims=True)
    hot = [p == m for p in pieces]
    if exact:
        cand = [jnp.where(h, io, total) for h, io in zip(hot, iotas)]
        idx = cand[0]
        for c in cand[1:]:
            idx = jnp.minimum(idx, c)
        idx = jnp.min(idx, axis=0, keepdims=True)
        hot = [io == idx for io in iotas]
    return m, hot, [jnp.where(h, NEG_INF, p) for h, p in zip(hot, pieces)]


def _topk_rank(s, exact, want_rank):
    rank = jnp.full(s.shape, float(PEER_TOPK), F32) if want_rank else None
    iota = [lax.broadcasted_iota(jnp.int32, s.shape, 0)]
    vals = []
    for r in range(PEER_TOPK):
        m, hot, (s,) = _extract_max([s], iota, s.shape[0], exact)
        vals.append(m)
        if want_rank:
            rank = jnp.where(hot[0], float(r), rank)
    return vals, rank


_PAIRS = [(k0, k1) for k0 in range(PEER_TOPK) for k1 in range(PEER_TOPK // (k0 + 1))]
_PAIR_ROWS = -(-len(_PAIRS) // SUBLANES) * SUBLANES


def _rows_to_tiles(rows, width):
    sub = lax.broadcasted_iota(jnp.int32, (SUBLANES, width), 0)
    tiles = []
    for t0 in range(0, len(rows), SUBLANES):
        tile = jnp.full((SUBLANES, width), NEG_INF, F32)
        for r in range(SUBLANES):
            if rows[t0 + r] is not None:
                tile = jnp.where(sub == r, rows[t0 + r], tile)
        tiles.append(tile)
    return tiles


def _route_tables(s0, s1, exact):
    tt = s0.shape[1]
    v0, rank0 = _topk_rank(s0, exact, want_rank=exact)
    v1, rank1 = _topk_rank(s1, exact, want_rank=True)
    pad = [None] * (_PAIR_ROWS - len(_PAIRS))
    tiles = [a + b for a, b in zip(_rows_to_tiles([v0[k0] for k0, _ in _PAIRS] + pad, tt),
                                   _rows_to_tiles([v1[k1] for _, k1 in _PAIRS] + pad, tt))]
    iotas = [lax.broadcasted_iota(jnp.int32, (SUBLANES, tt), 0) + t0
             for t0 in range(0, _PAIR_ROWS, SUBLANES)]
    cmax = v0[0] + v1[0]
    z = None
    for r in range(PEER_TOPK):
        m, _, tiles = _extract_max(tiles, iotas, _PAIR_ROWS, exact)
        z = jnp.exp(m - cmax) if z is None else z + jnp.exp(m - cmax)
    cnt = jnp.zeros(s0.shape, F32)
    total = None
    for k0 in range(PEER_TOPK):
        rows = [r for r, (a, _) in enumerate(_PAIRS) if a == k0]
        taken = None
        for ti in sorted({r // SUBLANES for r in rows}):
            inside = (iotas[ti] >= rows[0]) & (iotas[ti] <= rows[-1])
            part = jnp.sum(jnp.where(inside & (tiles[ti] == NEG_INF), 1.0, 0.0),
                           axis=0, keepdims=True)
            taken = part if taken is None else taken + part
        total = taken if total is None else total + taken
        cnt = jnp.where((rank0 == float(k0)) if exact else (s0 == v0[k0]), taken, cnt)
    k = float(PEER_TOPK)
    count = lambda mask: jnp.sum(jnp.where(mask, 1.0, 0.0), axis=0, keepdims=True)
    top0 = (rank0 < k) if exact else (s0 >= v0[PEER_TOPK - 1])
    ok = (total == k) & (count(top0) == k) & (count(rank1 < k) == k)
    return cnt, jnp.exp(s0 - v0[0]) / z, rank1, jnp.exp(s1 - v1[0]), ok


def _route_kernel(st_ref, cnt_ref, w0_ref, r1_ref, e1_ref):
    def emit(exact):
        cnt, w0, rank1, e1, ok = _route_tables(st_ref[:PEER_KEYS, :], st_ref[PEER_KEYS:, :], exact)
        cnt_ref[0] = cnt
        w0_ref[0] = w0 * GELU_OUT_SCALE
        r1_ref[0] = rank1.astype(BF16)
        e1_ref[0] = e1.astype(BF16)
        return ok

    ok = emit(exact=False)
    clean = jnp.min(jnp.where(ok, 1, 0))

    @pl.when(clean == 0)
    def _():
        emit(exact=True)


def _route(st, tt=512):
    sw, t = st.shape
    o_spec = pl.BlockSpec((1, PEER_KEYS, tt), lambda i, h: (h, 0, i))
    tab = lambda dt: jax.ShapeDtypeStruct((PEER_HEADS, PEER_KEYS, t), dt)
    return pl.pallas_call(
        _route_kernel,
        grid=(t // tt, PEER_HEADS),
        in_specs=[pl.BlockSpec((2 * PEER_KEYS, tt), lambda i, h: (h, i))],
        out_specs=[o_spec] * 4,
        out_shape=[tab(F32), tab(F32), tab(BF16), tab(BF16)],
        compiler_params=_params(("parallel", "parallel")),
        name="peer_route",
    )(st)


GELU_IN_SCALE = 2.0 ** -0.5
GELU_OUT_SCALE = 0.5 * 2.0 ** 0.5


def _gelu_core(y):
    return y * (1.0 + lax.erf(y))


def _peer_kernel(xnt_ref, u_ref, vt_ref, cnt_ref, w0_ref, r1_ref, e1_ref, x2_ref, gf_ref,
                 y_ref, acc_ref, *, rows_per_step, rows_per_unit):
    e = pl.program_id(1)

    @pl.when(e == 0)
    def _():
        acc_ref[...] = jnp.zeros(acc_ref.shape, F32)

    tt = xnt_ref.shape[1]
    unit = rows_per_unit * PEER_KEYS
    n_units = rows_per_step // rows_per_unit

    def hidden(s):
        return jnp.dot(u_ref[s * unit:(s + 1) * unit, :], xnt_ref[...],
                       preferred_element_type=F32)

    def activation(s, hid):
        gates = []
        for ii in range(s * rows_per_unit, (s + 1) * rows_per_unit):
            g = None
            for h in range(PEER_HEADS):
                cnt_row = jnp.broadcast_to(cnt_ref[h, ii:ii + 1, :], (PEER_KEYS, tt)).astype(BF16)
                w_row = jnp.broadcast_to(w0_ref[h, ii:ii + 1, :], (PEER_KEYS, tt)).astype(BF16)
                term = jnp.where(r1_ref[h] < cnt_row, e1_ref[h] * w_row, jnp.zeros((), BF16))
                g = term if g is None else g + term
            gates.append(g)
        return _gelu_core(hid).astype(BF16) * jnp.concatenate(gates, axis=0)

    def mix(s, act):
        return jnp.dot(vt_ref[:, s * unit:(s + 1) * unit], act, preferred_element_type=F32)

    acc = acc_ref[...]
    h_next = hidden(0)
    pending = None
    for s in range(n_units):
        h_cur = h_next
        if s + 1 < n_units:
            h_next = hidden(s + 1)
        if pending is not None:
            acc = acc + mix(*pending)
        pending = (s, activation(s, h_cur))
    acc_ref[...] = acc + mix(*pending)

    @pl.when(e == pl.num_programs(1) - 1)
    def _():
        x3 = x2_ref[...] + acc_ref[...].T
        y_ref[...] = _rms(x3, gf_ref[...], EPS)


def _peer(xnt, u_bf, vt_bf, cnt, w0, r1, e1, x2, gf, tt=512, rows_per_step=16, rows_per_unit=2):
    t = xnt.shape[1]
    te = rows_per_step * PEER_KEYS
    tok = lambda i, e: (i, 0)
    tab_i = pl.BlockSpec((PEER_HEADS, rows_per_step, tt), lambda i, e: (0, e, i))
    tab_j = pl.BlockSpec((PEER_HEADS, PEER_KEYS, tt), lambda i, e: (0, 0, i))
    return pl.pallas_call(
        functools.partial(_peer_kernel, rows_per_step=rows_per_step,
                          rows_per_unit=rows_per_unit),
        grid=(t // tt, PEER_EXPERTS // te),
        in_specs=[pl.BlockSpec((D_MODEL, tt), lambda i, e: (0, i)),
                  pl.BlockSpec((te, D_MODEL), lambda i, e: (e, 0)),
                  pl.BlockSpec((D_MODEL, te), lambda i, e: (0, e)),
                  tab_i, tab_i, tab_j, tab_j,
                  pl.BlockSpec((tt, D_MODEL), tok),
                  pl.BlockSpec((1, D_MODEL), lambda i, e: (0, 0))],
        out_specs=pl.BlockSpec((tt, D_MODEL), tok),
        out_shape=jax.ShapeDtypeStruct((t, D_MODEL), F32),
        scratch_shapes=[pltpu.VMEM((D_MODEL, tt), F32)],
        compiler_params=_params(("parallel", "arbitrary")),
        name="peer_dense",
    )(xnt, u_bf, vt_bf, cnt, w0, r1, e1, x2, gf)


def _rope_tables(seq_len):
    f32 = np.float32
    inv = f32(ROPE_THETA) ** (-np.arange(0, ROT_DIM, 2, dtype=f32) / f32(ROT_DIM))
    ang = (np.arange(seq_len, dtype=f32)[:, None] * inv[None, :]).astype(np.float64)
    cos, sin = np.cos(ang), np.sin(ang)
    half = ROT_DIM // 2
    comp = np.zeros((3, seq_len, ATT_HEAD_DIM))
    comp[0, :, :half], comp[0, :, half:ROT_DIM], comp[0, :, ROT_DIM:] = cos, cos, 1.0
    comp[1, :, :half] = -sin
    comp[2, :, half:ROT_DIM] = sin
    both = np.concatenate([comp, comp], axis=2).astype(f32)
    return jnp.asarray(both[0]), jnp.asarray(both[1]), jnp.asarray(both[2])


def _trunk(x, p):
    batch, seq_len, _ = x.shape
    xf = x.reshape(batch * seq_len, D_MODEL)
    rope_c, rope_sa, rope_sb = _rope_tables(seq_len)
    zh, qt1, qt2, k, vt = _inproj(xf, p["norm1_g"], p["w_in"], rope_c, rope_sa, rope_sb, seq_len)
    u, x0 = _shortconv(zh, p["hy_conv_w"], p["hy_conv_b"], seq_len)
    filters = _hyena_filters(seq_len, *p["filter"])
    yh = _hyena_long_conv(u, x0, filters, p["hy_skip"], batch, seq_len)
    ya = _attention(qt1, qt2, k, vt, p["lam_vecs"], p["subln_g"], batch, seq_len)
    x2, xnt, st = _outproj(yh, ya, xf, p["w_out"], p["norm2_g"], p["wq"],
                          p["keys_hi"], p["keys_lo"])
    cnt, w0, r1, e1 = _route(st)
    y = _peer(xnt, p["peer_u"], p["peer_vt"], cnt, w0, r1, e1, x2, p["normf_g"])
    return y.reshape(batch, seq_len, D_MODEL)


def kernel(x_prompt, x_sample, norm1_g, w_in, hy_conv_w, hy_conv_b, hf_w1, hf_b1, hf_freq1,
           hf_w2, hf_b2, hf_freq2, hf_w3, hf_b3, hy_skip, lam_q1, lam_k1, lam_q2, lam_k2,
           subln_g, w_out, norm2_g, peer_wq, peer_keys, peer_u, peer_v, normf_g):
    l = 0
    keys = peer_keys[l].reshape(2 * PEER_HEADS, PEER_KEYS, PEER_HALF)
    keys_hi = keys.astype(BF16)
    p = {
        "norm1_g": norm1_g[l][None, :],
        "w_in": w_in[l].astype(BF16),
        "hy_conv_w": hy_conv_w[l],
        "hy_conv_b": hy_conv_b[l][None, :],
        "filter": (hf_w1[l], hf_b1[l], hf_freq1[l], hf_w2[l], hf_b2[l], hf_freq2[l],
                   hf_w3[l], hf_b3[l]),
        "hy_skip": hy_skip[l],
        "lam_vecs": jnp.stack([lam_q1[l], lam_k1[l], lam_q2[l], lam_k2[l]]),
        "subln_g": subln_g[l],
        "w_out": w_out[l].astype(BF16),
        "norm2_g": norm2_g[l][None, :],
        "wq": peer_wq[l].astype(BF16),
        "keys_hi": keys_hi,
        "keys_lo": (keys - keys_hi.astype(F32)).astype(BF16),
        "peer_u": (peer_u[l] * GELU_IN_SCALE).astype(BF16),
        "peer_vt": peer_v[l].astype(BF16).T,
        "normf_g": normf_g[None, :],
    }
    return (_trunk(x_prompt, p), _trunk(x_sample, p))
```

```python
import functools
import math

import jax
import jax.numpy as jnp
import numpy as np
from jax import lax
from jax.experimental import pallas as pl
from jax.experimental.pallas import tpu as pltpu

F32 = jnp.float32
BF16 = jnp.bfloat16

D_MODEL = 1024
HY_WIDTH = 512
ATT_WIDTH = 512
N_ATT_HEADS = 4
ATT_HEAD_DIM = 64
ATT_V_DIM = 2 * ATT_HEAD_DIM
ROT_DIM = ATT_HEAD_DIM // 4
ROPE_THETA = 500000.0
IN_WIDTH = 3 * HY_WIDTH + 3 * ATT_WIDTH
FILTER_BANDS = 8
FILTER_EMB = 1 + 2 * FILTER_BANDS
FILTER_HIDDEN = 64
FAST_DECAY_PCT = 0.3
SLOW_DECAY_PCT = 1.5
DECAY_TARGET = 1e-2
PEER_HEADS = 8
PEER_KEYS = 128
PEER_EXPERTS = PEER_KEYS * PEER_KEYS
PEER_TOPK = 16
PEER_HALF = 128
EPS = 1e-6
SUBLN_EPS = 1e-5
LAMBDA_INIT = 0.8 - 0.6 * math.exp(-0.3 * 0)

LANES = 128
SUBLANES = 8
VMEM_LIMIT = 56 * 1024 * 1024
FFT_N2 = 128
VT_ROWS = ATT_V_DIM + 2 * SUBLANES
STABILISER_HEADROOM = 64.0
NEG_INF = float("-inf")


def _params(semantics, vmem=VMEM_LIMIT):
    return pltpu.CompilerParams(dimension_semantics=semantics, vmem_limit_bytes=vmem)


def _rms(x, g, eps):
    return x * lax.rsqrt(jnp.mean(x * x, axis=-1, keepdims=True) + eps) * g


def _dot_nt(a, b):
    return lax.dot_general(a, b, (((1,), (1,)), ((), ())), preferred_element_type=F32)


def _inproj_kernel(x_ref, g_ref, w_ref, c_ref, sa_ref, sb_ref,
                   zh_ref, qt1_ref, qt2_ref, k_ref, vt_ref):
    h = _rms(x_ref[...], g_ref[...], EPS).astype(BF16)
    z = jnp.dot(h, w_ref[...], preferred_element_type=F32)
    hy3 = 3 * HY_WIDTH
    zh_ref[...] = z[:, :hy3]
    c, sa, sb = c_ref[...], sa_ref[...], sb_ref[...]
    scale = ATT_HEAD_DIM ** -0.5 * math.log2(math.e)

    def rope(xb):
        return (xb * c + pltpu.roll(xb, LANES - ROT_DIM // 2, 1) * sa
                + pltpu.roll(xb, ROT_DIM // 2, 1) * sb)

    first = lax.broadcasted_iota(jnp.int32, (LANES, z.shape[0]), 0) < ATT_HEAD_DIM
    for hd in range(N_ATT_HEADS):
        lo, hi = hd * LANES, (hd + 1) * LANES
        qt = (rope(z[:, hy3 + lo:hy3 + hi]) * scale).T
        qt1_ref[lo:hi, :] = jnp.where(first, qt, 0.0).astype(BF16)
        qt2_ref[lo:hi, :] = jnp.where(first, 0.0, qt).astype(BF16)
        k_ref[:, lo:hi] = rope(z[:, hy3 + ATT_WIDTH + lo:hy3 + ATT_WIDTH + hi]).astype(BF16)
        vlo = hd * VT_ROWS
        vt_ref[vlo:vlo + ATT_V_DIM, :] = (
            z[:, hy3 + 2 * ATT_WIDTH + lo:hy3 + 2 * ATT_WIDTH + hi].T.astype(BF16))
        vt_ref[vlo + ATT_V_DIM:vlo + VT_ROWS, :] = jnp.ones(
            (VT_ROWS - ATT_V_DIM, z.shape[0]), BF16)


def _inproj(x, g, w_bf, rope_c, rope_sa, rope_sb, seq_len, tm=512):
    t = x.shape[0]
    nseq = seq_len // tm
    row = lambda i: (i, 0)
    col = lambda i: (0, i)
    fixed = lambda i: (0, 0)
    pos = lambda i: (i % nseq, 0)
    hy3 = 3 * HY_WIDTH
    tr_spec = pl.BlockSpec((ATT_WIDTH, tm), col)
    tr_shape = jax.ShapeDtypeStruct((ATT_WIDTH, t), BF16)
    return pl.pallas_call(
        _inproj_kernel,
        grid=(t // tm,),
        in_specs=[pl.BlockSpec((tm, D_MODEL), row),
                  pl.BlockSpec((1, D_MODEL), fixed),
                  pl.BlockSpec((D_MODEL, IN_WIDTH), fixed),
                  pl.BlockSpec((tm, LANES), pos),
                  pl.BlockSpec((tm, LANES), pos),
                  pl.BlockSpec((tm, LANES), pos)],
        out_specs=[pl.BlockSpec((tm, hy3), row), tr_spec, tr_spec,
                   pl.BlockSpec((tm, ATT_WIDTH), row),
                   pl.BlockSpec((N_ATT_HEADS * VT_ROWS, tm), col)],
        out_shape=[jax.ShapeDtypeStruct((t, hy3), F32), tr_shape, tr_shape,
                   jax.ShapeDtypeStruct((t, ATT_WIDTH), BF16),
                   jax.ShapeDtypeStruct((N_ATT_HEADS * VT_ROWS, t), BF16)],
        compiler_params=_params(("parallel",)),
        name="inproj",
    )(x, g, w_bf, rope_c, rope_sa, rope_sb)


def _shortconv_kernel(z_ref, zp_ref, zn_ref, w_ref, b_ref, u_ref, x0_ref, *, blocks_per_seq):
    i = pl.program_id(0)
    z = z_ref[...]
    tl = z.shape[0]
    row = lax.broadcasted_iota(jnp.int32, z.shape, 0)
    at_start = (i % blocks_per_seq) == 0
    at_end = (i % blocks_per_seq) == blocks_per_seq - 1
    prev_row = jnp.where(at_start, 0.0, zp_ref[SUBLANES - 1:SUBLANES, :])
    next_row = jnp.where(at_end, 0.0, zn_ref[0:1, :])
    zm = jnp.where(row == 0, prev_row, pltpu.roll(z, 1, 0))
    zp = jnp.where(row == tl - 1, next_row, pltpu.roll(z, tl - 1, 0))
    w = w_ref[...]
    conv = zm * w[0:1] + z * w[1:2] + zp * w[2:3] + b_ref[...]
    x0_ref[...] = conv[:, :HY_WIDTH]
    u_ref[...] = conv[:, 2 * HY_WIDTH:] * conv[:, HY_WIDTH:2 * HY_WIDTH]


def _shortconv(zh, w, b, seq_len, tl=512):
    t, hy3 = zh.shape
    per8 = tl // SUBLANES
    last8 = t // SUBLANES - 1
    return pl.pallas_call(
        functools.partial(_shortconv_kernel, blocks_per_seq=seq_len // tl),
        grid=(t // tl,),
        in_specs=[pl.BlockSpec((tl, hy3), lambda i: (i, 0)),
                  pl.BlockSpec((SUBLANES, hy3), lambda i: (jnp.maximum(i * per8 - 1, 0), 0)),
                  pl.BlockSpec((SUBLANES, hy3), lambda i: (jnp.minimum((i + 1) * per8, last8), 0)),
                  pl.BlockSpec((3, hy3), lambda i: (0, 0)),
                  pl.BlockSpec((1, hy3), lambda i: (0, 0))],
        out_specs=[pl.BlockSpec((tl, HY_WIDTH), lambda i: (i, 0))] * 2,
        out_shape=[jax.ShapeDtypeStruct((t, HY_WIDTH), F32)] * 2,
        compiler_params=_params(("parallel",)),
        name="shortconv",
    )(zh, zh, zh, w, b)


def _filter_kernel(f_ref, w1_ref, b1_ref, f1_ref, w2_ref, b2_ref, f2_ref, w3_ref, b3_ref,
                   dl_ref, h_ref):
    i = pl.program_id(0)
    hp = lax.Precision.HIGHEST
    feats = f_ref[...]
    tl = feats.shape[0]
    h = jnp.sin(f1_ref[...] * (jnp.dot(feats, w1_ref[...], precision=hp,
                                       preferred_element_type=F32) + b1_ref[...]))
    h = jnp.sin(f2_ref[...] * (jnp.dot(h, w2_ref[...], precision=hp,
                                       preferred_element_type=F32) + b2_ref[...]))
    h3 = jnp.dot(h, w3_ref[...], precision=hp, preferred_element_type=F32) + b3_ref[...]
    decay = jnp.exp(-feats[:, 0:1] * dl_ref[...])
    h_ref[0] = h3[:, :HY_WIDTH] * decay
    row = lax.broadcasted_iota(jnp.int32, (tl, HY_WIDTH), 0) + i * tl
    h_ref[1] = jnp.where(row == 0, 0.0, h3[:, HY_WIDTH:] * decay)


def _pad2(a, rows, cols):
    return jnp.pad(a, ((0, rows - a.shape[0]), (0, cols - a.shape[1])))


def _hyena_filters(seq_len, w1, b1, f1, w2, b2, f2, w3, b3, tl=512):
    f32 = np.float32
    t = np.linspace(0.0, 1.0, seq_len).astype(f32)[:, None]
    w = f32(2.0 * math.pi) * np.arange(seq_len, dtype=f32) / f32(seq_len)
    bands = np.linspace(1e-4, FILTER_BANDS - 1, FILTER_BANDS).astype(f32)
    ang = (w[:, None] * bands[None, :]).astype(np.float64)
    feats_np = np.zeros((seq_len, LANES), f32)
    feats_np[:, :FILTER_EMB] = np.concatenate([t, np.cos(ang), -np.sin(ang)], axis=-1)
    feats = jnp.asarray(feats_np)
    max_decay = math.log(DECAY_TARGET) / FAST_DECAY_PCT
    min_decay = math.log(DECAY_TARGET) / SLOW_DECAY_PCT
    deltas = jnp.abs(jnp.linspace(min_decay, max_decay, HY_WIDTH, dtype=F32))[None, :]
    w1p = _pad2(w1, LANES, LANES)
    w2p = _pad2(w2, LANES, LANES)
    w3p = _pad2(w3, LANES, 2 * HY_WIDTH)
    pad1 = lambda v: _pad2(v[None, :], 1, LANES)
    fixed = lambda i: (0, 0)
    vec = pl.BlockSpec((1, LANES), fixed)
    mat = pl.BlockSpec((LANES, LANES), fixed)
    return pl.pallas_call(
        _filter_kernel,
        grid=(seq_len // tl,),
        in_specs=[pl.BlockSpec((tl, LANES), lambda i: (i, 0)), mat, vec, vec, mat, vec, vec,
                  pl.BlockSpec((LANES, 2 * HY_WIDTH), fixed),
                  pl.BlockSpec((1, 2 * HY_WIDTH), fixed),
                  pl.BlockSpec((1, HY_WIDTH), fixed)],
        out_specs=pl.BlockSpec((2, tl, HY_WIDTH), lambda i: (0, i, 0)),
        out_shape=jax.ShapeDtypeStruct((2, seq_len, HY_WIDTH), F32),
        compiler_params=_params(("parallel",)),
        name="hyena_filter",
    )(feats, w1p, pad1(b1), pad1(f1), w2p, pad1(b2), pad1(f2), w3p, b3[None, :], deltas)


def _dft_tables(seq_len):
    n = 2 * seq_len
    n2 = FFT_N2
    n1 = n // n2
    n1h = n1 // 2
    a1 = (2.0 * np.pi / n1) * np.outer(np.arange(n1), np.arange(n1))
    f1r, f1i = np.cos(a1), -np.sin(a1)
    fwd1 = np.concatenate([f1r[:, :n1h], f1i[:, :n1h]], axis=0)
    inv1 = np.concatenate([f1r[:n1h, :], f1i[:n1h, :]], axis=1) / n
    a2 = (2.0 * np.pi / n2) * np.outer(np.arange(n2), np.arange(n2))
    f2 = np.concatenate([np.cos(a2), -np.sin(a2)], axis=0)
    at = (2.0 * np.pi / n) * np.outer(np.arange(n1), np.arange(n2))
    twr = np.broadcast_to(np.cos(at)[:, :, None], (n1, n2, LANES))
    twi = np.broadcast_to(-np.sin(at)[:, :, None], (n1, n2, LANES))
    return (jnp.asarray(fwd1, BF16), jnp.asarray(inv1, BF16), jnp.asarray(f2, BF16),
            jnp.asarray(twr, F32), jnp.asarray(twi, F32), n1, n2)


def _fft1_kernel(u_ref, f_ref, ar_ref, ai_ref):
    r = jnp.dot(f_ref[...], u_ref[0].astype(BF16), preferred_element_type=F32)
    n1 = r.shape[0] // 2
    ar_ref[0] = r[:n1].astype(BF16)
    ai_ref[0] = r[n1:].astype(BF16)


def _fft1(u3, fwd1, tn=2048):
    b, n1h, cols = u3.shape
    n1 = 2 * n1h
    return pl.pallas_call(
        _fft1_kernel,
        grid=(b, cols // tn),
        in_specs=[pl.BlockSpec((1, n1h, tn), lambda bi, j: (bi, 0, j)),
                  pl.BlockSpec((2 * n1, n1h), lambda bi, j: (0, 0))],
        out_specs=[pl.BlockSpec((1, n1, tn), lambda bi, j: (bi, 0, j))] * 2,
        out_shape=[jax.ShapeDtypeStruct((b, n1, cols), BF16)] * 2,
        compiler_params=_params(("parallel", "parallel")),
        name="fft_stage1",
    )(u3, fwd1)


def _tile_lanes(x, width):
    return jnp.concatenate([x] * (width // x.shape[1]), axis=1)


def _dft2_forward(ar, ai, twr, twi, f2):
    n2, c = ar.shape
    ar, ai = ar.astype(F32), ai.astype(F32)
    tr, ti = _tile_lanes(twr, c), _tile_lanes(twi, c)
    cat = jnp.concatenate([ar * tr - ai * ti, ar * ti + ai * tr], axis=1).astype(BF16)
    p = jnp.dot(f2, cat, preferred_element_type=F32)
    return p[:n2, :c] - p[n2:, c:], p[:n2, c:] + p[n2:, :c]


def _fft2_filter_kernel(ar_ref, ai_ref, twr_ref, twi_ref, f2_ref, kr_ref, ki_ref, *, kb):
    f2 = f2_ref[...]
    for j in range(kb):
        fr, fi = _dft2_forward(ar_ref[0, j], ai_ref[0, j], twr_ref[j], twi_ref[j], f2)
        gr, gi = _dft2_forward(ar_ref[1, j], ai_ref[1, j], twr_ref[j], twi_ref[j], f2)
        kr_ref[j] = fr + gr
        ki_ref[j] = fi - gi


def _fft2_filter(ar, ai, twr, twi, f2, kb=8):
    _, n1, n2, c = ar.shape
    a_spec = pl.BlockSpec((2, kb, n2, c), lambda k: (0, k, 0, 0))
    t_spec = pl.BlockSpec((kb, n2, LANES), lambda k: (k, 0, 0))
    o_spec = pl.BlockSpec((kb, n2, c), lambda k: (k, 0, 0))
    return pl.pallas_call(
        functools.partial(_fft2_filter_kernel, kb=kb),
        grid=(n1 // kb,),
        in_specs=[a_spec, a_spec, t_spec, t_spec, pl.BlockSpec((2 * n2, n2), lambda k: (0, 0))],
        out_specs=[o_spec, o_spec],
        out_shape=[jax.ShapeDtypeStruct((n1, n2, c), F32)] * 2,
        compiler_params=_params(("parallel",)),
        name="fft_filter_spectrum",
    )(ar, ai, twr, twi, f2)


def _fft2_kernel(ar_ref, ai_ref, twr_ref, twi_ref, kr_ref, ki_ref, f2_ref, br_ref, bi_ref, *, kb):
    f2 = f2_ref[...]
    for j in range(kb):
        twr, twi = twr_ref[j], twi_ref[j]
        xr, xi = _dft2_forward(ar_ref[0, j], ai_ref[0, j], twr, twi, f2)
        n2, c = xr.shape
        kr, ki = kr_ref[j], ki_ref[j]
        cat = jnp.concatenate([xr * kr - xi * ki, xr * ki + xi * kr], axis=1).astype(BF16)
        p = jnp.dot(f2, cat, preferred_element_type=F32)
        b_r = p[:n2, :c] + p[n2:, c:]
        b_i = p[:n2, c:] - p[n2:, :c]
        tr, ti = _tile_lanes(twr, c), _tile_lanes(twi, c)
        br_ref[0, j] = (b_r * tr + b_i * ti).astype(BF16)
        bi_ref[0, j] = (b_i * tr - b_r * ti).astype(BF16)


def _fft2(ar, ai, twr, twi, kr, ki, f2, kb=8):
    b, n1, n2, c = ar.shape
    a_spec = pl.BlockSpec((1, kb, n2, c), lambda bi, k: (bi, k, 0, 0))
    t_spec = pl.BlockSpec((kb, n2, LANES), lambda bi, k: (k, 0, 0))
    k_spec = pl.BlockSpec((kb, n2, c), lambda bi, k: (k, 0, 0))
    return pl.pallas_call(
        functools.partial(_fft2_kernel, kb=kb),
        grid=(b, n1 // kb),
        in_specs=[a_spec, a_spec, t_spec, t_spec, k_spec, k_spec,
                  pl.BlockSpec((2 * n2, n2), lambda bi, k: (0, 0))],
        out_specs=[a_spec, a_spec],
        out_shape=[jax.ShapeDtypeStruct((b, n1, n2, c), BF16)] * 2,
        compiler_params=_params(("parallel", "parallel")),
        name="fft_stage2",
    )(ar, ai, twr, twi, kr, ki, f2)


def _fft3_kernel(br_ref, bi_ref, g_ref, u_ref, x0_ref, skip_ref, y_ref):
    g = g_ref[...]
    n1 = g.shape[1] // 2
    y = (jnp.dot(g[:, :n1], br_ref[0], preferred_element_type=F32)
         + jnp.dot(g[:, n1:], bi_ref[0], preferred_element_type=F32))
    y_ref[0] = (x0_ref[0] * (y + u_ref[0] * skip_ref[...])).astype(BF16)


def _fft3(br, bi, inv1, u3, x03, skip_t, tn=2048):
    b, n1, cols = br.shape
    n1h = n1 // 2
    big = pl.BlockSpec((1, n1, tn), lambda bi_, j: (bi_, 0, j))
    half = pl.BlockSpec((1, n1h, tn), lambda bi_, j: (bi_, 0, j))
    return pl.pallas_call(
        _fft3_kernel,
        grid=(b, cols // tn),
        in_specs=[big, big, pl.BlockSpec((n1h, 2 * n1), lambda bi_, j: (0, 0)), half, half,
                  pl.BlockSpec((1, tn), lambda bi_, j: (0, 0))],
        out_specs=half,
        out_shape=jax.ShapeDtypeStruct((b, n1h, cols), BF16),
        compiler_params=_params(("parallel", "parallel")),
        name="fft_stage3",
    )(br, bi, inv1, u3, x03, skip_t)


def _hyena_long_conv(u, x0, filters, skip, batch, seq_len):
    c = HY_WIDTH
    fwd1, inv1, f2, twr, twi, n1, n2 = _dft_tables(seq_len)
    n1h = n1 // 2
    fa_r, fa_i = _fft1(filters.reshape(2, n1h, n2 * c), fwd1)
    kr, ki = _fft2_filter(fa_r.reshape(2, n1, n2, c), fa_i.reshape(2, n1, n2, c), twr, twi, f2)
    u3 = u.reshape(batch, n1h, n2 * c)
    a_r, a_i = _fft1(u3, fwd1)
    b_r, b_i = _fft2(a_r.reshape(batch, n1, n2, c), a_i.reshape(batch, n1, n2, c),
                     twr, twi, kr, ki, f2)
    tn = 2048
    skip_t = jnp.tile(skip[None, :], (1, tn // c))
    yh = _fft3(b_r.reshape(batch, n1, n2 * c), b_i.reshape(batch, n1, n2 * c), inv1,
               u3, x0.reshape(batch, n1h, n2 * c), skip_t, tn)
    return yh.reshape(batch * seq_len, c)


def _pipelined(units, scores, softmax, values):
    s_next = scores(units[0])
    pending = None
    for i, unit in enumerate(units):
        s_cur = s_next
        if i + 1 < len(units):
            s_next = scores(units[i + 1])
        if pending is not None:
            values(*pending)
        pending = (unit,) + softmax(unit, s_cur)
    values(*pending)


def _attn_kernel(q1_ref, q2_ref, k_ref, vt_ref, lam_ref, sg_ref, o_ref,
                 m1, a1, m2, a2, *, chunk):
    ki = pl.program_id(3)
    q_refs = (q1_ref, q2_ref)
    m_refs = (m1, m2)
    a_refs = (a1, a2)
    n_chunks = k_ref.shape[0] // chunk
    units = [(c, comp) for c in range(n_chunks) for comp in (0, 1)]

    def scores(unit):
        c, comp = unit
        return jnp.dot(k_ref[c * chunk:(c + 1) * chunk, :], q_refs[comp][...],
                       preferred_element_type=F32)

    def value_dot(unit, p):
        c = unit[0]
        return jnp.dot(vt_ref[:, c * chunk:(c + 1) * chunk], p, preferred_element_type=F32)

    @pl.when(ki == 0)
    def _():
        for comp in (0, 1):
            head = jnp.dot(k_ref[0:2 * SUBLANES, :], q_refs[comp][...], preferred_element_type=F32)
            m_refs[comp][...] = jnp.max(head, axis=0, keepdims=True)
            a_refs[comp][...] = jnp.zeros(a_refs[comp].shape, F32)

    m_start = [m1[...], m2[...]]
    step_max = [None, None]
    step_acc = [None, None]

    def fast_softmax(unit, s):
        comp = unit[1]
        cm = jnp.max(s, axis=0, keepdims=True)
        step_max[comp] = cm if step_max[comp] is None else jnp.maximum(step_max[comp], cm)
        return (jnp.exp2(s - m_start[comp]).astype(BF16),)

    def fast_values(unit, p):
        comp = unit[1]
        d = value_dot(unit, p)
        step_acc[comp] = d if step_acc[comp] is None else step_acc[comp] + d

    _pipelined(units, scores, fast_softmax, fast_values)
    excess = jnp.maximum(step_max[0] - m_start[0], step_max[1] - m_start[1])
    fast_ok = jnp.max(excess) <= STABILISER_HEADROOM

    @pl.when(fast_ok)
    def _():
        for comp in (0, 1):
            m_new = jnp.maximum(m_start[comp], step_max[comp])
            a_refs[comp][...] = jnp.exp2(m_start[comp] - m_new) * (a_refs[comp][...] + step_acc[comp])
            m_refs[comp][...] = m_new

    @pl.when(jnp.logical_not(fast_ok))
    def _():
        stats = [[m1[...], a1[...]], [m2[...], a2[...]]]

        def softmax(unit, s):
            m_prev = stats[unit[1]][0]
            m_new = jnp.maximum(m_prev, jnp.max(s, axis=0, keepdims=True))
            stats[unit[1]][0] = m_new
            return jnp.exp2(s - m_new).astype(BF16), jnp.exp2(m_prev - m_new)

        def values(unit, p, alpha):
            comp = unit[1]
            stats[comp][1] = alpha * stats[comp][1] + value_dot(unit, p)

        _pipelined(units, scores, softmax, values)
        for comp in (0, 1):
            m_refs[comp][...] = stats[comp][0]
            a_refs[comp][...] = stats[comp][1]

    @pl.when(ki == pl.num_programs(3) - 1)
    def _():
        lv = lam_ref[...]
        lam = (jnp.exp(jnp.sum(lv[0:1] * lv[1:2], axis=1, keepdims=True))
               - jnp.exp(jnp.sum(lv[2:3] * lv[3:4], axis=1, keepdims=True)) + LAMBDA_INIT)
        d = ATT_V_DIM
        o = (a1[:d, :] / a1[d:d + 1, :] - lam * (a2[:d, :] / a2[d:d + 1, :]))
        inv = lax.rsqrt(jnp.mean(o * o, axis=0, keepdims=True) + SUBLN_EPS)
        g = pltpu.repeat(sg_ref[...], o.shape[1] // LANES, 1)
        o_ref[...] = (o * inv * g * (1.0 - LAMBDA_INIT)).astype(BF16)


def _attention(qt1, qt2, k, vt, lam_vecs, subln_g, batch, seq_len, tq=1024, tk=2048, chunk=512):
    tk = min(tk, seq_len)
    nq, nk = seq_len // tq, seq_len // tk
    q_spec = pl.BlockSpec((LANES, tq), lambda b, h, i, j: (h, b * nq + i))
    sg = jnp.broadcast_to(subln_g[:, None], (ATT_V_DIM, LANES))
    return pl.pallas_call(
        functools.partial(_attn_kernel, chunk=chunk),
        grid=(batch, N_ATT_HEADS, nq, nk),
        in_specs=[q_spec, q_spec,
                  pl.BlockSpec((tk, LANES), lambda b, h, i, j: (b * nk + j, h)),
                  pl.BlockSpec((VT_ROWS, tk), lambda b, h, i, j: (h, b * nk + j)),
                  pl.BlockSpec((4, ATT_HEAD_DIM), lambda b, h, i, j: (0, 0)),
                  pl.BlockSpec((ATT_V_DIM, LANES), lambda b, h, i, j: (0, 0))],
        out_specs=q_spec,
        out_shape=jax.ShapeDtypeStruct((ATT_WIDTH, batch * seq_len), BF16),
        scratch_shapes=[pltpu.VMEM((1, tq), F32), pltpu.VMEM((VT_ROWS, tq), F32)] * 2,
        compiler_params=_params(("parallel", "parallel", "parallel", "arbitrary")),
        name="diff_attention",
    )(qt1, qt2, k, vt, lam_vecs, sg)


def _split_bf16(x):
    hi = x.astype(BF16)
    return hi, (x - hi.astype(F32)).astype(BF16)


def _outproj_kernel(yh_ref, ya_ref, x_ref, wo_ref, g_ref, wq_ref, kh_ref, kl_ref,
                    x2_ref, xnt_ref, st_ref):
    wo = wo_ref[...]
    x2 = (x_ref[...] + jnp.dot(yh_ref[...], wo[:HY_WIDTH], preferred_element_type=F32)
          + lax.dot_general(ya_ref[...], wo[HY_WIDTH:], (((0,), (0,)), ((), ())),
                            preferred_element_type=F32))
    x2_ref[...] = x2
    xn = _rms(x2, g_ref[...], EPS)
    xnt_ref[...] = xn.T.astype(BF16)
    q = jnp.dot(xn.astype(BF16), wq_ref[...], preferred_element_type=F32)
    for hc in range(2 * PEER_HEADS):
        qh, ql = _split_bf16(q[:, hc * PEER_HALF:(hc + 1) * PEER_HALF])
        kh, kl = kh_ref[hc], kl_ref[hc]
        st_ref[hc * PEER_KEYS:(hc + 1) * PEER_KEYS, :] = (
            _dot_nt(kh, qh) + _dot_nt(kh, ql) + _dot_nt(kl, qh))


def _outproj(yh, ya, x, wo_bf, g2, wq_bf, keys_hi, keys_lo, tm=512):
    t = x.shape[0]
    qw = 2 * PEER_HEADS * PEER_HALF
    sw = 2 * PEER_HEADS * PEER_KEYS
    row = lambda i: (i, 0)
    fixed = lambda i: (0, 0)
    fixed3 = lambda i: (0, 0, 0)
    return pl.pallas_call(
        _outproj_kernel,
        grid=(t // tm,),
        in_specs=[pl.BlockSpec((tm, HY_WIDTH), row),
                  pl.BlockSpec((ATT_WIDTH, tm), lambda i: (0, i)),
                  pl.BlockSpec((tm, D_MODEL), row),
                  pl.BlockSpec((HY_WIDTH + ATT_WIDTH, D_MODEL), fixed),
                  pl.BlockSpec((1, D_MODEL), fixed),
                  pl.BlockSpec((D_MODEL, qw), fixed),
                  pl.BlockSpec((2 * PEER_HEADS, PEER_KEYS, PEER_HALF), fixed3),
                  pl.BlockSpec((2 * PEER_HEADS, PEER_KEYS, PEER_HALF), fixed3)],
        out_specs=[pl.BlockSpec((tm, D_MODEL), row), pl.BlockSpec((D_MODEL, tm), lambda i: (0, i)),
                   pl.BlockSpec((sw, tm), lambda i: (0, i))],
        out_shape=[jax.ShapeDtypeStruct((t, D_MODEL), F32),
                   jax.ShapeDtypeStruct((D_MODEL, t), BF16),
                   jax.ShapeDtypeStruct((sw, t), F32)],
        compiler_params=_params(("parallel",)),
        name="outproj_peer_query",
    )(yh, ya, x, wo_bf, g2, wq_bf, keys_hi, keys_lo)


def _extract_max(pieces, iotas, total, exact):
    m = pieces[0]
    for p in pieces[1:]:
        m = jnp.maximum(m, p)
    m = jnp.max(m, axis=0, keepdims=True)
    hot = [p == m for p in pieces]
    if exact:
        cand = [jnp.where(h, io, total) for h, io in zip(hot, iotas)]
        idx = cand[0]
        for c in cand[1:]:
            idx = jnp.minimum(idx, c)
        idx = jnp.min(idx, axis=0, keepdims=True)
        hot = [io == idx for io in iotas]
    return m, hot, [jnp.where(h, NEG_INF, p) for h, p in zip(hot, pieces)]


def _topk_rank(s, exact, want_rank):
    rank = jnp.full(s.shape, float(PEER_TOPK), F32) if want_rank else None
    iota = [lax.broadcasted_iota(jnp.int32, s.shape, 0)]
    vals = []
    for r in range(PEER_TOPK):
        m, hot, (s,) = _extract_max([s], iota, s.shape[0], exact)
        vals.append(m)
        if want_rank:
            rank = jnp.where(hot[0], float(r), rank)
    return vals, rank


_PAIRS = [(k0, k1) for k0 in range(PEER_TOPK) for k1 in range(PEER_TOPK // (k0 + 1))]
_PAIR_ROWS = -(-len(_PAIRS) // SUBLANES) * SUBLANES


def _rows_to_tiles(rows, width):
    sub = lax.broadcasted_iota(jnp.int32, (SUBLANES, width), 0)
    tiles = []
    for t0 in range(0, len(rows), SUBLANES):
        tile = jnp.full((SUBLANES, width), NEG_INF, F32)
        for r in range(SUBLANES):
            if rows[t0 + r] is not None:
                tile = jnp.where(sub == r, rows[t0 + r], tile)
        tiles.append(tile)
    return tiles


def _route_tables(s0, s1, exact):
    tt = s0.shape[1]
    v0, rank0 = _topk_rank(s0, exact, want_rank=exact)
    v1, rank1 = _topk_rank(s1, exact, want_rank=True)
    pad = [None] * (_PAIR_ROWS - len(_PAIRS))
    tiles = [a + b for a, b in zip(_rows_to_tiles([v0[k0] for k0, _ in _PAIRS] + pad, tt),
                                   _rows_to_tiles([v1[k1] for _, k1 in _PAIRS] + pad, tt))]
    iotas = [lax.broadcasted_iota(jnp.int32, (SUBLANES, tt), 0) + t0
             for t0 in range(0, _PAIR_ROWS, SUBLANES)]
    cmax = v0[0] + v1[0]
    z = None
    for r in range(PEER_TOPK):
        m, _, tiles = _extract_max(tiles, iotas, _PAIR_ROWS, exact)
        z = jnp.exp(m - cmax) if z is None else z + jnp.exp(m - cmax)
    cnt = jnp.zeros(s0.shape, F32)
    total = None
    for k0 in range(PEER_TOPK):
        rows = [r for r, (a, _) in enumerate(_PAIRS) if a == k0]
        taken = None
        for ti in sorted({r // SUBLANES for r in rows}):
            inside = (iotas[ti] >= rows[0]) & (iotas[ti] <= rows[-1])
            part = jnp.sum(jnp.where(inside & (tiles[ti] == NEG_INF), 1.0, 0.0),
                           axis=0, keepdims=True)
            taken = part if taken is None else taken + part
        total = taken if total is None else total + taken
        cnt = jnp.where((rank0 == float(k0)) if exact else (s0 == v0[k0]), taken, cnt)
    k = float(PEER_TOPK)
    count = lambda mask: jnp.sum(jnp.where(mask, 1.0, 0.0), axis=0, keepdims=True)
    top0 = (rank0 < k) if exact else (s0 >= v0[PEER_TOPK - 1])
    ok = (total == k) & (count(top0) == k) & (count(rank1 < k) == k)
    return cnt, jnp.exp(s0 - v0[0]) / z, rank1, jnp.exp(s1 - v1[0]), ok


def _route_kernel(st_ref, cnt_ref, w0_ref, r1_ref, e1_ref):
    def emit(exact):
        cnt, w0, rank1, e1, ok = _route_tables(st_ref[:PEER_KEYS, :], st_ref[PEER_KEYS:, :], exact)
        cnt_ref[0] = cnt
        w0_ref[0] = w0 * GELU_OUT_SCALE
        r1_ref[0] = rank1.astype(BF16)
        e1_ref[0] = e1.astype(BF16)
        return ok

    ok = emit(exact=False)
    clean = jnp.min(jnp.where(ok, 1, 0))

    @pl.when(clean == 0)
    def _():
        emit(exact=True)


def _route(st, tt=512):
    sw, t = st.shape
    o_spec = pl.BlockSpec((1, PEER_KEYS, tt), lambda i, h: (h, 0, i))
    tab = lambda dt: jax.ShapeDtypeStruct((PEER_HEADS, PEER_KEYS, t), dt)
    return pl.pallas_call(
        _route_kernel,
        grid=(t // tt, PEER_HEADS),
        in_specs=[pl.BlockSpec((2 * PEER_KEYS, tt), lambda i, h: (h, i))],
        out_specs=[o_spec] * 4,
        out_shape=[tab(F32), tab(F32), tab(BF16), tab(BF16)],
        compiler_params=_params(("parallel", "parallel")),
        name="peer_route",
    )(st)


GELU_IN_SCALE = 2.0 ** -0.5
GELU_OUT_SCALE = 0.5 * 2.0 ** 0.5


def _gelu_core(y):
    return y * (1.0 + lax.erf(y))


def _peer_kernel(xnt_ref, u_ref, vt_ref, cnt_ref, w0_ref, r1_ref, e1_ref, x2_ref, gf_ref,
                 y_ref, acc_ref, *, rows_per_step, rows_per_unit):
    e = pl.program_id(1)

    @pl.when(e == 0)
    def _():
        acc_ref[...] = jnp.zeros(acc_ref.shape, F32)

    tt = xnt_ref.shape[1]
    unit = rows_per_unit * PEER_KEYS
    n_units = rows_per_step // rows_per_unit

    def hidden(s):
        return jnp.dot(u_ref[s * unit:(s + 1) * unit, :], xnt_ref[...],
                       preferred_element_type=F32)

    def activation(s, hid):
        gates = []
        for ii in range(s * rows_per_unit, (s + 1) * rows_per_unit):
            g = None
            for h in range(PEER_HEADS):
                cnt_row = jnp.broadcast_to(cnt_ref[h, ii:ii + 1, :], (PEER_KEYS, tt)).astype(BF16)
                w_row = jnp.broadcast_to(w0_ref[h, ii:ii + 1, :], (PEER_KEYS, tt)).astype(BF16)
                term = jnp.where(r1_ref[h] < cnt_row, e1_ref[h] * w_row, jnp.zeros((), BF16))
                g = term if g is None else g + term
            gates.append(g)
        return _gelu_core(hid).astype(BF16) * jnp.concatenate(gates, axis=0)

    def mix(s, act):
        return jnp.dot(vt_ref[:, s * unit:(s + 1) * unit], act, preferred_element_type=F32)

    acc = acc_ref[...]
    h_next = hidden(0)
    pending = None
    for s in range(n_units):
        h_cur = h_next
        if s + 1 < n_units:
            h_next = hidden(s + 1)
        if pending is not None:
            acc = acc + mix(*pending)
        pending = (s, activation(s, h_cur))
    acc_ref[...] = acc + mix(*pending)

    @pl.when(e == pl.num_programs(1) - 1)
    def _():
        x3 = x2_ref[...] + acc_ref[...].T
        y_ref[...] = _rms(x3, gf_ref[...], EPS)


def _peer(xnt, u_bf, vt_bf, cnt, w0, r1, e1, x2, gf, tt=512, rows_per_step=16, rows_per_unit=2):
    t = xnt.shape[1]
    te = rows_per_step * PEER_KEYS
    tok = lambda i, e: (i, 0)
    tab_i = pl.BlockSpec((PEER_HEADS, rows_per_step, tt), lambda i, e: (0, e, i))
    tab_j = pl.BlockSpec((PEER_HEADS, PEER_KEYS, tt), lambda i, e: (0, 0, i))
    return pl.pallas_call(
        functools.partial(_peer_kernel, rows_per_step=rows_per_step,
                          rows_per_unit=rows_per_unit),
        grid=(t // tt, PEER_EXPERTS // te),
        in_specs=[pl.BlockSpec((D_MODEL, tt), lambda i, e: (0, i)),
                  pl.BlockSpec((te, D_MODEL), lambda i, e: (e, 0)),
                  pl.BlockSpec((D_MODEL, te), lambda i, e: (0, e)),
                  tab_i, tab_i, tab_j, tab_j,
                  pl.BlockSpec((tt, D_MODEL), tok),
                  pl.BlockSpec((1, D_MODEL), lambda i, e: (0, 0))],
        out_specs=pl.BlockSpec((tt, D_MODEL), tok),
        out_shape=jax.ShapeDtypeStruct((t, D_MODEL), F32),
        scratch_shapes=[pltpu.VMEM((D_MODEL, tt), F32)],
        compiler_params=_params(("parallel", "arbitrary")),
        name="peer_dense",
    )(xnt, u_bf, vt_bf, cnt, w0, r1, e1, x2, gf)


def _rope_tables(seq_len):
    f32 = np.float32
    inv = f32(ROPE_THETA) ** (-np.arange(0, ROT_DIM, 2, dtype=f32) / f32(ROT_DIM))
    ang = (np.arange(seq_len, dtype=f32)[:, None] * inv[None, :]).astype(np.float64)
    cos, sin = np.cos(ang), np.sin(ang)
    half = ROT_DIM // 2
    comp = np.zeros((3, seq_len, ATT_HEAD_DIM))
    comp[0, :, :half], comp[0, :, half:ROT_DIM], comp[0, :, ROT_DIM:] = cos, cos, 1.0
    comp[1, :, :half] = -sin
    comp[2, :, half:ROT_DIM] = sin
    both = np.concatenate([comp, comp], axis=2).astype(f32)
    return jnp.asarray(both[0]), jnp.asarray(both[1]), jnp.asarray(both[2])


def _trunk(x, p):
    batch, seq_len, _ = x.shape
    xf = x.reshape(batch * seq_len, D_MODEL)
    rope_c, rope_sa, rope_sb = _rope_tables(seq_len)
    zh, qt1, qt2, k, vt = _inproj(xf, p["norm1_g"], p["w_in"], rope_c, rope_sa, rope_sb, seq_len)
    u, x0 = _shortconv(zh, p["hy_conv_w"], p["hy_conv_b"], seq_len)
    filters = _hyena_filters(seq_len, *p["filter"])
    yh = _hyena_long_conv(u, x0, filters, p["hy_skip"], batch, seq_len)
    ya = _attention(qt1, qt2, k, vt, p["lam_vecs"], p["subln_g"], batch, seq_len)
    x2, xnt, st = _outproj(yh, ya, xf, p["w_out"], p["norm2_g"], p["wq"],
                          p["keys_hi"], p["keys_lo"])
    cnt, w0, r1, e1 = _route(st)
    y = _peer(xnt, p["peer_u"], p["peer_vt"], cnt, w0, r1, e1, x2, p["normf_g"])
    return y.reshape(batch, seq_len, D_MODEL)


def kernel(x_prompt, x_sample, norm1_g, w_in, hy_conv_w, hy_conv_b, hf_w1, hf_b1, hf_freq1,
           hf_w2, hf_b2, hf_freq2, hf_w3, hf_b3, hy_skip, lam_q1, lam_k1, lam_q2, lam_k2,
           subln_g, w_out, norm2_g, peer_wq, peer_keys, peer_u, peer_v, normf_g):
    l = 0
    keys = peer_keys[l].reshape(2 * PEER_HEADS, PEER_KEYS, PEER_HALF)
    keys_hi = keys.astype(BF16)
    p = {
        "norm1_g": norm1_g[l][None, :],
        "w_in": w_in[l].astype(BF16),
        "hy_conv_w": hy_conv_w[l],
        "hy_conv_b": hy_conv_b[l][None, :],
        "filter": (hf_w1[l], hf_b1[l], hf_freq1[l], hf_w2[l], hf_b2[l], hf_freq2[l],
                   hf_w3[l], hf_b3[l]),
        "hy_skip": hy_skip[l],
        "lam_vecs": jnp.stack([lam_q1[l], lam_k1[l], lam_q2[l], lam_k2[l]]),
        "subln_g": subln_g[l],
        "w_out": w_out[l].astype(BF16),
        "norm2_g": norm2_g[l][None, :],
        "wq": peer_wq[l].astype(BF16),
        "keys_hi": keys_hi,
        "keys_lo": (keys - keys_hi.astype(F32)).astype(BF16),
        "peer_u": (peer_u[l] * GELU_IN_SCALE).astype(BF16),
        "peer_vt": peer_v[l].astype(BF16).T,
        "normf_g": normf_g[None, :],
    }
    return (_trunk(x_prompt, p), _trunk(x_sample, p))
```

```python
import functools
import math

import jax
import jax.numpy as jnp
import numpy as np
from jax import lax
from jax.experimental import pallas as pl
from jax.experimental.pallas import tpu as pltpu

F32 = jnp.float32
BF16 = jnp.bfloat16

D_MODEL = 1024
HY_WIDTH = 512
ATT_WIDTH = 512
N_ATT_HEADS = 4
ATT_HEAD_DIM = 64
ATT_V_DIM = 2 * ATT_HEAD_DIM
ROT_DIM = ATT_HEAD_DIM // 4
ROPE_THETA = 500000.0
IN_WIDTH = 3 * HY_WIDTH + 3 * ATT_WIDTH
FILTER_BANDS = 8
FILTER_EMB = 1 + 2 * FILTER_BANDS
FILTER_HIDDEN = 64
FAST_DECAY_PCT = 0.3
SLOW_DECAY_PCT = 1.5
DECAY_TARGET = 1e-2
PEER_HEADS = 8
PEER_KEYS = 128
PEER_EXPERTS = PEER_KEYS * PEER_KEYS
PEER_TOPK = 16
PEER_HALF = 128
EPS = 1e-6
SUBLN_EPS = 1e-5
LAMBDA_INIT = 0.8 - 0.6 * math.exp(-0.3 * 0)

LANES = 128
SUBLANES = 8
VMEM_LIMIT = 56 * 1024 * 1024
FFT_N2 = 128
VT_ROWS = ATT_V_DIM + 2 * SUBLANES
STABILISER_HEADROOM = 64.0
NEG_INF = float("-inf")


def _params(semantics, vmem=VMEM_LIMIT):
    return pltpu.CompilerParams(dimension_semantics=semantics, vmem_limit_bytes=vmem)


def _rms(x, g, eps):
    return x * lax.rsqrt(jnp.mean(x * x, axis=-1, keepdims=True) + eps) * g


def _dot_nt(a, b):
    return lax.dot_general(a, b, (((1,), (1,)), ((), ())), preferred_element_type=F32)


def _inproj_kernel(x_ref, g_ref, w_ref, c_ref, sa_ref, sb_ref,
                   zh_ref, qt1_ref, qt2_ref, k_ref, vt_ref):
    h = _rms(x_ref[...], g_ref[...], EPS).astype(BF16)
    z = jnp.dot(h, w_ref[...], preferred_element_type=F32)
    hy3 = 3 * HY_WIDTH
    zh_ref[...] = z[:, :hy3]
    c, sa, sb = c_ref[...], sa_ref[...], sb_ref[...]
    scale = ATT_HEAD_DIM ** -0.5 * math.log2(math.e)

    def rope(xb):
        return (xb * c + pltpu.roll(xb, LANES - ROT_DIM // 2, 1) * sa
                + pltpu.roll(xb, ROT_DIM // 2, 1) * sb)

    first = lax.broadcasted_iota(jnp.int32, (LANES, z.shape[0]), 0) < ATT_HEAD_DIM
    for hd in range(N_ATT_HEADS):
        lo, hi = hd * LANES, (hd + 1) * LANES
        qt = (rope(z[:, hy3 + lo:hy3 + hi]) * scale).T
        qt1_ref[lo:hi, :] = jnp.where(first, qt, 0.0).astype(BF16)
        qt2_ref[lo:hi, :] = jnp.where(first, 0.0, qt).astype(BF16)
        k_ref[:, lo:hi] = rope(z[:, hy3 + ATT_WIDTH + lo:hy3 + ATT_WIDTH + hi]).astype(BF16)
        vlo = hd * VT_ROWS
        vt_ref[vlo:vlo + ATT_V_DIM, :] = (
            z[:, hy3 + 2 * ATT_WIDTH + lo:hy3 + 2 * ATT_WIDTH + hi].T.astype(BF16))
        vt_ref[vlo + ATT_V_DIM:vlo + VT_ROWS, :] = jnp.ones(
            (VT_ROWS - ATT_V_DIM, z.shape[0]), BF16)


def _inproj(x, g, w_bf, rope_c, rope_sa, rope_sb, seq_len, tm=512):
    t = x.shape[0]
    nseq = seq_len // tm
    row = lambda i: (i, 0)
    col = lambda i: (0, i)
    fixed = lambda i: (0, 0)
    pos = lambda i: (i % nseq, 0)
    hy3 = 3 * HY_WIDTH
    tr_spec = pl.BlockSpec((ATT_WIDTH, tm), col)
    tr_shape = jax.ShapeDtypeStruct((ATT_WIDTH, t), BF16)
    return pl.pallas_call(
        _inproj_kernel,
        grid=(t // tm,),
        in_specs=[pl.BlockSpec((tm, D_MODEL), row),
                  pl.BlockSpec((1, D_MODEL), fixed),
                  pl.BlockSpec((D_MODEL, IN_WIDTH), fixed),
                  pl.BlockSpec((tm, LANES), pos),
                  pl.BlockSpec((tm, LANES), pos),
                  pl.BlockSpec((tm, LANES), pos)],
        out_specs=[pl.BlockSpec((tm, hy3), row), tr_spec, tr_spec,
                   pl.BlockSpec((tm, ATT_WIDTH), row),
                   pl.BlockSpec((N_ATT_HEADS * VT_ROWS, tm), col)],
        out_shape=[jax.ShapeDtypeStruct((t, hy3), F32), tr_shape, tr_shape,
                   jax.ShapeDtypeStruct((t, ATT_WIDTH), BF16),
                   jax.ShapeDtypeStruct((N_ATT_HEADS * VT_ROWS, t), BF16)],
        compiler_params=_params(("parallel",)),
        name="inproj",
    )(x, g, w_bf, rope_c, rope_sa, rope_sb)


def _shortconv_kernel(z_ref, zp_ref, zn_ref, w_ref, b_ref, u_ref, x0_ref, *, blocks_per_seq):
    i = pl.program_id(0)
    z = z_ref[...]
    tl = z.shape[0]
    row = lax.broadcasted_iota(jnp.int32, z.shape, 0)
    at_start = (i % blocks_per_seq) == 0
    at_end = (i % blocks_per_seq) == blocks_per_seq - 1
    prev_row = jnp.where(at_start, 0.0, zp_ref[SUBLANES - 1:SUBLANES, :])
    next_row = jnp.where(at_end, 0.0, zn_ref[0:1, :])
    zm = jnp.where(row == 0, prev_row, pltpu.roll(z, 1, 0))
    zp = jnp.where(row == tl - 1, next_row, pltpu.roll(z, tl - 1, 0))
    w = w_ref[...]
    conv = zm * w[0:1] + z * w[1:2] + zp * w[2:3] + b_ref[...]
    x0_ref[...] = conv[:, :HY_WIDTH]
    u_ref[...] = conv[:, 2 * HY_WIDTH:] * conv[:, HY_WIDTH:2 * HY_WIDTH]


def _shortconv(zh, w, b, seq_len, tl=512):
    t, hy3 = zh.shape
    per8 = tl // SUBLANES
    last8 = t // SUBLANES - 1
    return pl.pallas_call(
        functools.partial(_shortconv_kernel, blocks_per_seq=seq_len // tl),
        grid=(t // tl,),
        in_specs=[pl.BlockSpec((tl, hy3), lambda i: (i, 0)),
                  pl.BlockSpec((SUBLANES, hy3), lambda i: (jnp.maximum(i * per8 - 1, 0), 0)),
                  pl.BlockSpec((SUBLANES, hy3), lambda i: (jnp.minimum((i + 1) * per8, last8), 0)),
                  pl.BlockSpec((3, hy3), lambda i: (0, 0)),
                  pl.BlockSpec((1, hy3), lambda i: (0, 0))],
        out_specs=[pl.BlockSpec((tl, HY_WIDTH), lambda i: (i, 0))] * 2,
        out_shape=[jax.ShapeDtypeStruct((t, HY_WIDTH), F32)] * 2,
        compiler_params=_params(("parallel",)),
        name="shortconv",
    )(zh, zh, zh, w, b)


def _filter_kernel(f_ref, w1_ref, b1_ref, f1_ref, w2_ref, b2_ref, f2_ref, w3_ref, b3_ref,
                   dl_ref, h_ref):
    i = pl.program_id(0)
    hp = lax.Precision.HIGHEST
    feats = f_ref[...]
    tl = feats.shape[0]
    h = jnp.sin(f1_ref[...] * (jnp.dot(feats, w1_ref[...], precision=hp,
                                       preferred_element_type=F32) + b1_ref[...]))
    h = jnp.sin(f2_ref[...] * (jnp.dot(h, w2_ref[...], precision=hp,
                                       preferred_element_type=F32) + b2_ref[...]))
    h3 = jnp.dot(h, w3_ref[...], precision=hp, preferred_element_type=F32) + b3_ref[...]
    decay = jnp.exp(-feats[:, 0:1] * dl_ref[...])
    h_ref[0] = h3[:, :HY_WIDTH] * decay
    row = lax.broadcasted_iota(jnp.int32, (tl, HY_WIDTH), 0) + i * tl
    h_ref[1] = jnp.where(row == 0, 0.0, h3[:, HY_WIDTH:] * decay)


def _pad2(a, rows, cols):
    return jnp.pad(a, ((0, rows - a.shape[0]), (0, cols - a.shape[1])))


def _hyena_filters(seq_len, w1, b1, f1, w2, b2, f2, w3, b3, tl=512):
    f32 = np.float32
    t = np.linspace(0.0, 1.0, seq_len).astype(f32)[:, None]
    w = f32(2.0 * math.pi) * np.arange(seq_len, dtype=f32) / f32(seq_len)
    bands = np.linspace(1e-4, FILTER_BANDS - 1, FILTER_BANDS).astype(f32)
    ang = (w[:, None] * bands[None, :]).astype(np.float64)
    feats_np = np.zeros((seq_len, LANES), f32)
    feats_np[:, :FILTER_EMB] = np.concatenate([t, np.cos(ang), -np.sin(ang)], axis=-1)
    feats = jnp.asarray(feats_np)
    max_decay = math.log(DECAY_TARGET) / FAST_DECAY_PCT
    min_decay = math.log(DECAY_TARGET) / SLOW_DECAY_PCT
    deltas = jnp.abs(jnp.linspace(min_decay, max_decay, HY_WIDTH, dtype=F32))[None, :]
    w1p = _pad2(w1, LANES, LANES)
    w2p = _pad2(w2, LANES, LANES)
    w3p = _pad2(w3, LANES, 2 * HY_WIDTH)
    pad1 = lambda v: _pad2(v[None, :], 1, LANES)
    fixed = lambda i: (0, 0)
    vec = pl.BlockSpec((1, LANES), fixed)
    mat = pl.BlockSpec((LANES, LANES), fixed)
    return pl.pallas_call(
        _filter_kernel,
        grid=(seq_len // tl,),
        in_specs=[pl.BlockSpec((tl, LANES), lambda i: (i, 0)), mat, vec, vec, mat, vec, vec,
                  pl.BlockSpec((LANES, 2 * HY_WIDTH), fixed),
                  pl.BlockSpec((1, 2 * HY_WIDTH), fixed),
                  pl.BlockSpec((1, HY_WIDTH), fixed)],
        out_specs=pl.BlockSpec((2, tl, HY_WIDTH), lambda i: (0, i, 0)),
        out_shape=jax.ShapeDtypeStruct((2, seq_len, HY_WIDTH), F32),
        compiler_params=_params(("parallel",)),
        name="hyena_filter",
    )(feats, w1p, pad1(b1), pad1(f1), w2p, pad1(b2), pad1(f2), w3p, b3[None, :], deltas)


def _dft_tables(seq_len):
    n = 2 * seq_len
    n2 = FFT_N2
    n1 = n // n2
    n1h = n1 // 2
    a1 = (2.0 * np.pi / n1) * np.outer(np.arange(n1), np.arange(n1))
    f1r, f1i = np.cos(a1), -np.sin(a1)
    fwd1 = np.concatenate([f1r[:, :n1h], f1i[:, :n1h]], axis=0)
    inv1 = np.concatenate([f1r[:n1h, :], f1i[:n1h, :]], axis=1) / n
    a2 = (2.0 * np.pi / n2) * np.outer(np.arange(n2), np.arange(n2))
    f2 = np.concatenate([np.cos(a2), -np.sin(a2)], axis=0)
    at = (2.0 * np.pi / n) * np.outer(np.arange(n1), np.arange(n2))
    twr = np.broadcast_to(np.cos(at)[:, :, None], (n1, n2, LANES))
    twi = np.broadcast_to(-np.sin(at)[:, :, None], (n1, n2, LANES))
    return (jnp.asarray(fwd1, BF16), jnp.asarray(inv1, BF16), jnp.asarray(f2, BF16),
            jnp.asarray(twr, F32), jnp.asarray(twi, F32), n1, n2)


def _fft1_kernel(u_ref, f_ref, ar_ref, ai_ref):
    r = jnp.dot(f_ref[...], u_ref[0].astype(BF16), preferred_element_type=F32)
    n1 = r.shape[0] // 2
    ar_ref[0] = r[:n1].astype(BF16)
    ai_ref[0] = r[n1:].astype(BF16)


def _fft1(u3, fwd1, tn=2048):
    b, n1h, cols = u3.shape
    n1 = 2 * n1h
    return pl.pallas_call(
        _fft1_kernel,
        grid=(b, cols // tn),
        in_specs=[pl.BlockSpec((1, n1h, tn), lambda bi, j: (bi, 0, j)),
                  pl.BlockSpec((2 * n1, n1h), lambda bi, j: (0, 0))],
        out_specs=[pl.BlockSpec((1, n1, tn), lambda bi, j: (bi, 0, j))] * 2,
        out_shape=[jax.ShapeDtypeStruct((b, n1, cols), BF16)] * 2,
        compiler_params=_params(("parallel", "parallel")),
        name="fft_stage1",
    )(u3, fwd1)


def _tile_lanes(x, width):
    return jnp.concatenate([x] * (width // x.shape[1]), axis=1)


def _dft2_forward(ar, ai, twr, twi, f2):
    n2, c = ar.shape
    ar, ai = ar.astype(F32), ai.astype(F32)
    tr, ti = _tile_lanes(twr, c), _tile_lanes(twi, c)
    cat = jnp.concatenate([ar * tr - ai * ti, ar * ti + ai * tr], axis=1).astype(BF16)
    p = jnp.dot(f2, cat, preferred_element_type=F32)
    return p[:n2, :c] - p[n2:, c:], p[:n2, c:] + p[n2:, :c]


def _fft2_filter_kernel(ar_ref, ai_ref, twr_ref, twi_ref, f2_ref, kr_ref, ki_ref, *, kb):
    f2 = f2_ref[...]
    for j in range(kb):
        fr, fi = _dft2_forward(ar_ref[0, j], ai_ref[0, j], twr_ref[j], twi_ref[j], f2)
        gr, gi = _dft2_forward(ar_ref[1, j], ai_ref[1, j], twr_ref[j], twi_ref[j], f2)
        kr_ref[j] = (fr + gr).astype(BF16)
        ki_ref[j] = (fi - gi).astype(BF16)


def _fft2_filter(ar, ai, twr, twi, f2, kb=8):
    _, n1, n2, c = ar.shape
    a_spec = pl.BlockSpec((2, kb, n2, c), lambda k: (0, k, 0, 0))
    t_spec = pl.BlockSpec((kb, n2, LANES), lambda k: (k, 0, 0))
    o_spec = pl.BlockSpec((kb, n2, c), lambda k: (k, 0, 0))
    return pl.pallas_call(
        functools.partial(_fft2_filter_kernel, kb=kb),
        grid=(n1 // kb,),
        in_specs=[a_spec, a_spec, t_spec, t_spec, pl.BlockSpec((2 * n2, n2), lambda k: (0, 0))],
        out_specs=[o_spec, o_spec],
        out_shape=[jax.ShapeDtypeStruct((n1, n2, c), BF16)] * 2,
        compiler_params=_params(("parallel",)),
        name="fft_filter_spectrum",
    )(ar, ai, twr, twi, f2)


def _fft2_kernel(ar_ref, ai_ref, twr_ref, twi_ref, kr_ref, ki_ref, f2_ref, br_ref, bi_ref, *, kb):
    f2 = f2_ref[...]
    for j in range(kb):
        twr, twi = twr_ref[j], twi_ref[j]
        xr, xi = _dft2_forward(ar_ref[0, j], ai_ref[0, j], twr, twi, f2)
        n2, c = xr.shape
        kr, ki = kr_ref[j].astype(F32), ki_ref[j].astype(F32)
        cat = jnp.concatenate([xr * kr - xi * ki, xr * ki + xi * kr], axis=1).astype(BF16)
        p = jnp.dot(f2, cat, preferred_element_type=F32)
        b_r = p[:n2, :c] + p[n2:, c:]
        b_i = p[:n2, c:] - p[n2:, :c]
        tr, ti = _tile_lanes(twr, c), _tile_lanes(twi, c)
        br_ref[0, j] = (b_r * tr + b_i * ti).astype(BF16)
        bi_ref[0, j] = (b_i * tr - b_r * ti).astype(BF16)


def _fft2(ar, ai, twr, twi, kr, ki, f2, kb=8):
    b, n1, n2, c = ar.shape
    a_spec = pl.BlockSpec((1, kb, n2, c), lambda bi, k: (bi, k, 0, 0))
    t_spec = pl.BlockSpec((kb, n2, LANES), lambda bi, k: (k, 0, 0))
    k_spec = pl.BlockSpec((kb, n2, c), lambda bi, k: (k, 0, 0))
    return pl.pallas_call(
        functools.partial(_fft2_kernel, kb=kb),
        grid=(b, n1 // kb),
        in_specs=[a_spec, a_spec, t_spec, t_spec, k_spec, k_spec,
                  pl.BlockSpec((2 * n2, n2), lambda bi, k: (0, 0))],
        out_specs=[a_spec, a_spec],
        out_shape=[jax.ShapeDtypeStruct((b, n1, n2, c), BF16)] * 2,
        compiler_params=_params(("parallel", "parallel")),
        name="fft_stage2",
    )(ar, ai, twr, twi, kr, ki, f2)


def _fft3_kernel(br_ref, bi_ref, g_ref, u_ref, x0_ref, skip_ref, y_ref):
    g = g_ref[...]
    n1 = g.shape[1] // 2
    y = (jnp.dot(g[:, :n1], br_ref[0], preferred_element_type=F32)
         + jnp.dot(g[:, n1:], bi_ref[0], preferred_element_type=F32))
    y_ref[0] = (x0_ref[0] * (y + u_ref[0] * skip_ref[...])).astype(BF16)


def _fft3(br, bi, inv1, u3, x03, skip_t, tn=2048):
    b, n1, cols = br.shape
    n1h = n1 // 2
    big = pl.BlockSpec((1, n1, tn), lambda bi_, j: (bi_, 0, j))
    half = pl.BlockSpec((1, n1h, tn), lambda bi_, j: (bi_, 0, j))
    return pl.pallas_call(
        _fft3_kernel,
        grid=(b, cols // tn),
        in_specs=[big, big, pl.BlockSpec((n1h, 2 * n1), lambda bi_, j: (0, 0)), half, half,
                  pl.BlockSpec((1, tn), lambda bi_, j: (0, 0))],
        out_specs=half,
        out_shape=jax.ShapeDtypeStruct((b, n1h, cols), BF16),
        compiler_params=_params(("parallel", "parallel")),
        name="fft_stage3",
    )(br, bi, inv1, u3, x03, skip_t)


def _hyena_long_conv(u, x0, filters, skip, batch, seq_len):
    c = HY_WIDTH
    fwd1, inv1, f2, twr, twi, n1, n2 = _dft_tables(seq_len)
    n1h = n1 // 2
    fa_r, fa_i = _fft1(filters.reshape(2, n1h, n2 * c), fwd1)
    kr, ki = _fft2_filter(fa_r.reshape(2, n1, n2, c), fa_i.reshape(2, n1, n2, c), twr, twi, f2)
    u3 = u.reshape(batch, n1h, n2 * c)
    a_r, a_i = _fft1(u3, fwd1)
    b_r, b_i = _fft2(a_r.reshape(batch, n1, n2, c), a_i.reshape(batch, n1, n2, c),
                     twr, twi, kr, ki, f2)
    tn = 2048
    skip_t = jnp.tile(skip[None, :], (1, tn // c))
    yh = _fft3(b_r.reshape(batch, n1, n2 * c), b_i.reshape(batch, n1, n2 * c), inv1,
               u3, x0.reshape(batch, n1h, n2 * c), skip_t, tn)
    return yh.reshape(batch * seq_len, c)


def _pipelined(units, scores, softmax, values):
    s_next = scores(units[0])
    pending = None
    for i, unit in enumerate(units):
        s_cur = s_next
        if i + 1 < len(units):
            s_next = scores(units[i + 1])
        if pending is not None:
            values(*pending)
        pending = (unit,) + softmax(unit, s_cur)
    values(*pending)


def _attn_kernel(q1_ref, q2_ref, k_ref, vt_ref, lam_ref, sg_ref, o_ref,
                 m1, a1, m2, a2, *, chunk):
    ki = pl.program_id(3)
    q_refs = (q1_ref, q2_ref)
    m_refs = (m1, m2)
    a_refs = (a1, a2)
    n_chunks = k_ref.shape[0] // chunk
    units = [(c, comp) for c in range(n_chunks) for comp in (0, 1)]

    def scores(unit):
        c, comp = unit
        return jnp.dot(k_ref[c * chunk:(c + 1) * chunk, :], q_refs[comp][...],
                       preferred_element_type=F32)

    def value_dot(unit, p):
        c = unit[0]
        return jnp.dot(vt_ref[:, c * chunk:(c + 1) * chunk], p, preferred_element_type=F32)

    @pl.when(ki == 0)
    def _():
        for comp in (0, 1):
            head = jnp.dot(k_ref[0:2 * SUBLANES, :], q_refs[comp][...], preferred_element_type=F32)
            m_refs[comp][...] = jnp.max(head, axis=0, keepdims=True)
            a_refs[comp][...] = jnp.zeros(a_refs[comp].shape, F32)

    m_start = [m1[...], m2[...]]
    step_max = [None, None]
    step_acc = [None, None]

    def fast_softmax(unit, s):
        comp = unit[1]
        cm = jnp.max(s, axis=0, keepdims=True)
        step_max[comp] = cm if step_max[comp] is None else jnp.maximum(step_max[comp], cm)
        return (jnp.exp2(s - m_start[comp]).astype(BF16),)

    def fast_values(unit, p):
        comp = unit[1]
        d = value_dot(unit, p)
        step_acc[comp] = d if step_acc[comp] is None else step_acc[comp] + d

    _pipelined(units, scores, fast_softmax, fast_values)
    excess = jnp.maximum(step_max[0] - m_start[0], step_max[1] - m_start[1])
    fast_ok = jnp.max(excess) <= STABILISER_HEADROOM

    @pl.when(fast_ok)
    def _():
        for comp in (0, 1):
            m_new = jnp.maximum(m_start[comp], step_max[comp])
            a_refs[comp][...] = jnp.exp2(m_start[comp] - m_new) * (a_refs[comp][...] + step_acc[comp])
            m_refs[comp][...] = m_new

    @pl.when(jnp.logical_not(fast_ok))
    def _():
        stats = [[m1[...], a1[...]], [m2[...], a2[...]]]

        def softmax(unit, s):
            m_prev = stats[unit[1]][0]
            m_new = jnp.maximum(m_prev, jnp.max(s, axis=0, keepdims=True))
            stats[unit[1]][0] = m_new
            return jnp.exp2(s - m_new).astype(BF16), jnp.exp2(m_prev - m_new)

        def values(unit, p, alpha):
            comp = unit[1]
            stats[comp][1] = alpha * stats[comp][1] + value_dot(unit, p)

        _pipelined(units, scores, softmax, values)
        for comp in (0, 1):
            m_refs[comp][...] = stats[comp][0]
            a_refs[comp][...] = stats[comp][1]

    @pl.when(ki == pl.num_programs(3) - 1)
    def _():
        lv = lam_ref[...]
        lam = (jnp.exp(jnp.sum(lv[0:1] * lv[1:2], axis=1, keepdims=True))
               - jnp.exp(jnp.sum(lv[2:3] * lv[3:4], axis=1, keepdims=True)) + LAMBDA_INIT)
        d = ATT_V_DIM
        o = (a1[:d, :] / a1[d:d + 1, :] - lam * (a2[:d, :] / a2[d:d + 1, :]))
        inv = lax.rsqrt(jnp.mean(o * o, axis=0, keepdims=True) + SUBLN_EPS)
        g = pltpu.repeat(sg_ref[...], o.shape[1] // LANES, 1)
        o_ref[...] = (o * inv * g * (1.0 - LAMBDA_INIT)).astype(BF16)


def _attention(qt1, qt2, k, vt, lam_vecs, subln_g, batch, seq_len, tq=1024, tk=4096, chunk=512):
    tk = min(tk, seq_len)
    nq, nk = seq_len // tq, seq_len // tk
    q_spec = pl.BlockSpec((LANES, tq), lambda b, h, i, j: (h, b * nq + i))
    sg = jnp.broadcast_to(subln_g[:, None], (ATT_V_DIM, LANES))
    return pl.pallas_call(
        functools.partial(_attn_kernel, chunk=chunk),
        grid=(batch, N_ATT_HEADS, nq, nk),
        in_specs=[q_spec, q_spec,
                  pl.BlockSpec((tk, LANES), lambda b, h, i, j: (b * nk + j, h)),
                  pl.BlockSpec((VT_ROWS, tk), lambda b, h, i, j: (h, b * nk + j)),
                  pl.BlockSpec((4, ATT_HEAD_DIM), lambda b, h, i, j: (0, 0)),
                  pl.BlockSpec((ATT_V_DIM, LANES), lambda b, h, i, j: (0, 0))],
        out_specs=q_spec,
        out_shape=jax.ShapeDtypeStruct((ATT_WIDTH, batch * seq_len), BF16),
        scratch_shapes=[pltpu.VMEM((1, tq), F32), pltpu.VMEM((VT_ROWS, tq), F32)] * 2,
        compiler_params=_params(("parallel", "parallel", "parallel", "arbitrary")),
        name="diff_attention",
    )(qt1, qt2, k, vt, lam_vecs, sg)


def _split_bf16(x):
    hi = x.astype(BF16)
    return hi, (x - hi.astype(F32)).astype(BF16)


def _outproj_kernel(yh_ref, ya_ref, x_ref, wo_ref, g_ref, wqt_ref, kh_ref, kl_ref,
                    x2_ref, xnt_ref, st_ref):
    wo = wo_ref[...]
    x2 = (x_ref[...] + jnp.dot(yh_ref[...], wo[:HY_WIDTH], preferred_element_type=F32)
          + lax.dot_general(ya_ref[...], wo[HY_WIDTH:], (((0,), (0,)), ((), ())),
                            preferred_element_type=F32))
    x2_ref[...] = x2
    xn = _rms(x2, g_ref[...], EPS)
    xnt = xn.T.astype(BF16)
    xnt_ref[...] = xnt
    qt = jnp.dot(wqt_ref[...], xnt, preferred_element_type=F32)
    for hc in range(2 * PEER_HEADS):
        qh, ql = _split_bf16(qt[hc * PEER_HALF:(hc + 1) * PEER_HALF, :])
        kh, kl = kh_ref[hc], kl_ref[hc]
        st_ref[hc * PEER_KEYS:(hc + 1) * PEER_KEYS, :] = (
            jnp.dot(kh, qh, preferred_element_type=F32)
            + jnp.dot(kh, ql, preferred_element_type=F32)
            + jnp.dot(kl, qh, preferred_element_type=F32))


def _outproj(yh, ya, x, wo_bf, g2, wqt_bf, keys_hi, keys_lo, tm=512):
    t = x.shape[0]
    qw = 2 * PEER_HEADS * PEER_HALF
    sw = 2 * PEER_HEADS * PEER_KEYS
    row = lambda i: (i, 0)
    fixed = lambda i: (0, 0)
    fixed3 = lambda i: (0, 0, 0)
    return pl.pallas_call(
        _outproj_kernel,
        grid=(t // tm,),
        in_specs=[pl.BlockSpec((tm, HY_WIDTH), row),
                  pl.BlockSpec((ATT_WIDTH, tm), lambda i: (0, i)),
                  pl.BlockSpec((tm, D_MODEL), row),
                  pl.BlockSpec((HY_WIDTH + ATT_WIDTH, D_MODEL), fixed),
                  pl.BlockSpec((1, D_MODEL), fixed),
                  pl.BlockSpec((qw, D_MODEL), fixed),
                  pl.BlockSpec((2 * PEER_HEADS, PEER_KEYS, PEER_HALF), fixed3),
                  pl.BlockSpec((2 * PEER_HEADS, PEER_KEYS, PEER_HALF), fixed3)],
        out_specs=[pl.BlockSpec((tm, D_MODEL), row), pl.BlockSpec((D_MODEL, tm), lambda i: (0, i)),
                   pl.BlockSpec((sw, tm), lambda i: (0, i))],
        out_shape=[jax.ShapeDtypeStruct((t, D_MODEL), F32),
                   jax.ShapeDtypeStruct((D_MODEL, t), BF16),
                   jax.ShapeDtypeStruct((sw, t), F32)],
        compiler_params=_params(("parallel",)),
        name="outproj_peer_query",
    )(yh, ya, x, wo_bf, g2, wqt_bf, keys_hi, keys_lo)


def _extract_max(pieces, iotas, total, exact):
    m = pieces[0]
    for p in pieces[1:]:
        m = jnp.maximum(m, p)
    m = jnp.max(m, axis=0, keepdims=True)
    hot = [p == m for p in pieces]
    if exact:
        cand = [jnp.where(h, io, total) for h, io in zip(hot, iotas)]
        idx = cand[0]
        for c in cand[1:]:
            idx = jnp.minimum(idx, c)
        idx = jnp.min(idx, axis=0, keepdims=True)
        hot = [io == idx for io in iotas]
    return m, hot, [jnp.where(h, NEG_INF, p) for h, p in zip(hot, pieces)]


def _topk_rank(s, exact, want_rank):
    rank = jnp.full(s.shape, float(PEER_TOPK), F32) if want_rank else None
    iota = [lax.broadcasted_iota(jnp.int32, s.shape, 0)]
    vals = []
    for r in range(PEER_TOPK):
        m, hot, (s,) = _extract_max([s], iota, s.shape[0], exact)
        vals.append(m)
        if want_rank:
            rank = jnp.where(hot[0], float(r), rank)
    return vals, rank


_PAIRS = [(k0, k1) for k0 in range(PEER_TOPK) for k1 in range(PEER_TOPK // (k0 + 1))]
_PAIR_ROWS = -(-len(_PAIRS) // SUBLANES) * SUBLANES


def _rows_to_tiles(rows, width):
    sub = lax.broadcasted_iota(jnp.int32, (SUBLANES, width), 0)
    tiles = []
    for t0 in range(0, len(rows), SUBLANES):
        tile = jnp.full((SUBLANES, width), NEG_INF, F32)
        for r in range(SUBLANES):
            if rows[t0 + r] is not None:
                tile = jnp.where(sub == r, rows[t0 + r], tile)
        tiles.append(tile)
    return tiles


def _route_tables(s0, s1, exact):
    tt = s0.shape[1]
    v0, rank0 = _topk_rank(s0, exact, want_rank=exact)
    v1, rank1 = _topk_rank(s1, exact, want_rank=True)
    pad = [None] * (_PAIR_ROWS - len(_PAIRS))
    tiles = [a + b for a, b in zip(_rows_to_tiles([v0[k0] for k0, _ in _PAIRS] + pad, tt),
                                   _rows_to_tiles([v1[k1] for _, k1 in _PAIRS] + pad, tt))]
    iotas = [lax.broadcasted_iota(jnp.int32, (SUBLANES, tt), 0) + t0
             for t0 in range(0, _PAIR_ROWS, SUBLANES)]
    cmax = v0[0] + v1[0]
    z = None
    for r in range(PEER_TOPK):
        m, _, tiles = _extract_max(tiles, iotas, _PAIR_ROWS, exact)
        z = jnp.exp(m - cmax) if z is None else z + jnp.exp(m - cmax)
    cnt = jnp.zeros(s0.shape, F32)
    total = None
    for k0 in range(PEER_TOPK):
        rows = [r for r, (a, _) in enumerate(_PAIRS) if a == k0]
        taken = None
        for ti in sorted({r // SUBLANES for r in rows}):
            inside = (iotas[ti] >= rows[0]) & (iotas[ti] <= rows[-1])
            part = jnp.sum(jnp.where(inside & (tiles[ti] == NEG_INF), 1.0, 0.0),
                           axis=0, keepdims=True)
            taken = part if taken is None else taken + part
        total = taken if total is None else total + taken
        cnt = jnp.where((rank0 == float(k0)) if exact else (s0 == v0[k0]), taken, cnt)
    k = float(PEER_TOPK)
    count = lambda mask: jnp.sum(jnp.where(mask, 1.0, 0.0), axis=0, keepdims=True)
    top0 = (rank0 < k) if exact else (s0 >= v0[PEER_TOPK - 1])
    ok = (total == k) & (count(top0) == k) & (count(rank1 < k) == k)
    return cnt, jnp.exp(s0 - v0[0]) / z, rank1, jnp.exp(s1 - v1[0]), ok


def _route_kernel(st_ref, cnt_ref, w0_ref, r1_ref, e1_ref):
    def emit(exact):
        cnt, w0, rank1, e1, ok = _route_tables(st_ref[:PEER_KEYS, :], st_ref[PEER_KEYS:, :], exact)
        cnt_ref[0] = cnt
        w0_ref[0] = w0 * GELU_OUT_SCALE
        r1_ref[0] = rank1.astype(BF16)
        e1_ref[0] = e1.astype(BF16)
        return ok

    ok = emit(exact=False)
    clean = jnp.min(jnp.where(ok, 1, 0))

    @pl.when(clean == 0)
    def _():
        emit(exact=True)


def _route(st, tt=512):
    sw, t = st.shape
    o_spec = pl.BlockSpec((1, PEER_KEYS, tt), lambda i, h: (h, 0, i))
    tab = lambda dt: jax.ShapeDtypeStruct((PEER_HEADS, PEER_KEYS, t), dt)
    return pl.pallas_call(
        _route_kernel,
        grid=(t // tt, PEER_HEADS),
        in_specs=[pl.BlockSpec((2 * PEER_KEYS, tt), lambda i, h: (h, i))],
        out_specs=[o_spec] * 4,
        out_shape=[tab(F32), tab(F32), tab(BF16), tab(BF16)],
        compiler_params=_params(("parallel", "parallel")),
        name="peer_route",
    )(st)


GELU_IN_SCALE = 2.0 ** -0.5
GELU_OUT_SCALE = 0.5 * 2.0 ** 0.5


def _gelu_core(y):
    return y * (1.0 + lax.erf(y))


def _peer_kernel(xnt_ref, u_ref, vt_ref, cnt_ref, w0_ref, r1_ref, e1_ref, x2_ref, gf_ref,
                 y_ref, acc_ref, *, rows_per_step, rows_per_unit):
    e = pl.program_id(1)

    @pl.when(e == 0)
    def _():
        acc_ref[...] = jnp.zeros(acc_ref.shape, F32)

    tt = xnt_ref.shape[1]
    unit = rows_per_unit * PEER_KEYS
    n_units = rows_per_step // rows_per_unit

    def hidden(s):
        return jnp.dot(u_ref[s * unit:(s + 1) * unit, :], xnt_ref[...],
                       preferred_element_type=F32)

    def activation(s, hid):
        gates = []
        for ii in range(s * rows_per_unit, (s + 1) * rows_per_unit):
            g = None
            for h in range(PEER_HEADS):
                cnt_row = jnp.broadcast_to(cnt_ref[h, ii:ii + 1, :], (PEER_KEYS, tt)).astype(BF16)
                w_row = jnp.broadcast_to(w0_ref[h, ii:ii + 1, :], (PEER_KEYS, tt)).astype(BF16)
                term = jnp.where(r1_ref[h] < cnt_row, e1_ref[h] * w_row, jnp.zeros((), BF16))
                g = term if g is None else g + term
            gates.append(g)
        return _gelu_core(hid).astype(BF16) * jnp.concatenate(gates, axis=0)

    def mix(s, act):
        return jnp.dot(vt_ref[:, s * unit:(s + 1) * unit], act, preferred_element_type=F32)

    acc = acc_ref[...]
    h_next = hidden(0)
    pending = None
    for s in range(n_units):
        h_cur = h_next
        if s + 1 < n_units:
            h_next = hidden(s + 1)
        if pending is not None:
            acc = acc + mix(*pending)
        pending = (s, activation(s, h_cur))
    acc_ref[...] = acc + mix(*pending)

    @pl.when(e == pl.num_programs(1) - 1)
    def _():
        x3 = x2_ref[...] + acc_ref[...].T
        y_ref[...] = _rms(x3, gf_ref[...], EPS)


def _peer(xnt, u_bf, vt_bf, cnt, w0, r1, e1, x2, gf, tt=512, rows_per_step=16, rows_per_unit=2):
    t = xnt.shape[1]
    te = rows_per_step * PEER_KEYS
    tok = lambda i, e: (i, 0)
    tab_i = pl.BlockSpec((PEER_HEADS, rows_per_step, tt), lambda i, e: (0, e, i))
    tab_j = pl.BlockSpec((PEER_HEADS, PEER_KEYS, tt), lambda i, e: (0, 0, i))
    return pl.pallas_call(
        functools.partial(_peer_kernel, rows_per_step=rows_per_step,
                          rows_per_unit=rows_per_unit),
        grid=(t // tt, PEER_EXPERTS // te),
        in_specs=[pl.BlockSpec((D_MODEL, tt), lambda i, e: (0, i)),
                  pl.BlockSpec((te, D_MODEL), lambda i, e: (e, 0)),
                  pl.BlockSpec((D_MODEL, te), lambda i, e: (0, e)),
                  tab_i, tab_i, tab_j, tab_j,
                  pl.BlockSpec((tt, D_MODEL), tok),
                  pl.BlockSpec((1, D_MODEL), lambda i, e: (0, 0))],
        out_specs=pl.BlockSpec((tt, D_MODEL), tok),
        out_shape=jax.ShapeDtypeStruct((t, D_MODEL), F32),
        scratch_shapes=[pltpu.VMEM((D_MODEL, tt), F32)],
        compiler_params=_params(("parallel", "arbitrary")),
        name="peer_dense",
    )(xnt, u_bf, vt_bf, cnt, w0, r1, e1, x2, gf)


def _rope_tables(seq_len):
    f32 = np.float32
    inv = f32(ROPE_THETA) ** (-np.arange(0, ROT_DIM, 2, dtype=f32) / f32(ROT_DIM))
    ang = (np.arange(seq_len, dtype=f32)[:, None] * inv[None, :]).astype(np.float64)
    cos, sin = np.cos(ang), np.sin(ang)
    half = ROT_DIM // 2
    comp = np.zeros((3, seq_len, ATT_HEAD_DIM))
    comp[0, :, :half], comp[0, :, half:ROT_DIM], comp[0, :, ROT_DIM:] = cos, cos, 1.0
    comp[1, :, :half] = -sin
    comp[2, :, half:ROT_DIM] = sin
    both = np.concatenate([comp, comp], axis=2).astype(f32)
    return jnp.asarray(both[0]), jnp.asarray(both[1]), jnp.asarray(both[2])


def _trunk(x, p):
    batch, seq_len, _ = x.shape
    xf = x.reshape(batch * seq_len, D_MODEL)
    rope_c, rope_sa, rope_sb = _rope_tables(seq_len)
    zh, qt1, qt2, k, vt = _inproj(xf, p["norm1_g"], p["w_in"], rope_c, rope_sa, rope_sb, seq_len)
    u, x0 = _shortconv(zh, p["hy_conv_w"], p["hy_conv_b"], seq_len)
    filters = _hyena_filters(seq_len, *p["filter"])
    yh = _hyena_long_conv(u, x0, filters, p["hy_skip"], batch, seq_len)
    ya = _attention(qt1, qt2, k, vt, p["lam_vecs"], p["subln_g"], batch, seq_len)
    x2, xnt, st = _outproj(yh, ya, xf, p["w_out"], p["norm2_g"], p["wqt"],
                          p["keys_hi"], p["keys_lo"])
    cnt, w0, r1, e1 = _route(st)
    y = _peer(xnt, p["peer_u"], p["peer_vt"], cnt, w0, r1, e1, x2, p["normf_g"])
    return y.reshape(batch, seq_len, D_MODEL)


def kernel(x_prompt, x_sample, norm1_g, w_in, hy_conv_w, hy_conv_b, hf_w1, hf_b1, hf_freq1,
           hf_w2, hf_b2, hf_freq2, hf_w3, hf_b3, hy_skip, lam_q1, lam_k1, lam_q2, lam_k2,
           subln_g, w_out, norm2_g, peer_wq, peer_keys, peer_u, peer_v, normf_g):
    l = 0
    keys = peer_keys[l].reshape(2 * PEER_HEADS, PEER_KEYS, PEER_HALF)
    keys_hi = keys.astype(BF16)
    p = {
        "norm1_g": norm1_g[l][None, :],
        "w_in": w_in[l].astype(BF16),
        "hy_conv_w": hy_conv_w[l],
        "hy_conv_b": hy_conv_b[l][None, :],
        "filter": (hf_w1[l], hf_b1[l], hf_freq1[l], hf_w2[l], hf_b2[l], hf_freq2[l],
                   hf_w3[l], hf_b3[l]),
        "hy_skip": hy_skip[l],
        "lam_vecs": jnp.stack([lam_q1[l], lam_k1[l], lam_q2[l], lam_k2[l]]),
        "subln_g": subln_g[l],
        "w_out": w_out[l].astype(BF16),
        "norm2_g": norm2_g[l][None, :],
        "wqt": peer_wq[l].astype(BF16).T,
        "keys_hi": keys_hi,
        "keys_lo": (keys - keys_hi.astype(F32)).astype(BF16),
        "peer_u": (peer_u[l] * GELU_IN_SCALE).astype(BF16),
        "peer_vt": peer_v[l].astype(BF16).T,
        "normf_g": normf_g[None, :],
    }
    return (_trunk(x_prompt, p), _trunk(x_sample, p))
```

```python
import functools
import math

import jax
import jax.numpy as jnp
import numpy as np
from jax import lax
from jax.experimental import pallas as pl
from jax.experimental.pallas import tpu as pltpu

F32 = jnp.float32
BF16 = jnp.bfloat16

D_MODEL = 1024
HY_WIDTH = 512
ATT_WIDTH = 512
N_ATT_HEADS = 4
ATT_HEAD_DIM = 64
ATT_V_DIM = 2 * ATT_HEAD_DIM
ROT_DIM = ATT_HEAD_DIM // 4
ROPE_THETA = 500000.0
IN_WIDTH = 3 * HY_WIDTH + 3 * ATT_WIDTH
FILTER_BANDS = 8
FILTER_EMB = 1 + 2 * FILTER_BANDS
FILTER_HIDDEN = 64
FAST_DECAY_PCT = 0.3
SLOW_DECAY_PCT = 1.5
DECAY_TARGET = 1e-2
PEER_HEADS = 8
PEER_KEYS = 128
PEER_EXPERTS = PEER_KEYS * PEER_KEYS
PEER_TOPK = 16
PEER_HALF = 128
EPS = 1e-6
SUBLN_EPS = 1e-5
LAMBDA_INIT = 0.8 - 0.6 * math.exp(-0.3 * 0)

LANES = 128
SUBLANES = 8
VMEM_LIMIT = 56 * 1024 * 1024
FFT_N2 = 128
VT_ROWS = ATT_V_DIM + 2 * SUBLANES
STABILISER_HEADROOM = 64.0
NEG_INF = float("-inf")


def _params(semantics, vmem=VMEM_LIMIT):
    return pltpu.CompilerParams(dimension_semantics=semantics, vmem_limit_bytes=vmem)


def _rms(x, g, eps):
    return x * lax.rsqrt(jnp.mean(x * x, axis=-1, keepdims=True) + eps) * g


def _dot_nt(a, b):
    return lax.dot_general(a, b, (((1,), (1,)), ((), ())), preferred_element_type=F32)


def _inproj_kernel(x_ref, g_ref, w_ref, c_ref, sa_ref, sb_ref,
                   zh_ref, qt1_ref, qt2_ref, k_ref, vt_ref):
    h = _rms(x_ref[...], g_ref[...], EPS).astype(BF16)
    z = jnp.dot(h, w_ref[...], preferred_element_type=F32)
    hy3 = 3 * HY_WIDTH
    zh_ref[...] = z[:, :hy3]
    c, sa, sb = c_ref[...], sa_ref[...], sb_ref[...]
    scale = ATT_HEAD_DIM ** -0.5 * math.log2(math.e)

    def rope(xb):
        return (xb * c + pltpu.roll(xb, LANES - ROT_DIM // 2, 1) * sa
                + pltpu.roll(xb, ROT_DIM // 2, 1) * sb)

    first = lax.broadcasted_iota(jnp.int32, (LANES, z.shape[0]), 0) < ATT_HEAD_DIM
    for hd in range(N_ATT_HEADS):
        lo, hi = hd * LANES, (hd + 1) * LANES
        qt = (rope(z[:, hy3 + lo:hy3 + hi]) * scale).T
        qt1_ref[lo:hi, :] = jnp.where(first, qt, 0.0).astype(BF16)
        qt2_ref[lo:hi, :] = jnp.where(first, 0.0, qt).astype(BF16)
        k_ref[:, lo:hi] = rope(z[:, hy3 + ATT_WIDTH + lo:hy3 + ATT_WIDTH + hi]).astype(BF16)
        vlo = hd * VT_ROWS
        vt_ref[vlo:vlo + ATT_V_DIM, :] = (
            z[:, hy3 + 2 * ATT_WIDTH + lo:hy3 + 2 * ATT_WIDTH + hi].T.astype(BF16))
        vt_ref[vlo + ATT_V_DIM:vlo + VT_ROWS, :] = jnp.ones(
            (VT_ROWS - ATT_V_DIM, z.shape[0]), BF16)


def _inproj(x, g, w_bf, rope_c, rope_sa, rope_sb, seq_len, tm=512):
    t = x.shape[0]
    nseq = seq_len // tm
    row = lambda i: (i, 0)
    col = lambda i: (0, i)
    fixed = lambda i: (0, 0)
    pos = lambda i: (i % nseq, 0)
    hy3 = 3 * HY_WIDTH
    tr_spec = pl.BlockSpec((ATT_WIDTH, tm), col)
    tr_shape = jax.ShapeDtypeStruct((ATT_WIDTH, t), BF16)
    return pl.pallas_call(
        _inproj_kernel,
        grid=(t // tm,),
        in_specs=[pl.BlockSpec((tm, D_MODEL), row),
                  pl.BlockSpec((1, D_MODEL), fixed),
                  pl.BlockSpec((D_MODEL, IN_WIDTH), fixed),
                  pl.BlockSpec((tm, LANES), pos),
                  pl.BlockSpec((tm, LANES), pos),
                  pl.BlockSpec((tm, LANES), pos)],
        out_specs=[pl.BlockSpec((tm, hy3), row), tr_spec, tr_spec,
                   pl.BlockSpec((tm, ATT_WIDTH), row),
                   pl.BlockSpec((N_ATT_HEADS * VT_ROWS, tm), col)],
        out_shape=[jax.ShapeDtypeStruct((t, hy3), F32), tr_shape, tr_shape,
                   jax.ShapeDtypeStruct((t, ATT_WIDTH), BF16),
                   jax.ShapeDtypeStruct((N_ATT_HEADS * VT_ROWS, t), BF16)],
        compiler_params=_params(("parallel",)),
        name="inproj",
    )(x, g, w_bf, rope_c, rope_sa, rope_sb)


def _store_channel_major(dst_ref, x, lead=()):
    rows, c = x.shape
    for j in range(rows // LANES):
        for cb in range(c // LANES):
            dst_ref[lead + (slice(cb * LANES, (cb + 1) * LANES), j, slice(None))] = (
                x[j * LANES:(j + 1) * LANES, cb * LANES:(cb + 1) * LANES].T)


def _shortconv_kernel(z_ref, zp_ref, zn_ref, w_ref, b_ref, ut_ref, x0_ref, *, blocks_per_seq):
    i = pl.program_id(0)
    z = z_ref[...]
    tl = z.shape[0]
    row = lax.broadcasted_iota(jnp.int32, z.shape, 0)
    at_start = (i % blocks_per_seq) == 0
    at_end = (i % blocks_per_seq) == blocks_per_seq - 1
    prev_row = jnp.where(at_start, 0.0, zp_ref[SUBLANES - 1:SUBLANES, :])
    next_row = jnp.where(at_end, 0.0, zn_ref[0:1, :])
    zm = jnp.where(row == 0, prev_row, pltpu.roll(z, 1, 0))
    zp = jnp.where(row == tl - 1, next_row, pltpu.roll(z, tl - 1, 0))
    w = w_ref[...]
    conv = zm * w[0:1] + z * w[1:2] + zp * w[2:3] + b_ref[...]
    x0_ref[...] = conv[:, :HY_WIDTH]
    _store_channel_major(ut_ref, conv[:, 2 * HY_WIDTH:] * conv[:, HY_WIDTH:2 * HY_WIDTH])


def _shortconv(zh, w, b, seq_len, tl=1024):
    t, hy3 = zh.shape
    per8 = tl // SUBLANES
    last8 = t // SUBLANES - 1
    return pl.pallas_call(
        functools.partial(_shortconv_kernel, blocks_per_seq=seq_len // tl),
        grid=(t // tl,),
        in_specs=[pl.BlockSpec((tl, hy3), lambda i: (i, 0)),
                  pl.BlockSpec((SUBLANES, hy3), lambda i: (jnp.maximum(i * per8 - 1, 0), 0)),
                  pl.BlockSpec((SUBLANES, hy3), lambda i: (jnp.minimum((i + 1) * per8, last8), 0)),
                  pl.BlockSpec((3, hy3), lambda i: (0, 0)),
                  pl.BlockSpec((1, hy3), lambda i: (0, 0))],
        out_specs=[pl.BlockSpec((HY_WIDTH, tl // LANES, LANES), lambda i: (0, i, 0)),
                   pl.BlockSpec((tl, HY_WIDTH), lambda i: (i, 0))],
        out_shape=[jax.ShapeDtypeStruct((HY_WIDTH, t // LANES, LANES), F32),
                   jax.ShapeDtypeStruct((t, HY_WIDTH), F32)],
        compiler_params=_params(("parallel",)),
        name="shortconv",
    )(zh, zh, zh, w, b)


def _filter_kernel(f_ref, w1_ref, b1_ref, f1_ref, w2_ref, b2_ref, f2_ref, w3_ref, b3_ref,
                   dl_ref, h_ref):
    i = pl.program_id(0)
    hp = lax.Precision.HIGHEST
    feats = f_ref[...]
    tl = feats.shape[0]
    h = jnp.sin(f1_ref[...] * (jnp.dot(feats, w1_ref[...], precision=hp,
                                       preferred_element_type=F32) + b1_ref[...]))
    h = jnp.sin(f2_ref[...] * (jnp.dot(h, w2_ref[...], precision=hp,
                                       preferred_element_type=F32) + b2_ref[...]))
    h3 = jnp.dot(h, w3_ref[...], precision=hp, preferred_element_type=F32) + b3_ref[...]
    decay = jnp.exp(-feats[:, 0:1] * dl_ref[...])
    _store_channel_major(h_ref, h3[:, :HY_WIDTH] * decay, lead=(0,))
    row = lax.broadcasted_iota(jnp.int32, (tl, HY_WIDTH), 0) + i * tl
    _store_channel_major(h_ref, jnp.where(row == 0, 0.0, h3[:, HY_WIDTH:] * decay), lead=(1,))


def _pad2(a, rows, cols):
    return jnp.pad(a, ((0, rows - a.shape[0]), (0, cols - a.shape[1])))


def _hyena_filters(seq_len, w1, b1, f1, w2, b2, f2, w3, b3, tl=1024):
    f32 = np.float32
    t = np.linspace(0.0, 1.0, seq_len).astype(f32)[:, None]
    w = f32(2.0 * math.pi) * np.arange(seq_len, dtype=f32) / f32(seq_len)
    bands = np.linspace(1e-4, FILTER_BANDS - 1, FILTER_BANDS).astype(f32)
    ang = (w[:, None] * bands[None, :]).astype(np.float64)
    feats_np = np.zeros((seq_len, LANES), f32)
    feats_np[:, :FILTER_EMB] = np.concatenate([t, np.cos(ang), -np.sin(ang)], axis=-1)
    feats = jnp.asarray(feats_np)
    max_decay = math.log(DECAY_TARGET) / FAST_DECAY_PCT
    min_decay = math.log(DECAY_TARGET) / SLOW_DECAY_PCT
    deltas = jnp.abs(jnp.linspace(min_decay, max_decay, HY_WIDTH, dtype=F32))[None, :]
    w1p = _pad2(w1, LANES, LANES)
    w2p = _pad2(w2, LANES, LANES)
    w3p = _pad2(w3, LANES, 2 * HY_WIDTH)
    pad1 = lambda v: _pad2(v[None, :], 1, LANES)
    fixed = lambda i: (0, 0)
    vec = pl.BlockSpec((1, LANES), fixed)
    mat = pl.BlockSpec((LANES, LANES), fixed)
    return pl.pallas_call(
        _filter_kernel,
        grid=(seq_len // tl,),
        in_specs=[pl.BlockSpec((tl, LANES), lambda i: (i, 0)), mat, vec, vec, mat, vec, vec,
                  pl.BlockSpec((LANES, 2 * HY_WIDTH), fixed),
                  pl.BlockSpec((1, 2 * HY_WIDTH), fixed),
                  pl.BlockSpec((1, HY_WIDTH), fixed)],
        out_specs=pl.BlockSpec((2, HY_WIDTH, tl // LANES, LANES), lambda i: (0, 0, i, 0)),
        out_shape=jax.ShapeDtypeStruct((2, HY_WIDTH, seq_len // LANES, LANES), F32),
        compiler_params=_params(("parallel",)),
        name="hyena_filter",
    )(feats, w1p, pad1(b1), pad1(f1), w2p, pad1(b2), pad1(f2), w3p, b3[None, :], deltas)


def _dft_tables(seq_len):
    n = 2 * seq_len
    n2 = FFT_N2
    n1 = n // n2
    n1h = n1 // 2
    a1 = (2.0 * np.pi / n1) * np.outer(np.arange(n1), np.arange(n1))
    f1r, f1i = np.cos(a1), -np.sin(a1)
    fwd1 = np.concatenate([f1r[:, :n1h], f1i[:, :n1h]], axis=0)
    inv1 = np.concatenate([f1r[:n1h, :], f1i[:n1h, :]], axis=1) / n
    a2 = (2.0 * np.pi / n2) * np.outer(np.arange(n2), np.arange(n2))
    f2 = np.concatenate([np.cos(a2), -np.sin(a2)], axis=1)
    at = (2.0 * np.pi / n) * np.outer(np.arange(n1), np.arange(n2))
    return (jnp.asarray(fwd1, BF16), jnp.asarray(inv1, BF16), jnp.asarray(f2, BF16),
            jnp.asarray(np.cos(at), F32), jnp.asarray(-np.sin(at), F32))


def _forward_spectra(tiles, fwd1, twr, twi, f2):
    n1, w = twr.shape
    tr, ti = _tile_lanes(twr, 2 * w), _tile_lanes(twi, 2 * w)
    firsts = [jnp.dot(fwd1, jnp.concatenate([xa, xb], axis=1).astype(BF16),
                      preferred_element_type=F32) for xa, xb in tiles]
    twiddled = []
    for a in firsts:
        ar, ai = a[:n1], a[n1:]
        zr, zi = ar * tr - ai * ti, ar * ti + ai * tr
        twiddled.append(jnp.concatenate([zr[:, :w], zi[:, :w], zr[:, w:], zi[:, w:]],
                                        axis=0).astype(BF16))
    seconds = [jnp.dot(z, f2, preferred_element_type=F32) for z in twiddled]

    def spectrum(q):
        qr, qi = q[:n1], q[n1:]
        return qr[:, :w] - qi[:, w:], qr[:, w:] + qi[:, :w]

    return [(spectrum(p[:2 * n1]), spectrum(p[2 * n1:])) for p in seconds]


def _inverse_signals(spectra, inv1, twr, twi, f2):
    n1, w = twr.shape
    firsts = [jnp.dot(jnp.concatenate([wa[0], wa[1], wb[0], wb[1]], axis=0).astype(BF16), f2,
                      preferred_element_type=F32) for wa, wb in spectra]

    def untwiddle(r):
        rr, ri = r[:n1], r[n1:]
        vr, vi = rr[:, :w] + ri[:, w:], ri[:, :w] - rr[:, w:]
        return vr * twr + vi * twi, vi * twr - vr * twi

    stacked = []
    for q in firsts:
        (bra, bia), (brb, bib) = untwiddle(q[:2 * n1]), untwiddle(q[2 * n1:])
        stacked.append(jnp.concatenate([jnp.concatenate([bra, brb], axis=1),
                                        jnp.concatenate([bia, bib], axis=1)],
                                       axis=0).astype(BF16))
    outs = [jnp.dot(inv1, r, preferred_element_type=F32) for r in stacked]
    return [(y[:, :w], y[:, w:]) for y in outs]


def _tile_lanes(x, width):
    return jnp.concatenate([x] * (width // x.shape[1]), axis=1)


def _spectrum_kernel(h_ref, fwd1_ref, f2_ref, twr_ref, twi_ref, kr_ref, ki_ref, *, chains):
    fwd1, f2, twr, twi = fwd1_ref[...], f2_ref[...], twr_ref[...], twi_ref[...]

    def body(i, carry):
        cs = [chains * i + j for j in range(chains)]
        out = _forward_spectra([(h_ref[0, c], h_ref[1, c]) for c in cs], fwd1, twr, twi, f2)
        for c, ((fr, fi), (gr, gi)) in zip(cs, out):
            kr_ref[c] = (fr + gr).astype(BF16)
            ki_ref[c] = (fi - gi).astype(BF16)
        return carry

    lax.fori_loop(0, h_ref.shape[1] // chains, body, 0)


def _filter_spectrum(h3, tables, cb=16):
    fwd1, _, f2, twr, twi = tables
    _, c, n1h, w = h3.shape
    n1 = 2 * n1h
    const = lambda a: pl.BlockSpec(a.shape, lambda i: (0,) * a.ndim)
    o_spec = pl.BlockSpec((cb, n1, w), lambda i: (i, 0, 0))
    return pl.pallas_call(
        functools.partial(_spectrum_kernel, chains=4),
        grid=(c // cb,),
        in_specs=[pl.BlockSpec((2, cb, n1h, w), lambda i: (0, i, 0, 0)),
                  const(fwd1), const(f2), const(twr), const(twi)],
        out_specs=[o_spec, o_spec],
        out_shape=[jax.ShapeDtypeStruct((c, n1, w), BF16)] * 2,
        compiler_params=_params(("parallel",)),
        name="filter_spectrum",
    )(h3, fwd1, f2, twr, twi)


def _longconv_kernel(u_ref, kr_ref, ki_ref, skip_ref, fwd1_ref, inv1_ref, f2_ref, twr_ref,
                     twi_ref, y_ref, *, chains):
    fwd1, inv1, f2 = fwd1_ref[...], inv1_ref[...], f2_ref[...]
    twr, twi = twr_ref[...], twi_ref[...]

    def filtered(c, spec):
        kr, ki = kr_ref[c].astype(F32), ki_ref[c].astype(F32)
        return spec[0] * kr - spec[1] * ki, spec[0] * ki + spec[1] * kr

    def body(i, carry):
        cs = [2 * (chains * i + j) for j in range(chains)]
        xs = [(u_ref[c], u_ref[c + 1]) for c in cs]
        spectra = _forward_spectra(xs, fwd1, twr, twi, f2)
        prods = [(filtered(c, sa), filtered(c + 1, sb)) for c, (sa, sb) in zip(cs, spectra)]
        for c, (xa, xb), (ya, yb) in zip(cs, xs, _inverse_signals(prods, inv1, twr, twi, f2)):
            y_ref[c] = ya + skip_ref[c] * xa
            y_ref[c + 1] = yb + skip_ref[c + 1] * xb
        return carry

    lax.fori_loop(0, u_ref.shape[0] // (2 * chains), body, 0)


def _longconv(ut3, kr, ki, skip3, tables, batch, cb=16):
    fwd1, inv1, f2, twr, twi = tables
    c, rows, w = ut3.shape
    n1h = rows // batch
    n1 = 2 * n1h
    const = lambda a: pl.BlockSpec(a.shape, lambda i, b: (0,) * a.ndim)
    sig = pl.BlockSpec((cb, n1h, w), lambda i, b: (i, b, 0))
    spec = pl.BlockSpec((cb, n1, w), lambda i, b: (i, 0, 0))
    return pl.pallas_call(
        functools.partial(_longconv_kernel, chains=2),
        grid=(c // cb, batch),
        in_specs=[sig, spec, spec, pl.BlockSpec((cb, 1, w), lambda i, b: (i, 0, 0)),
                  const(fwd1), const(inv1), const(f2), const(twr), const(twi)],
        out_specs=sig,
        out_shape=jax.ShapeDtypeStruct((c, rows, w), F32),
        compiler_params=_params(("parallel", "parallel")),
        name="long_conv",
    )(ut3, kr, ki, skip3, fwd1, inv1, f2, twr, twi)


def _gate_kernel(y_ref, x0_ref, o_ref):
    rows, c = x0_ref.shape
    for j in range(rows // LANES):
        for cb in range(c // LANES):
            r, l = slice(j * LANES, (j + 1) * LANES), slice(cb * LANES, (cb + 1) * LANES)
            o_ref[r, l] = (x0_ref[r, l] * y_ref[l, j, :].T).astype(BF16)


def _gate(y3, x0, tl=1024):
    t, c = x0.shape
    return pl.pallas_call(
        _gate_kernel,
        grid=(t // tl,),
        in_specs=[pl.BlockSpec((c, tl // LANES, LANES), lambda i: (0, i, 0)),
                  pl.BlockSpec((tl, c), lambda i: (i, 0))],
        out_specs=pl.BlockSpec((tl, c), lambda i: (i, 0)),
        out_shape=jax.ShapeDtypeStruct((t, c), BF16),
        compiler_params=_params(("parallel",)),
        name="hyena_gate",
    )(y3, x0)


def _hyena_long_conv(ut3, x0, filters, skip, batch, seq_len):
    tables = _dft_tables(seq_len)
    kr, ki = _filter_spectrum(filters, tables)
    skip3 = jnp.broadcast_to(skip[:, None, None], (HY_WIDTH, 1, LANES))
    return _gate(_longconv(ut3, kr, ki, skip3, tables, batch), x0)


def _pipelined(units, scores, softmax, values):
    s_next = scores(units[0])
    pending = None
    for i, unit in enumerate(units):
        s_cur = s_next
        if i + 1 < len(units):
            s_next = scores(units[i + 1])
        if pending is not None:
            values(*pending)
        pending = (unit,) + softmax(unit, s_cur)
    values(*pending)


def _attn_kernel(q1_ref, q2_ref, k_ref, vt_ref, lam_ref, sg_ref, o_ref,
                 m1, a1, m2, a2, *, chunk):
    ki = pl.program_id(3)
    q_refs = (q1_ref, q2_ref)
    m_refs = (m1, m2)
    a_refs = (a1, a2)
    n_chunks = k_ref.shape[0] // chunk
    units = [(c, comp) for c in range(n_chunks) for comp in (0, 1)]

    def scores(unit):
        c, comp = unit
        return jnp.dot(k_ref[c * chunk:(c + 1) * chunk, :], q_refs[comp][...],
                       preferred_element_type=F32)

    def value_dot(unit, p):
        c = unit[0]
        return jnp.dot(vt_ref[:, c * chunk:(c + 1) * chunk], p, preferred_element_type=F32)

    @pl.when(ki == 0)
    def _():
        for comp in (0, 1):
            head = jnp.dot(k_ref[0:2 * SUBLANES, :], q_refs[comp][...], preferred_element_type=F32)
            m_refs[comp][...] = jnp.max(head, axis=0, keepdims=True)
            a_refs[comp][...] = jnp.zeros(a_refs[comp].shape, F32)

    m_start = [m1[...], m2[...]]
    step_max = [None, None]
    step_acc = [None, None]

    def fast_softmax(unit, s):
        comp = unit[1]
        cm = jnp.max(s, axis=0, keepdims=True)
        step_max[comp] = cm if step_max[comp] is None else jnp.maximum(step_max[comp], cm)
        return (jnp.exp2(s - m_start[comp]).astype(BF16),)

    def fast_values(unit, p):
        comp = unit[1]
        d = value_dot(unit, p)
        step_acc[comp] = d if step_acc[comp] is None else step_acc[comp] + d

    _pipelined(units, scores, fast_softmax, fast_values)
    excess = jnp.maximum(step_max[0] - m_start[0], step_max[1] - m_start[1])
    fast_ok = jnp.max(excess) <= STABILISER_HEADROOM

    @pl.when(fast_ok)
    def _():
        for comp in (0, 1):
            m_new = jnp.maximum(m_start[comp], step_max[comp])
            a_refs[comp][...] = jnp.exp2(m_start[comp] - m_new) * (a_refs[comp][...] + step_acc[comp])
            m_refs[comp][...] = m_new

    @pl.when(jnp.logical_not(fast_ok))
    def _():
        stats = [[m1[...], a1[...]], [m2[...], a2[...]]]

        def softmax(unit, s):
            m_prev = stats[unit[1]][0]
            m_new = jnp.maximum(m_prev, jnp.max(s, axis=0, keepdims=True))
            stats[unit[1]][0] = m_new
            return jnp.exp2(s - m_new).astype(BF16), jnp.exp2(m_prev - m_new)

        def values(unit, p, alpha):
            comp = unit[1]
            stats[comp][1] = alpha * stats[comp][1] + value_dot(unit, p)

        _pipelined(units, scores, softmax, values)
        for comp in (0, 1):
            m_refs[comp][...] = stats[comp][0]
            a_refs[comp][...] = stats[comp][1]

    @pl.when(ki == pl.num_programs(3) - 1)
    def _():
        lv = lam_ref[...]
        lam = (jnp.exp(jnp.sum(lv[0:1] * lv[1:2], axis=1, keepdims=True))
               - jnp.exp(jnp.sum(lv[2:3] * lv[3:4], axis=1, keepdims=True)) + LAMBDA_INIT)
        d = ATT_V_DIM
        o = (a1[:d, :] / a1[d:d + 1, :] - lam * (a2[:d, :] / a2[d:d + 1, :]))
        inv = lax.rsqrt(jnp.mean(o * o, axis=0, keepdims=True) + SUBLN_EPS)
        g = pltpu.repeat(sg_ref[...], o.shape[1] // LANES, 1)
        o_ref[...] = (o * inv * g * (1.0 - LAMBDA_INIT)).astype(BF16)


def _attention(qt1, qt2, k, vt, lam_vecs, subln_g, batch, seq_len, tq=1024, tk=4096, chunk=512):
    tk = min(tk, seq_len)
    nq, nk = seq_len // tq, seq_len // tk
    q_spec = pl.BlockSpec((LANES, tq), lambda b, h, i, j: (h, b * nq + i))
    sg = jnp.broadcast_to(subln_g[:, None], (ATT_V_DIM, LANES))
    return pl.pallas_call(
        functools.partial(_attn_kernel, chunk=chunk),
        grid=(batch, N_ATT_HEADS, nq, nk),
        in_specs=[q_spec, q_spec,
                  pl.BlockSpec((tk, LANES), lambda b, h, i, j: (b * nk + j, h)),
                  pl.BlockSpec((VT_ROWS, tk), lambda b, h, i, j: (h, b * nk + j)),
                  pl.BlockSpec((4, ATT_HEAD_DIM), lambda b, h, i, j: (0, 0)),
                  pl.BlockSpec((ATT_V_DIM, LANES), lambda b, h, i, j: (0, 0))],
        out_specs=q_spec,
        out_shape=jax.ShapeDtypeStruct((ATT_WIDTH, batch * seq_len), BF16),
        scratch_shapes=[pltpu.VMEM((1, tq), F32), pltpu.VMEM((VT_ROWS, tq), F32)] * 2,
        compiler_params=_params(("parallel", "parallel", "parallel", "arbitrary")),
        name="diff_attention",
    )(qt1, qt2, k, vt, lam_vecs, sg)


def _split_bf16(x):
    hi = x.astype(BF16)
    return hi, (x - hi.astype(F32)).astype(BF16)


def _outproj_kernel(yh_ref, ya_ref, x_ref, wo_ref, g_ref, wqt_ref, kh_ref, kl_ref,
                    x2_ref, xnt_ref, st_ref):
    wo = wo_ref[...]
    x2 = (x_ref[...] + jnp.dot(yh_ref[...], wo[:HY_WIDTH], preferred_element_type=F32)
          + lax.dot_general(ya_ref[...], wo[HY_WIDTH:], (((0,), (0,)), ((), ())),
                            preferred_element_type=F32))
    x2_ref[...] = x2
    xn = _rms(x2, g_ref[...], EPS)
    xnt = xn.T.astype(BF16)
    xnt_ref[...] = xnt
    qt = jnp.dot(wqt_ref[...], xnt, preferred_element_type=F32)
    for hc in range(2 * PEER_HEADS):
        qh, ql = _split_bf16(qt[hc * PEER_HALF:(hc + 1) * PEER_HALF, :])
        kh, kl = kh_ref[hc], kl_ref[hc]
        st_ref[hc * PEER_KEYS:(hc + 1) * PEER_KEYS, :] = (
            jnp.dot(kh, qh, preferred_element_type=F32)
            + jnp.dot(kh, ql, preferred_element_type=F32)
            + jnp.dot(kl, qh, preferred_element_type=F32))


def _outproj(yh, ya, x, wo_bf, g2, wqt_bf, keys_hi, keys_lo, tm=512):
    t = x.shape[0]
    qw = 2 * PEER_HEADS * PEER_HALF
    sw = 2 * PEER_HEADS * PEER_KEYS
    row = lambda i: (i, 0)
    fixed = lambda i: (0, 0)
    fixed3 = lambda i: (0, 0, 0)
    return pl.pallas_call(
        _outproj_kernel,
        grid=(t // tm,),
        in_specs=[pl.BlockSpec((tm, HY_WIDTH), row),
                  pl.BlockSpec((ATT_WIDTH, tm), lambda i: (0, i)),
                  pl.BlockSpec((tm, D_MODEL), row),
                  pl.BlockSpec((HY_WIDTH + ATT_WIDTH, D_MODEL), fixed),
                  pl.BlockSpec((1, D_MODEL), fixed),
                  pl.BlockSpec((qw, D_MODEL), fixed),
                  pl.BlockSpec((2 * PEER_HEADS, PEER_KEYS, PEER_HALF), fixed3),
                  pl.BlockSpec((2 * PEER_HEADS, PEER_KEYS, PEER_HALF), fixed3)],
        out_specs=[pl.BlockSpec((tm, D_MODEL), row), pl.BlockSpec((D_MODEL, tm), lambda i: (0, i)),
                   pl.BlockSpec((sw, tm), lambda i: (0, i))],
        out_shape=[jax.ShapeDtypeStruct((t, D_MODEL), F32),
                   jax.ShapeDtypeStruct((D_MODEL, t), BF16),
                   jax.ShapeDtypeStruct((sw, t), F32)],
        compiler_params=_params(("parallel",)),
        name="outproj_peer_query",
    )(yh, ya, x, wo_bf, g2, wqt_bf, keys_hi, keys_lo)


def _extract_max(pieces, iotas, total, exact):
    m = pieces[0]
    for p in pieces[1:]:
        m = jnp.maximum(m, p)
    m = jnp.max(m, axis=0, keepdims=True)
    hot = [p == m for p in pieces]
    if exact:
        cand = [jnp.where(h, io, total) for h, io in zip(hot, iotas)]
        idx = cand[0]
        for c in cand[1:]:
            idx = jnp.minimum(idx, c)
        idx = jnp.min(idx, axis=0, keepdims=True)
        hot = [io == idx for io in iotas]
    return m, hot, [jnp.where(h, NEG_INF, p) for h, p in zip(hot, pieces)]


def _topk_rank(s, exact, want_rank):
    rank = jnp.full(s.shape, float(PEER_TOPK), F32) if want_rank else None
    iota = [lax.broadcasted_iota(jnp.int32, s.shape, 0)]
    vals = []
    for r in range(PEER_TOPK):
        m, hot, (s,) = _extract_max([s], iota, s.shape[0], exact)
        vals.append(m)
        if want_rank:
            rank = jnp.where(hot[0], float(r), rank)
    return vals, rank


_PAIRS = [(k0, k1) for k0 in range(PEER_TOPK) for k1 in range(PEER_TOPK // (k0 + 1))]
_PAIR_ROWS = -(-len(_PAIRS) // SUBLANES) * SUBLANES


def _rows_to_tiles(rows, width):
    sub = lax.broadcasted_iota(jnp.int32, (SUBLANES, width), 0)
    tiles = []
    for t0 in range(0, len(rows), SUBLANES):
        tile = jnp.full((SUBLANES, width), NEG_INF, F32)
        for r in range(SUBLANES):
            if rows[t0 + r] is not None:
                tile = jnp.where(sub == r, rows[t0 + r], tile)
        tiles.append(tile)
    return tiles


def _route_tables(s0, s1, exact):
    tt = s0.shape[1]
    v0, rank0 = _topk_rank(s0, exact, want_rank=exact)
    v1, rank1 = _topk_rank(s1, exact, want_rank=True)
    pad = [None] * (_PAIR_ROWS - len(_PAIRS))
    tiles = [a + b for a, b in zip(_rows_to_tiles([v0[k0] for k0, _ in _PAIRS] + pad, tt),
                                   _rows_to_tiles([v1[k1] for _, k1 in _PAIRS] + pad, tt))]
    iotas = [lax.broadcasted_iota(jnp.int32, (SUBLANES, tt), 0) + t0
             for t0 in range(0, _PAIR_ROWS, SUBLANES)]
    cmax = v0[0] + v1[0]
    z = None
    for r in range(PEER_TOPK):
        m, _, tiles = _extract_max(tiles, iotas, _PAIR_ROWS, exact)
        z = jnp.exp(m - cmax) if z is None else z + jnp.exp(m - cmax)
    cnt = jnp.zeros(s0.shape, F32)
    total = None
    for k0 in range(PEER_TOPK):
        rows = [r for r, (a, _) in enumerate(_PAIRS) if a == k0]
        taken = None
        for ti in sorted({r // SUBLANES for r in rows}):
            inside = (iotas[ti] >= rows[0]) & (iotas[ti] <= rows[-1])
            part = jnp.sum(jnp.where(inside & (tiles[ti] == NEG_INF), 1.0, 0.0),
                           axis=0, keepdims=True)
            taken = part if taken is None else taken + part
        total = taken if total is None else total + taken
        cnt = jnp.where((rank0 == float(k0)) if exact else (s0 == v0[k0]), taken, cnt)
    k = float(PEER_TOPK)
    count = lambda mask: jnp.sum(jnp.where(mask, 1.0, 0.0), axis=0, keepdims=True)
    top0 = (rank0 < k) if exact else (s0 >= v0[PEER_TOPK - 1])
    ok = (total == k) & (count(top0) == k) & (count(rank1 < k) == k)
    return cnt, jnp.exp(s0 - v0[0]) / z, rank1, jnp.exp(s1 - v1[0]), ok


def _route_kernel(st_ref, cnt_ref, w0_ref, r1_ref, e1_ref):
    def emit(exact):
        cnt, w0, rank1, e1, ok = _route_tables(st_ref[:PEER_KEYS, :], st_ref[PEER_KEYS:, :], exact)
        cnt_ref[0] = cnt
        w0_ref[0] = w0 * GELU_OUT_SCALE
        r1_ref[0] = rank1.astype(BF16)
        e1_ref[0] = e1.astype(BF16)
        return ok

    ok = emit(exact=False)
    clean = jnp.min(jnp.where(ok, 1, 0))

    @pl.when(clean == 0)
    def _():
        emit(exact=True)


def _route(st, tt=512):
    sw, t = st.shape
    o_spec = pl.BlockSpec((1, PEER_KEYS, tt), lambda i, h: (h, 0, i))
    tab = lambda dt: jax.ShapeDtypeStruct((PEER_HEADS, PEER_KEYS, t), dt)
    return pl.pallas_call(
        _route_kernel,
        grid=(t // tt, PEER_HEADS),
        in_specs=[pl.BlockSpec((2 * PEER_KEYS, tt), lambda i, h: (h, i))],
        out_specs=[o_spec] * 4,
        out_shape=[tab(F32), tab(F32), tab(BF16), tab(BF16)],
        compiler_params=_params(("parallel", "parallel")),
        name="peer_route",
    )(st)


GELU_IN_SCALE = 2.0 ** -0.5
GELU_OUT_SCALE = 0.5 * 2.0 ** 0.5


def _gelu_core(y):
    return y * (1.0 + lax.erf(y))


def _peer_kernel(xnt_ref, u_ref, vt_ref, cnt_ref, w0_ref, r1_ref, e1_ref, x2_ref, gf_ref,
                 y_ref, acc_ref, *, rows_per_step, rows_per_unit):
    e = pl.program_id(1)

    @pl.when(e == 0)
    def _():
        acc_ref[...] = jnp.zeros(acc_ref.shape, F32)

    tt = xnt_ref.shape[1]
    unit = rows_per_unit * PEER_KEYS
    n_units = rows_per_step // rows_per_unit

    def hidden(s):
        return jnp.dot(u_ref[s * unit:(s + 1) * unit, :], xnt_ref[...],
                       preferred_element_type=F32)

    def activation(s, hid):
        gates = []
        for ii in range(s * rows_per_unit, (s + 1) * rows_per_unit):
            g = None
            for h in range(PEER_HEADS):
                cnt_row = jnp.broadcast_to(cnt_ref[h, ii:ii + 1, :], (PEER_KEYS, tt)).astype(BF16)
                w_row = jnp.broadcast_to(w0_ref[h, ii:ii + 1, :], (PEER_KEYS, tt)).astype(BF16)
                term = jnp.where(r1_ref[h] < cnt_row, e1_ref[h] * w_row, jnp.zeros((), BF16))
                g = term if g is None else g + term
            gates.append(g)
        return _gelu_core(hid).astype(BF16) * jnp.concatenate(gates, axis=0)

    def mix(s, act):
        return jnp.dot(vt_ref[:, s * unit:(s + 1) * unit], act, preferred_element_type=F32)

    acc = acc_ref[...]
    h_next = hidden(0)
    pending = None
    for s in range(n_units):
        h_cur = h_next
        if s + 1 < n_units:
            h_next = hidden(s + 1)
        if pending is not None:
            acc = acc + mix(*pending)
        pending = (s, activation(s, h_cur))
    acc_ref[...] = acc + mix(*pending)

    @pl.when(e == pl.num_programs(1) - 1)
    def _():
        x3 = x2_ref[...] + acc_ref[...].T
        y_ref[...] = _rms(x3, gf_ref[...], EPS)


def _peer(xnt, u_bf, vt_bf, cnt, w0, r1, e1, x2, gf, tt=512, rows_per_step=16, rows_per_unit=2):
    t = xnt.shape[1]
    te = rows_per_step * PEER_KEYS
    tok = lambda i, e: (i, 0)
    tab_i = pl.BlockSpec((PEER_HEADS, rows_per_step, tt), lambda i, e: (0, e, i))
    tab_j = pl.BlockSpec((PEER_HEADS, PEER_KEYS, tt), lambda i, e: (0, 0, i))
    return pl.pallas_call(
        functools.partial(_peer_kernel, rows_per_step=rows_per_step,
                          rows_per_unit=rows_per_unit),
        grid=(t // tt, PEER_EXPERTS // te),
        in_specs=[pl.BlockSpec((D_MODEL, tt), lambda i, e: (0, i)),
                  pl.BlockSpec((te, D_MODEL), lambda i, e: (e, 0)),
                  pl.BlockSpec((D_MODEL, te), lambda i, e: (0, e)),
                  tab_i, tab_i, tab_j, tab_j,
                  pl.BlockSpec((tt, D_MODEL), tok),
                  pl.BlockSpec((1, D_MODEL), lambda i, e: (0, 0))],
        out_specs=pl.BlockSpec((tt, D_MODEL), tok),
        out_shape=jax.ShapeDtypeStruct((t, D_MODEL), F32),
        scratch_shapes=[pltpu.VMEM((D_MODEL, tt), F32)],
        compiler_params=_params(("parallel", "arbitrary")),
        name="peer_dense",
    )(xnt, u_bf, vt_bf, cnt, w0, r1, e1, x2, gf)


def _rope_tables(seq_len):
    f32 = np.float32
    inv = f32(ROPE_THETA) ** (-np.arange(0, ROT_DIM, 2, dtype=f32) / f32(ROT_DIM))
    ang = (np.arange(seq_len, dtype=f32)[:, None] * inv[None, :]).astype(np.float64)
    cos, sin = np.cos(ang), np.sin(ang)
    half = ROT_DIM // 2
    comp = np.zeros((3, seq_len, ATT_HEAD_DIM))
    comp[0, :, :half], comp[0, :, half:ROT_DIM], comp[0, :, ROT_DIM:] = cos, cos, 1.0
    comp[1, :, :half] = -sin
    comp[2, :, half:ROT_DIM] = sin
    both = np.concatenate([comp, comp], axis=2).astype(f32)
    return jnp.asarray(both[0]), jnp.asarray(both[1]), jnp.asarray(both[2])


def _trunk(x, p):
    batch, seq_len, _ = x.shape
    xf = x.reshape(batch * seq_len, D_MODEL)
    rope_c, rope_sa, rope_sb = _rope_tables(seq_len)
    zh, qt1, qt2, k, vt = _inproj(xf, p["norm1_g"], p["w_in"], rope_c, rope_sa, rope_sb, seq_len)
    u, x0 = _shortconv(zh, p["hy_conv_w"], p["hy_conv_b"], seq_len)
    filters = _hyena_filters(seq_len, *p["filter"])
    yh = _hyena_long_conv(u, x0, filters, p["hy_skip"], batch, seq_len)
    ya = _attention(qt1, qt2, k, vt, p["lam_vecs"], p["subln_g"], batch, seq_len)
    x2, xnt, st = _outproj(yh, ya, xf, p["w_out"], p["norm2_g"], p["wqt"],
                          p["keys_hi"], p["keys_lo"])
    cnt, w0, r1, e1 = _route(st)
    y = _peer(xnt, p["peer_u"], p["peer_vt"], cnt, w0, r1, e1, x2, p["normf_g"])
    return y.reshape(batch, seq_len, D_MODEL)


def kernel(x_prompt, x_sample, norm1_g, w_in, hy_conv_w, hy_conv_b, hf_w1, hf_b1, hf_freq1,
           hf_w2, hf_b2, hf_freq2, hf_w3, hf_b3, hy_skip, lam_q1, lam_k1, lam_q2, lam_k2,
           subln_g, w_out, norm2_g, peer_wq, peer_keys, peer_u, peer_v, normf_g):
    l = 0
    keys = peer_keys[l].reshape(2 * PEER_HEADS, PEER_KEYS, PEER_HALF)
    keys_hi = keys.astype(BF16)
    p = {
        "norm1_g": norm1_g[l][None, :],
        "w_in": w_in[l].astype(BF16),
        "hy_conv_w": hy_conv_w[l],
        "hy_conv_b": hy_conv_b[l][None, :],
        "filter": (hf_w1[l], hf_b1[l], hf_freq1[l], hf_w2[l], hf_b2[l], hf_freq2[l],
                   hf_w3[l], hf_b3[l]),
        "hy_skip": hy_skip[l],
        "lam_vecs": jnp.stack([lam_q1[l], lam_k1[l], lam_q2[l], lam_k2[l]]),
        "subln_g": subln_g[l],
        "w_out": w_out[l].astype(BF16),
        "norm2_g": norm2_g[l][None, :],
        "wqt": peer_wq[l].astype(BF16).T,
        "keys_hi": keys_hi,
        "keys_lo": (keys - keys_hi.astype(F32)).astype(BF16),
        "peer_u": (peer_u[l] * GELU_IN_SCALE).astype(BF16),
        "peer_vt": peer_v[l].astype(BF16).T,
        "normf_g": normf_g[None, :],
    }
    return (_trunk(x_prompt, p), _trunk(x_sample, p))
```

```python
import functools
import math

import jax
import jax.numpy as jnp
import numpy as np
from jax import lax
from jax.experimental import pallas as pl
from jax.experimental.pallas import tpu as pltpu

F32 = jnp.float32
BF16 = jnp.bfloat16

D_MODEL = 1024
HY_WIDTH = 512
ATT_WIDTH = 512
N_ATT_HEADS = 4
ATT_HEAD_DIM = 64
ATT_V_DIM = 2 * ATT_HEAD_DIM
ROT_DIM = ATT_HEAD_DIM // 4
ROPE_THETA = 500000.0
IN_WIDTH = 3 * HY_WIDTH + 3 * ATT_WIDTH
FILTER_BANDS = 8
FILTER_EMB = 1 + 2 * FILTER_BANDS
FILTER_HIDDEN = 64
FAST_DECAY_PCT = 0.3
SLOW_DECAY_PCT = 1.5
DECAY_TARGET = 1e-2
PEER_HEADS = 8
PEER_KEYS = 128
PEER_EXPERTS = PEER_KEYS * PEER_KEYS
PEER_TOPK = 16
PEER_HALF = 128
EPS = 1e-6
SUBLN_EPS = 1e-5
LAMBDA_INIT = 0.8 - 0.6 * math.exp(-0.3 * 0)

LANES = 128
SUBLANES = 8
VMEM_LIMIT = 56 * 1024 * 1024
FFT_N2 = 128
VT_ROWS = ATT_V_DIM + 2 * SUBLANES
STABILISER_HEADROOM = 64.0
NEG_INF = float("-inf")


def _params(semantics, vmem=VMEM_LIMIT):
    return pltpu.CompilerParams(dimension_semantics=semantics, vmem_limit_bytes=vmem)


def _rms(x, g, eps):
    return x * lax.rsqrt(jnp.mean(x * x, axis=-1, keepdims=True) + eps) * g


def _dot_nt(a, b):
    return lax.dot_general(a, b, (((1,), (1,)), ((), ())), preferred_element_type=F32)


def _inproj_kernel(x_ref, g_ref, w_ref, c_ref, sa_ref, sb_ref,
                   zh_ref, qt1_ref, qt2_ref, k_ref, vt_ref):
    h = _rms(x_ref[...], g_ref[...], EPS).astype(BF16)
    z = jnp.dot(h, w_ref[...], preferred_element_type=F32)
    hy3 = 3 * HY_WIDTH
    zh_ref[...] = z[:, :hy3]
    c, sa, sb = c_ref[...], sa_ref[...], sb_ref[...]
    scale = ATT_HEAD_DIM ** -0.5 * math.log2(math.e)

    def rope(xb):
        return (xb * c + pltpu.roll(xb, LANES - ROT_DIM // 2, 1) * sa
                + pltpu.roll(xb, ROT_DIM // 2, 1) * sb)

    first = lax.broadcasted_iota(jnp.int32, (LANES, z.shape[0]), 0) < ATT_HEAD_DIM
    for hd in range(N_ATT_HEADS):
        lo, hi = hd * LANES, (hd + 1) * LANES
        qt = (rope(z[:, hy3 + lo:hy3 + hi]) * scale).T
        qt1_ref[lo:hi, :] = jnp.where(first, qt, 0.0).astype(BF16)
        qt2_ref[lo:hi, :] = jnp.where(first, 0.0, qt).astype(BF16)
        k_ref[:, lo:hi] = rope(z[:, hy3 + ATT_WIDTH + lo:hy3 + ATT_WIDTH + hi]).astype(BF16)
        vlo = hd * VT_ROWS
        vt_ref[vlo:vlo + ATT_V_DIM, :] = (
            z[:, hy3 + 2 * ATT_WIDTH + lo:hy3 + 2 * ATT_WIDTH + hi].T.astype(BF16))
        vt_ref[vlo + ATT_V_DIM:vlo + VT_ROWS, :] = jnp.ones(
            (VT_ROWS - ATT_V_DIM, z.shape[0]), BF16)


def _inproj(x, g, w_bf, rope_c, rope_sa, rope_sb, seq_len, tm=512):
    t = x.shape[0]
    nseq = seq_len // tm
    row = lambda i: (i, 0)
    col = lambda i: (0, i)
    fixed = lambda i: (0, 0)
    pos = lambda i: (i % nseq, 0)
    hy3 = 3 * HY_WIDTH
    tr_spec = pl.BlockSpec((ATT_WIDTH, tm), col)
    tr_shape = jax.ShapeDtypeStruct((ATT_WIDTH, t), BF16)
    return pl.pallas_call(
        _inproj_kernel,
        grid=(t // tm,),
        in_specs=[pl.BlockSpec((tm, D_MODEL), row),
                  pl.BlockSpec((1, D_MODEL), fixed),
                  pl.BlockSpec((D_MODEL, IN_WIDTH), fixed),
                  pl.BlockSpec((tm, LANES), pos),
                  pl.BlockSpec((tm, LANES), pos),
                  pl.BlockSpec((tm, LANES), pos)],
        out_specs=[pl.BlockSpec((tm, hy3), row), tr_spec, tr_spec,
                   pl.BlockSpec((tm, ATT_WIDTH), row),
                   pl.BlockSpec((N_ATT_HEADS * VT_ROWS, tm), col)],
        out_shape=[jax.ShapeDtypeStruct((t, hy3), F32), tr_shape, tr_shape,
                   jax.ShapeDtypeStruct((t, ATT_WIDTH), BF16),
                   jax.ShapeDtypeStruct((N_ATT_HEADS * VT_ROWS, t), BF16)],
        compiler_params=_params(("parallel",)),
        name="inproj",
    )(x, g, w_bf, rope_c, rope_sa, rope_sb)


def _store_channel_major(dst_ref, x, lead=()):
    rows, c = x.shape
    for j in range(rows // LANES):
        for cb in range(c // LANES):
            dst_ref[lead + (slice(cb * LANES, (cb + 1) * LANES), j, slice(None))] = (
                x[j * LANES:(j + 1) * LANES, cb * LANES:(cb + 1) * LANES].T)


def _shortconv_kernel(z_ref, zp_ref, zn_ref, w_ref, b_ref, ut_ref, x0_ref, *, blocks_per_seq):
    i = pl.program_id(0)
    z = z_ref[...]
    tl = z.shape[0]
    row = lax.broadcasted_iota(jnp.int32, z.shape, 0)
    at_start = (i % blocks_per_seq) == 0
    at_end = (i % blocks_per_seq) == blocks_per_seq - 1
    prev_row = jnp.where(at_start, 0.0, zp_ref[SUBLANES - 1:SUBLANES, :])
    next_row = jnp.where(at_end, 0.0, zn_ref[0:1, :])
    zm = jnp.where(row == 0, prev_row, pltpu.roll(z, 1, 0))
    zp = jnp.where(row == tl - 1, next_row, pltpu.roll(z, tl - 1, 0))
    w = w_ref[...]
    conv = zm * w[0:1] + z * w[1:2] + zp * w[2:3] + b_ref[...]
    x0_ref[...] = conv[:, :HY_WIDTH]
    _store_channel_major(ut_ref, conv[:, 2 * HY_WIDTH:] * conv[:, HY_WIDTH:2 * HY_WIDTH])


def _shortconv(zh, w, b, seq_len, tl=1024):
    t, hy3 = zh.shape
    per8 = tl // SUBLANES
    last8 = t // SUBLANES - 1
    return pl.pallas_call(
        functools.partial(_shortconv_kernel, blocks_per_seq=seq_len // tl),
        grid=(t // tl,),
        in_specs=[pl.BlockSpec((tl, hy3), lambda i: (i, 0)),
                  pl.BlockSpec((SUBLANES, hy3), lambda i: (jnp.maximum(i * per8 - 1, 0), 0)),
                  pl.BlockSpec((SUBLANES, hy3), lambda i: (jnp.minimum((i + 1) * per8, last8), 0)),
                  pl.BlockSpec((3, hy3), lambda i: (0, 0)),
                  pl.BlockSpec((1, hy3), lambda i: (0, 0))],
        out_specs=[pl.BlockSpec((HY_WIDTH, tl // LANES, LANES), lambda i: (0, i, 0)),
                   pl.BlockSpec((tl, HY_WIDTH), lambda i: (i, 0))],
        out_shape=[jax.ShapeDtypeStruct((HY_WIDTH, t // LANES, LANES), F32),
                   jax.ShapeDtypeStruct((t, HY_WIDTH), F32)],
        compiler_params=_params(("parallel",)),
        name="shortconv",
    )(zh, zh, zh, w, b)


def _filter_kernel(f_ref, w1_ref, b1_ref, f1_ref, w2_ref, b2_ref, f2_ref, w3_ref, b3_ref,
                   dl_ref, h_ref):
    i = pl.program_id(0)
    hp = lax.Precision.HIGHEST
    feats = f_ref[...]
    tl = feats.shape[0]
    h = jnp.sin(f1_ref[...] * (jnp.dot(feats, w1_ref[...], precision=hp,
                                       preferred_element_type=F32) + b1_ref[...]))
    h = jnp.sin(f2_ref[...] * (jnp.dot(h, w2_ref[...], precision=hp,
                                       preferred_element_type=F32) + b2_ref[...]))
    h3 = jnp.dot(h, w3_ref[...], precision=hp, preferred_element_type=F32) + b3_ref[...]
    decay = jnp.exp(-feats[:, 0:1] * dl_ref[...])
    _store_channel_major(h_ref, h3[:, :HY_WIDTH] * decay, lead=(0,))
    row = lax.broadcasted_iota(jnp.int32, (tl, HY_WIDTH), 0) + i * tl
    _store_channel_major(h_ref, jnp.where(row == 0, 0.0, h3[:, HY_WIDTH:] * decay), lead=(1,))


def _pad2(a, rows, cols):
    return jnp.pad(a, ((0, rows - a.shape[0]), (0, cols - a.shape[1])))


def _hyena_filters(seq_len, w1, b1, f1, w2, b2, f2, w3, b3, tl=1024):
    f32 = np.float32
    t = np.linspace(0.0, 1.0, seq_len).astype(f32)[:, None]
    w = f32(2.0 * math.pi) * np.arange(seq_len, dtype=f32) / f32(seq_len)
    bands = np.linspace(1e-4, FILTER_BANDS - 1, FILTER_BANDS).astype(f32)
    ang = (w[:, None] * bands[None, :]).astype(np.float64)
    feats_np = np.zeros((seq_len, LANES), f32)
    feats_np[:, :FILTER_EMB] = np.concatenate([t, np.cos(ang), -np.sin(ang)], axis=-1)
    feats = jnp.asarray(feats_np)
    max_decay = math.log(DECAY_TARGET) / FAST_DECAY_PCT
    min_decay = math.log(DECAY_TARGET) / SLOW_DECAY_PCT
    deltas = jnp.abs(jnp.linspace(min_decay, max_decay, HY_WIDTH, dtype=F32))[None, :]
    w1p = _pad2(w1, LANES, LANES)
    w2p = _pad2(w2, LANES, LANES)
    w3p = _pad2(w3, LANES, 2 * HY_WIDTH)
    pad1 = lambda v: _pad2(v[None, :], 1, LANES)
    fixed = lambda i: (0, 0)
    vec = pl.BlockSpec((1, LANES), fixed)
    mat = pl.BlockSpec((LANES, LANES), fixed)
    return pl.pallas_call(
        _filter_kernel,
        grid=(seq_len // tl,),
        in_specs=[pl.BlockSpec((tl, LANES), lambda i: (i, 0)), mat, vec, vec, mat, vec, vec,
                  pl.BlockSpec((LANES, 2 * HY_WIDTH), fixed),
                  pl.BlockSpec((1, 2 * HY_WIDTH), fixed),
                  pl.BlockSpec((1, HY_WIDTH), fixed)],
        out_specs=pl.BlockSpec((2, HY_WIDTH, tl // LANES, LANES), lambda i: (0, 0, i, 0)),
        out_shape=jax.ShapeDtypeStruct((2, HY_WIDTH, seq_len // LANES, LANES), F32),
        compiler_params=_params(("parallel",)),
        name="hyena_filter",
    )(feats, w1p, pad1(b1), pad1(f1), w2p, pad1(b2), pad1(f2), w3p, b3[None, :], deltas)


def _dft_tables(seq_len):
    n = 2 * seq_len
    n2 = FFT_N2
    n1 = n // n2
    n1h = n1 // 2
    a1 = (2.0 * np.pi / n1) * np.outer(np.arange(n1), np.arange(n1))
    f1r, f1i = np.cos(a1), -np.sin(a1)
    fwd1 = np.concatenate([f1r[:, :n1h], f1i[:, :n1h]], axis=0)
    inv1 = np.concatenate([f1r[:n1h, :], f1i[:n1h, :]], axis=1) / n
    a2 = (2.0 * np.pi / n2) * np.outer(np.arange(n2), np.arange(n2))
    f2 = np.concatenate([np.cos(a2), -np.sin(a2)], axis=1)
    at = (2.0 * np.pi / n) * np.outer(np.arange(n1), np.arange(n2))
    return (jnp.asarray(fwd1, BF16), jnp.asarray(inv1, BF16), jnp.asarray(f2, BF16),
            jnp.asarray(np.cos(at), F32), jnp.asarray(-np.sin(at), F32))


def _forward_spectra(tiles, fwd1, twr, twi, f2):
    n1, w = twr.shape
    tr, ti = _tile_lanes(twr, 2 * w), _tile_lanes(twi, 2 * w)
    firsts = [jnp.dot(fwd1, jnp.concatenate([xa, xb], axis=1).astype(BF16),
                      preferred_element_type=F32) for xa, xb in tiles]
    twiddled = []
    for a in firsts:
        ar, ai = a[:n1], a[n1:]
        zr, zi = ar * tr - ai * ti, ar * ti + ai * tr
        twiddled.append(jnp.concatenate([zr[:, :w], zi[:, :w], zr[:, w:], zi[:, w:]],
                                        axis=0).astype(BF16))
    seconds = [jnp.dot(z, f2, preferred_element_type=F32) for z in twiddled]

    def spectrum(q):
        qr, qi = q[:n1], q[n1:]
        return qr[:, :w] - qi[:, w:], qr[:, w:] + qi[:, :w]

    return [(spectrum(p[:2 * n1]), spectrum(p[2 * n1:])) for p in seconds]


def _inverse_signals(spectra, inv1, twr, twi, f2):
    n1, w = twr.shape
    firsts = [jnp.dot(jnp.concatenate([wa[0], wa[1], wb[0], wb[1]], axis=0).astype(BF16), f2,
                      preferred_element_type=F32) for wa, wb in spectra]

    def untwiddle(r):
        rr, ri = r[:n1], r[n1:]
        vr, vi = rr[:, :w] + ri[:, w:], ri[:, :w] - rr[:, w:]
        return vr * twr + vi * twi, vi * twr - vr * twi

    stacked = []
    for q in firsts:
        (bra, bia), (brb, bib) = untwiddle(q[:2 * n1]), untwiddle(q[2 * n1:])
        stacked.append(jnp.concatenate([jnp.concatenate([bra, brb], axis=1),
                                        jnp.concatenate([bia, bib], axis=1)],
                                       axis=0).astype(BF16))
    outs = [jnp.dot(inv1, r, preferred_element_type=F32) for r in stacked]
    return [(y[:, :w], y[:, w:]) for y in outs]


def _tile_lanes(x, width):
    return jnp.concatenate([x] * (width // x.shape[1]), axis=1)


def _spectrum_kernel(h_ref, fwd1_ref, f2_ref, twr_ref, twi_ref, kr_ref, ki_ref, *, chains):
    fwd1, f2, twr, twi = fwd1_ref[...], f2_ref[...], twr_ref[...], twi_ref[...]

    def body(i, carry):
        cs = [chains * i + j for j in range(chains)]
        out = _forward_spectra([(h_ref[0, c], h_ref[1, c]) for c in cs], fwd1, twr, twi, f2)
        for c, ((fr, fi), (gr, gi)) in zip(cs, out):
            kr_ref[c] = (fr + gr).astype(BF16)
            ki_ref[c] = (fi - gi).astype(BF16)
        return carry

    lax.fori_loop(0, h_ref.shape[1] // chains, body, 0)


def _filter_spectrum(h3, tables, cb=16):
    fwd1, _, f2, twr, twi = tables
    _, c, n1h, w = h3.shape
    n1 = 2 * n1h
    const = lambda a: pl.BlockSpec(a.shape, lambda i: (0,) * a.ndim)
    o_spec = pl.BlockSpec((cb, n1, w), lambda i: (i, 0, 0))
    return pl.pallas_call(
        functools.partial(_spectrum_kernel, chains=8),
        grid=(c // cb,),
        in_specs=[pl.BlockSpec((2, cb, n1h, w), lambda i: (0, i, 0, 0)),
                  const(fwd1), const(f2), const(twr), const(twi)],
        out_specs=[o_spec, o_spec],
        out_shape=[jax.ShapeDtypeStruct((c, n1, w), BF16)] * 2,
        compiler_params=_params(("parallel",)),
        name="filter_spectrum",
    )(h3, fwd1, f2, twr, twi)


def _longconv_kernel(u_ref, kr_ref, ki_ref, skip_ref, fwd1_ref, inv1_ref, f2_ref, twr_ref,
                     twi_ref, y_ref, *, chains):
    fwd1, inv1, f2 = fwd1_ref[...], inv1_ref[...], f2_ref[...]
    twr, twi = twr_ref[...], twi_ref[...]

    def filtered(c, spec):
        kr, ki = kr_ref[c].astype(F32), ki_ref[c].astype(F32)
        return spec[0] * kr - spec[1] * ki, spec[0] * ki + spec[1] * kr

    def body(i, carry):
        cs = [2 * (chains * i + j) for j in range(chains)]
        xs = [(u_ref[c], u_ref[c + 1]) for c in cs]
        spectra = _forward_spectra(xs, fwd1, twr, twi, f2)
        prods = [(filtered(c, sa), filtered(c + 1, sb)) for c, (sa, sb) in zip(cs, spectra)]
        for c, (xa, xb), (ya, yb) in zip(cs, xs, _inverse_signals(prods, inv1, twr, twi, f2)):
            y_ref[c] = ya + skip_ref[c] * xa
            y_ref[c + 1] = yb + skip_ref[c + 1] * xb
        return carry

    lax.fori_loop(0, u_ref.shape[0] // (2 * chains), body, 0)


def _longconv(ut3, kr, ki, skip3, tables, batch, cb=16):
    fwd1, inv1, f2, twr, twi = tables
    c, rows, w = ut3.shape
    n1h = rows // batch
    n1 = 2 * n1h
    const = lambda a: pl.BlockSpec(a.shape, lambda i, b: (0,) * a.ndim)
    sig = pl.BlockSpec((cb, n1h, w), lambda i, b: (i, b, 0))
    spec = pl.BlockSpec((cb, n1, w), lambda i, b: (i, 0, 0))
    return pl.pallas_call(
        functools.partial(_longconv_kernel, chains=4),
        grid=(c // cb, batch),
        in_specs=[sig, spec, spec, pl.BlockSpec((cb, 1, w), lambda i, b: (i, 0, 0)),
                  const(fwd1), const(inv1), const(f2), const(twr), const(twi)],
        out_specs=sig,
        out_shape=jax.ShapeDtypeStruct((c, rows, w), F32),
        compiler_params=_params(("parallel", "parallel")),
        name="long_conv",
    )(ut3, kr, ki, skip3, fwd1, inv1, f2, twr, twi)


def _gate_kernel(y_ref, x0_ref, o_ref):
    rows, c = x0_ref.shape
    for j in range(rows // LANES):
        for cb in range(c // LANES):
            r, l = slice(j * LANES, (j + 1) * LANES), slice(cb * LANES, (cb + 1) * LANES)
            o_ref[r, l] = (x0_ref[r, l] * y_ref[l, j, :].T).astype(BF16)


def _gate(y3, x0, tl=1024):
    t, c = x0.shape
    return pl.pallas_call(
        _gate_kernel,
        grid=(t // tl,),
        in_specs=[pl.BlockSpec((c, tl // LANES, LANES), lambda i: (0, i, 0)),
                  pl.BlockSpec((tl, c), lambda i: (i, 0))],
        out_specs=pl.BlockSpec((tl, c), lambda i: (i, 0)),
        out_shape=jax.ShapeDtypeStruct((t, c), BF16),
        compiler_params=_params(("parallel",)),
        name="hyena_gate",
    )(y3, x0)


def _hyena_long_conv(ut3, x0, filters, skip, batch, seq_len):
    tables = _dft_tables(seq_len)
    kr, ki = _filter_spectrum(filters, tables)
    skip3 = jnp.broadcast_to(skip[:, None, None], (HY_WIDTH, 1, LANES))
    return _gate(_longconv(ut3, kr, ki, skip3, tables, batch), x0)


def _pipelined(units, scores, softmax, values):
    s_next = scores(units[0])
    pending = None
    for i, unit in enumerate(units):
        s_cur = s_next
        if i + 1 < len(units):
            s_next = scores(units[i + 1])
        if pending is not None:
            values(*pending)
        pending = (unit,) + softmax(unit, s_cur)
    values(*pending)


def _attn_kernel(q1_ref, q2_ref, k_ref, vt_ref, lam_ref, sg_ref, o_ref,
                 m1, a1, m2, a2, *, chunk):
    ki = pl.program_id(3)
    q_refs = (q1_ref, q2_ref)
    m_refs = (m1, m2)
    a_refs = (a1, a2)
    n_chunks = k_ref.shape[0] // chunk
    units = [(c, comp) for c in range(n_chunks) for comp in (0, 1)]

    def scores(unit):
        c, comp = unit
        return jnp.dot(k_ref[c * chunk:(c + 1) * chunk, :], q_refs[comp][...],
                       preferred_element_type=F32)

    def value_dot(unit, p):
        c = unit[0]
        return jnp.dot(vt_ref[:, c * chunk:(c + 1) * chunk], p, preferred_element_type=F32)

    @pl.when(ki == 0)
    def _():
        for comp in (0, 1):
            head = jnp.dot(k_ref[0:2 * SUBLANES, :], q_refs[comp][...], preferred_element_type=F32)
            m_refs[comp][...] = jnp.max(head, axis=0, keepdims=True)
            a_refs[comp][...] = jnp.zeros(a_refs[comp].shape, F32)

    m_start = [m1[...], m2[...]]
    step_max = [None, None]
    step_acc = [None, None]

    def fast_softmax(unit, s):
        comp = unit[1]
        cm = jnp.max(s, axis=0, keepdims=True)
        step_max[comp] = cm if step_max[comp] is None else jnp.maximum(step_max[comp], cm)
        return (jnp.exp2(s - m_start[comp]).astype(BF16),)

    def fast_values(unit, p):
        comp = unit[1]
        d = value_dot(unit, p)
        step_acc[comp] = d if step_acc[comp] is None else step_acc[comp] + d

    _pipelined(units, scores, fast_softmax, fast_values)
    excess = jnp.maximum(step_max[0] - m_start[0], step_max[1] - m_start[1])
    fast_ok = jnp.max(excess) <= STABILISER_HEADROOM

    @pl.when(fast_ok)
    def _():
        for comp in (0, 1):
            m_new = jnp.maximum(m_start[comp], step_max[comp])
            a_refs[comp][...] = jnp.exp2(m_start[comp] - m_new) * (a_refs[comp][...] + step_acc[comp])
            m_refs[comp][...] = m_new

    @pl.when(jnp.logical_not(fast_ok))
    def _():
        stats = [[m1[...], a1[...]], [m2[...], a2[...]]]

        def softmax(unit, s):
            m_prev = stats[unit[1]][0]
            m_new = jnp.maximum(m_prev, jnp.max(s, axis=0, keepdims=True))
            stats[unit[1]][0] = m_new
            return jnp.exp2(s - m_new).astype(BF16), jnp.exp2(m_prev - m_new)

        def values(unit, p, alpha):
            comp = unit[1]
            stats[comp][1] = alpha * stats[comp][1] + value_dot(unit, p)

        _pipelined(units, scores, softmax, values)
        for comp in (0, 1):
            m_refs[comp][...] = stats[comp][0]
            a_refs[comp][...] = stats[comp][1]

    @pl.when(ki == pl.num_programs(3) - 1)
    def _():
        lv = lam_ref[...]
        lam = (jnp.exp(jnp.sum(lv[0:1] * lv[1:2], axis=1, keepdims=True))
               - jnp.exp(jnp.sum(lv[2:3] * lv[3:4], axis=1, keepdims=True)) + LAMBDA_INIT)
        d = ATT_V_DIM
        o = (a1[:d, :] / a1[d:d + 1, :] - lam * (a2[:d, :] / a2[d:d + 1, :]))
        inv = lax.rsqrt(jnp.mean(o * o, axis=0, keepdims=True) + SUBLN_EPS)
        g = pltpu.repeat(sg_ref[...], o.shape[1] // LANES, 1)
        o_ref[...] = (o * inv * g * (1.0 - LAMBDA_INIT)).astype(BF16)


def _attention(qt1, qt2, k, vt, lam_vecs, subln_g, batch, seq_len, tq=1024, tk=4096, chunk=512):
    tk = min(tk, seq_len)
    nq, nk = seq_len // tq, seq_len // tk
    q_spec = pl.BlockSpec((LANES, tq), lambda b, h, i, j: (h, b * nq + i))
    sg = jnp.broadcast_to(subln_g[:, None], (ATT_V_DIM, LANES))
    return pl.pallas_call(
        functools.partial(_attn_kernel, chunk=chunk),
        grid=(batch, N_ATT_HEADS, nq, nk),
        in_specs=[q_spec, q_spec,
                  pl.BlockSpec((tk, LANES), lambda b, h, i, j: (b * nk + j, h)),
                  pl.BlockSpec((VT_ROWS, tk), lambda b, h, i, j: (h, b * nk + j)),
                  pl.BlockSpec((4, ATT_HEAD_DIM), lambda b, h, i, j: (0, 0)),
                  pl.BlockSpec((ATT_V_DIM, LANES), lambda b, h, i, j: (0, 0))],
        out_specs=q_spec,
        out_shape=jax.ShapeDtypeStruct((ATT_WIDTH, batch * seq_len), BF16),
        scratch_shapes=[pltpu.VMEM((1, tq), F32), pltpu.VMEM((VT_ROWS, tq), F32)] * 2,
        compiler_params=_params(("parallel", "parallel", "parallel", "arbitrary")),
        name="diff_attention",
    )(qt1, qt2, k, vt, lam_vecs, sg)


def _split_bf16(x):
    hi = x.astype(BF16)
    return hi, (x - hi.astype(F32)).astype(BF16)


def _outproj_kernel(yh_ref, ya_ref, x_ref, wo_ref, g_ref, wqt_ref, kh_ref, kl_ref,
                    x2_ref, xnt_ref, st_ref):
    wo = wo_ref[...]
    x2 = (x_ref[...] + jnp.dot(yh_ref[...], wo[:HY_WIDTH], preferred_element_type=F32)
          + lax.dot_general(ya_ref[...], wo[HY_WIDTH:], (((0,), (0,)), ((), ())),
                            preferred_element_type=F32))
    x2_ref[...] = x2
    xn = _rms(x2, g_ref[...], EPS)
    xnt = xn.T.astype(BF16)
    xnt_ref[...] = xnt
    qt = jnp.dot(wqt_ref[...], xnt, preferred_element_type=F32)
    for hc in range(2 * PEER_HEADS):
        qh, ql = _split_bf16(qt[hc * PEER_HALF:(hc + 1) * PEER_HALF, :])
        kh, kl = kh_ref[hc], kl_ref[hc]
        st_ref[hc * PEER_KEYS:(hc + 1) * PEER_KEYS, :] = (
            jnp.dot(kh, qh, preferred_element_type=F32)
            + jnp.dot(kh, ql, preferred_element_type=F32)
            + jnp.dot(kl, qh, preferred_element_type=F32))


def _outproj(yh, ya, x, wo_bf, g2, wqt_bf, keys_hi, keys_lo, tm=512):
    t = x.shape[0]
    qw = 2 * PEER_HEADS * PEER_HALF
    sw = 2 * PEER_HEADS * PEER_KEYS
    row = lambda i: (i, 0)
    fixed = lambda i: (0, 0)
    fixed3 = lambda i: (0, 0, 0)
    return pl.pallas_call(
        _outproj_kernel,
        grid=(t // tm,),
        in_specs=[pl.BlockSpec((tm, HY_WIDTH), row),
                  pl.BlockSpec((ATT_WIDTH, tm), lambda i: (0, i)),
                  pl.BlockSpec((tm, D_MODEL), row),
                  pl.BlockSpec((HY_WIDTH + ATT_WIDTH, D_MODEL), fixed),
                  pl.BlockSpec((1, D_MODEL), fixed),
                  pl.BlockSpec((qw, D_MODEL), fixed),
                  pl.BlockSpec((2 * PEER_HEADS, PEER_KEYS, PEER_HALF), fixed3),
                  pl.BlockSpec((2 * PEER_HEADS, PEER_KEYS, PEER_HALF), fixed3)],
        out_specs=[pl.BlockSpec((tm, D_MODEL), row), pl.BlockSpec((D_MODEL, tm), lambda i: (0, i)),
                   pl.BlockSpec((sw, tm), lambda i: (0, i))],
        out_shape=[jax.ShapeDtypeStruct((t, D_MODEL), F32),
                   jax.ShapeDtypeStruct((D_MODEL, t), BF16),
                   jax.ShapeDtypeStruct((sw, t), F32)],
        compiler_params=_params(("parallel",)),
        name="outproj_peer_query",
    )(yh, ya, x, wo_bf, g2, wqt_bf, keys_hi, keys_lo)


def _extract_max(pieces, iotas, total, exact):
    m = pieces[0]
    for p in pieces[1:]:
        m = jnp.maximum(m, p)
    m = jnp.max(m, axis=0, keepdims=True)
    hot = [p == m for p in pieces]
    if exact:
        cand = [jnp.where(h, io, total) for h, io in zip(hot, iotas)]
        idx = cand[0]
        for c in cand[1:]:
            idx = jnp.minimum(idx, c)
        idx = jnp.min(idx, axis=0, keepdims=True)
        hot = [io == idx for io in iotas]
    return m, hot, [jnp.where(h, NEG_INF, p) for h, p in zip(hot, pieces)]


def _topk_rank(s, exact, want_rank):
    rank = jnp.full(s.shape, float(PEER_TOPK), F32) if want_rank else None
    iota = [lax.broadcasted_iota(jnp.int32, s.shape, 0)]
    vals = []
    for r in range(PEER_TOPK):
        m, hot, (s,) = _extract_max([s], iota, s.shape[0], exact)
        vals.append(m)
        if want_rank:
            rank = jnp.where(hot[0], float(r), rank)
    return vals, rank


_PAIRS = [(k0, k1) for k0 in range(PEER_TOPK) for k1 in range(PEER_TOPK // (k0 + 1))]
_PAIR_ROWS = -(-len(_PAIRS) // SUBLANES) * SUBLANES


def _rows_to_tiles(rows, width):
    sub = lax.broadcasted_iota(jnp.int32, (SUBLANES, width), 0)
    tiles = []
    for t0 in range(0, len(rows), SUBLANES):
        tile = jnp.full((SUBLANES, width), NEG_INF, F32)
        for r in range(SUBLANES):
            if rows[t0 + r] is not None:
                tile = jnp.where(sub == r, rows[t0 + r], tile)
        tiles.append(tile)
    return tiles


def _route_tables(s0, s1, exact):
    tt = s0.shape[1]
    v0, rank0 = _topk_rank(s0, exact, want_rank=exact)
    v1, rank1 = _topk_rank(s1, exact, want_rank=True)
    pad = [None] * (_PAIR_ROWS - len(_PAIRS))
    tiles = [a + b for a, b in zip(_rows_to_tiles([v0[k0] for k0, _ in _PAIRS] + pad, tt),
                                   _rows_to_tiles([v1[k1] for _, k1 in _PAIRS] + pad, tt))]
    iotas = [lax.broadcasted_iota(jnp.int32, (SUBLANES, tt), 0) + t0
             for t0 in range(0, _PAIR_ROWS, SUBLANES)]
    cmax = v0[0] + v1[0]
    z = None
    for r in range(PEER_TOPK):
        m, _, tiles = _extract_max(tiles, iotas, _PAIR_ROWS, exact)
        z = jnp.exp(m - cmax) if z is None else z + jnp.exp(m - cmax)
    cnt = jnp.zeros(s0.shape, F32)
    total = None
    for k0 in range(PEER_TOPK):
        rows = [r for r, (a, _) in enumerate(_PAIRS) if a == k0]
        taken = None
        for ti in sorted({r // SUBLANES for r in rows}):
            inside = (iotas[ti] >= rows[0]) & (iotas[ti] <= rows[-1])
            part = jnp.sum(jnp.where(inside & (tiles[ti] == NEG_INF), 1.0, 0.0),
                           axis=0, keepdims=True)
            taken = part if taken is None else taken + part
        total = taken if total is None else total + taken
        cnt = jnp.where((rank0 == float(k0)) if exact else (s0 == v0[k0]), taken, cnt)
    k = float(PEER_TOPK)
    count = lambda mask: jnp.sum(jnp.where(mask, 1.0, 0.0), axis=0, keepdims=True)
    top0 = (rank0 < k) if exact else (s0 >= v0[PEER_TOPK - 1])
    ok = (total == k) & (count(top0) == k) & (count(rank1 < k) == k)
    return cnt, jnp.exp(s0 - v0[0]) / z, rank1, jnp.exp(s1 - v1[0]), ok


def _route_kernel(st_ref, cnt_ref, w0_ref, r1_ref, e1_ref):
    def emit(exact):
        cnt, w0, rank1, e1, ok = _route_tables(st_ref[:PEER_KEYS, :], st_ref[PEER_KEYS:, :], exact)
        cnt_ref[0] = cnt
        w0_ref[0] = w0 * GELU_OUT_SCALE
        r1_ref[0] = rank1.astype(BF16)
        e1_ref[0] = e1.astype(BF16)
        return ok

    ok = emit(exact=False)
    clean = jnp.min(jnp.where(ok, 1, 0))

    @pl.when(clean == 0)
    def _():
        emit(exact=True)


def _route(st, tt=1024):
    sw, t = st.shape
    o_spec = pl.BlockSpec((1, PEER_KEYS, tt), lambda i, h: (h, 0, i))
    tab = lambda dt: jax.ShapeDtypeStruct((PEER_HEADS, PEER_KEYS, t), dt)
    return pl.pallas_call(
        _route_kernel,
        grid=(t // tt, PEER_HEADS),
        in_specs=[pl.BlockSpec((2 * PEER_KEYS, tt), lambda i, h: (h, i))],
        out_specs=[o_spec] * 4,
        out_shape=[tab(F32), tab(F32), tab(BF16), tab(BF16)],
        compiler_params=_params(("parallel", "parallel")),
        name="peer_route",
    )(st)


GELU_IN_SCALE = 2.0 ** -0.5
GELU_OUT_SCALE = 0.5 * 2.0 ** 0.5


def _gelu_core(y):
    return y * (1.0 + lax.erf(y))


def _peer_kernel(xnt_ref, u_ref, vt_ref, cnt_ref, w0_ref, r1_ref, e1_ref, x2_ref, gf_ref,
                 y_ref, acc_ref, *, rows_per_step, rows_per_unit):
    e = pl.program_id(1)

    @pl.when(e == 0)
    def _():
        acc_ref[...] = jnp.zeros(acc_ref.shape, F32)

    tt = xnt_ref.shape[1]
    unit = rows_per_unit * PEER_KEYS
    n_units = rows_per_step // rows_per_unit

    def hidden(s):
        return jnp.dot(u_ref[s * unit:(s + 1) * unit, :], xnt_ref[...],
                       preferred_element_type=F32)

    def activation(s, hid):
        gates = []
        for ii in range(s * rows_per_unit, (s + 1) * rows_per_unit):
            g = None
            for h in range(PEER_HEADS):
                cnt_row = jnp.broadcast_to(cnt_ref[h, ii:ii + 1, :], (PEER_KEYS, tt)).astype(BF16)
                w_row = jnp.broadcast_to(w0_ref[h, ii:ii + 1, :], (PEER_KEYS, tt)).astype(BF16)
                term = jnp.where(r1_ref[h] < cnt_row, e1_ref[h] * w_row, jnp.zeros((), BF16))
                g = term if g is None else g + term
            gates.append(g)
        return _gelu_core(hid).astype(BF16) * jnp.concatenate(gates, axis=0)

    def mix(s, act):
        return jnp.dot(vt_ref[:, s * unit:(s + 1) * unit], act, preferred_element_type=F32)

    acc = acc_ref[...]
    h_next = hidden(0)
    pending = None
    for s in range(n_units):
        h_cur = h_next
        if s + 1 < n_units:
            h_next = hidden(s + 1)
        if pending is not None:
            acc = acc + mix(*pending)
        pending = (s, activation(s, h_cur))
    acc_ref[...] = acc + mix(*pending)

    @pl.when(e == pl.num_programs(1) - 1)
    def _():
        x3 = x2_ref[...] + acc_ref[...].T
        y_ref[...] = _rms(x3, gf_ref[...], EPS)


def _peer(xnt, u_bf, vt_bf, cnt, w0, r1, e1, x2, gf, tt=512, rows_per_step=16, rows_per_unit=2):
    t = xnt.shape[1]
    te = rows_per_step * PEER_KEYS
    tok = lambda i, e: (i, 0)
    tab_i = pl.BlockSpec((PEER_HEADS, rows_per_step, tt), lambda i, e: (0, e, i))
    tab_j = pl.BlockSpec((PEER_HEADS, PEER_KEYS, tt), lambda i, e: (0, 0, i))
    return pl.pallas_call(
        functools.partial(_peer_kernel, rows_per_step=rows_per_step,
                          rows_per_unit=rows_per_unit),
        grid=(t // tt, PEER_EXPERTS // te),
        in_specs=[pl.BlockSpec((D_MODEL, tt), lambda i, e: (0, i)),
                  pl.BlockSpec((te, D_MODEL), lambda i, e: (e, 0)),
                  pl.BlockSpec((D_MODEL, te), lambda i, e: (0, e)),
                  tab_i, tab_i, tab_j, tab_j,
                  pl.BlockSpec((tt, D_MODEL), tok),
                  pl.BlockSpec((1, D_MODEL), lambda i, e: (0, 0))],
        out_specs=pl.BlockSpec((tt, D_MODEL), tok),
        out_shape=jax.ShapeDtypeStruct((t, D_MODEL), F32),
        scratch_shapes=[pltpu.VMEM((D_MODEL, tt), F32)],
        compiler_params=_params(("parallel", "arbitrary")),
        name="peer_dense",
    )(xnt, u_bf, vt_bf, cnt, w0, r1, e1, x2, gf)


def _rope_tables(seq_len):
    f32 = np.float32
    inv = f32(ROPE_THETA) ** (-np.arange(0, ROT_DIM, 2, dtype=f32) / f32(ROT_DIM))
    ang = (np.arange(seq_len, dtype=f32)[:, None] * inv[None, :]).astype(np.float64)
    cos, sin = np.cos(ang), np.sin(ang)
    half = ROT_DIM // 2
    comp = np.zeros((3, seq_len, ATT_HEAD_DIM))
    comp[0, :, :half], comp[0, :, half:ROT_DIM], comp[0, :, ROT_DIM:] = cos, cos, 1.0
    comp[1, :, :half] = -sin
    comp[2, :, half:ROT_DIM] = sin
    both = np.concatenate([comp, comp], axis=2).astype(f32)
    return jnp.asarray(both[0]), jnp.asarray(both[1]), jnp.asarray(both[2])


def _trunk(x, p):
    batch, seq_len, _ = x.shape
    xf = x.reshape(batch * seq_len, D_MODEL)
    rope_c, rope_sa, rope_sb = _rope_tables(seq_len)
    zh, qt1, qt2, k, vt = _inproj(xf, p["norm1_g"], p["w_in"], rope_c, rope_sa, rope_sb, seq_len)
    u, x0 = _shortconv(zh, p["hy_conv_w"], p["hy_conv_b"], seq_len)
    filters = _hyena_filters(seq_len, *p["filter"])
    yh = _hyena_long_conv(u, x0, filters, p["hy_skip"], batch, seq_len)
    ya = _attention(qt1, qt2, k, vt, p["lam_vecs"], p["subln_g"], batch, seq_len)
    x2, xnt, st = _outproj(yh, ya, xf, p["w_out"], p["norm2_g"], p["wqt"],
                          p["keys_hi"], p["keys_lo"])
    cnt, w0, r1, e1 = _route(st)
    y = _peer(xnt, p["peer_u"], p["peer_vt"], cnt, w0, r1, e1, x2, p["normf_g"])
    return y.reshape(batch, seq_len, D_MODEL)


def kernel(x_prompt, x_sample, norm1_g, w_in, hy_conv_w, hy_conv_b, hf_w1, hf_b1, hf_freq1,
           hf_w2, hf_b2, hf_freq2, hf_w3, hf_b3, hy_skip, lam_q1, lam_k1, lam_q2, lam_k2,
           subln_g, w_out, norm2_g, peer_wq, peer_keys, peer_u, peer_v, normf_g):
    l = 0
    keys = peer_keys[l].reshape(2 * PEER_HEADS, PEER_KEYS, PEER_HALF)
    keys_hi = keys.astype(BF16)
    p = {
        "norm1_g": norm1_g[l][None, :],
        "w_in": w_in[l].astype(BF16),
        "hy_conv_w": hy_conv_w[l],
        "hy_conv_b": hy_conv_b[l][None, :],
        "filter": (hf_w1[l], hf_b1[l], hf_freq1[l], hf_w2[l], hf_b2[l], hf_freq2[l],
                   hf_w3[l], hf_b3[l]),
        "hy_skip": hy_skip[l],
        "lam_vecs": jnp.stack([lam_q1[l], lam_k1[l], lam_q2[l], lam_k2[l]]),
        "subln_g": subln_g[l],
        "w_out": w_out[l].astype(BF16),
        "norm2_g": norm2_g[l][None, :],
        "wqt": peer_wq[l].astype(BF16).T,
        "keys_hi": keys_hi,
        "keys_lo": (keys - keys_hi.astype(F32)).astype(BF16),
        "peer_u": (peer_u[l] * GELU_IN_SCALE).astype(BF16),
        "peer_vt": peer_v[l].astype(BF16).T,
        "normf_g": normf_g[None, :],
    }
    return (_trunk(x_prompt, p), _trunk(x_sample, p))
```

```python
import functools
import math

import jax
import jax.numpy as jnp
import numpy as np
from jax import lax
from jax.experimental import pallas as pl
from jax.experimental.pallas import tpu as pltpu

F32 = jnp.float32
BF16 = jnp.bfloat16

D_MODEL = 1024
HY_WIDTH = 512
ATT_WIDTH = 512
N_ATT_HEADS = 4
ATT_HEAD_DIM = 64
ATT_V_DIM = 2 * ATT_HEAD_DIM
ROT_DIM = ATT_HEAD_DIM // 4
ROPE_THETA = 500000.0
IN_WIDTH = 3 * HY_WIDTH + 3 * ATT_WIDTH
FILTER_BANDS = 8
FILTER_EMB = 1 + 2 * FILTER_BANDS
FILTER_HIDDEN = 64
FAST_DECAY_PCT = 0.3
SLOW_DECAY_PCT = 1.5
DECAY_TARGET = 1e-2
PEER_HEADS = 8
PEER_KEYS = 128
PEER_EXPERTS = PEER_KEYS * PEER_KEYS
PEER_TOPK = 16
PEER_HALF = 128
EPS = 1e-6
SUBLN_EPS = 1e-5
LAMBDA_INIT = 0.8 - 0.6 * math.exp(-0.3 * 0)

LANES = 128
SUBLANES = 8
VMEM_LIMIT = 56 * 1024 * 1024
FFT_N2 = 128
VT_ROWS = ATT_V_DIM + 2 * SUBLANES
STABILISER_HEADROOM = 64.0
NEG_INF = float("-inf")


def _params(semantics, vmem=VMEM_LIMIT):
    return pltpu.CompilerParams(dimension_semantics=semantics, vmem_limit_bytes=vmem)


def _rms(x, g, eps):
    return x * lax.rsqrt(jnp.mean(x * x, axis=-1, keepdims=True) + eps) * g


def _dot_nt(a, b):
    return lax.dot_general(a, b, (((1,), (1,)), ((), ())), preferred_element_type=F32)


def _inproj_kernel(x_ref, g_ref, w_ref, c_ref, sa_ref, sb_ref,
                   zh_ref, qt1_ref, qt2_ref, k_ref, vt_ref):
    h = _rms(x_ref[...], g_ref[...], EPS).astype(BF16)
    z = jnp.dot(h, w_ref[...], preferred_element_type=F32)
    hy3 = 3 * HY_WIDTH
    zh_ref[...] = z[:, :hy3]
    c, sa, sb = c_ref[...], sa_ref[...], sb_ref[...]
    scale = ATT_HEAD_DIM ** -0.5 * math.log2(math.e)

    def rope(xb):
        return (xb * c + pltpu.roll(xb, LANES - ROT_DIM // 2, 1) * sa
                + pltpu.roll(xb, ROT_DIM // 2, 1) * sb)

    first = lax.broadcasted_iota(jnp.int32, (LANES, z.shape[0]), 0) < ATT_HEAD_DIM
    for hd in range(N_ATT_HEADS):
        lo, hi = hd * LANES, (hd + 1) * LANES
        qt = (rope(z[:, hy3 + lo:hy3 + hi]) * scale).T
        qt1_ref[lo:hi, :] = jnp.where(first, qt, 0.0).astype(BF16)
        qt2_ref[lo:hi, :] = jnp.where(first, 0.0, qt).astype(BF16)
        k_ref[:, lo:hi] = rope(z[:, hy3 + ATT_WIDTH + lo:hy3 + ATT_WIDTH + hi]).astype(BF16)
        vlo = hd * VT_ROWS
        vt_ref[vlo:vlo + ATT_V_DIM, :] = (
            z[:, hy3 + 2 * ATT_WIDTH + lo:hy3 + 2 * ATT_WIDTH + hi].T.astype(BF16))
        vt_ref[vlo + ATT_V_DIM:vlo + VT_ROWS, :] = jnp.ones(
            (VT_ROWS - ATT_V_DIM, z.shape[0]), BF16)


def _inproj(x, g, w_bf, rope_c, rope_sa, rope_sb, seq_len, tm=512):
    t = x.shape[0]
    nseq = seq_len // tm
    row = lambda i: (i, 0)
    col = lambda i: (0, i)
    fixed = lambda i: (0, 0)
    pos = lambda i: (i % nseq, 0)
    hy3 = 3 * HY_WIDTH
    tr_spec = pl.BlockSpec((ATT_WIDTH, tm), col)
    tr_shape = jax.ShapeDtypeStruct((ATT_WIDTH, t), BF16)
    return pl.pallas_call(
        _inproj_kernel,
        grid=(t // tm,),
        in_specs=[pl.BlockSpec((tm, D_MODEL), row),
                  pl.BlockSpec((1, D_MODEL), fixed),
                  pl.BlockSpec((D_MODEL, IN_WIDTH), fixed),
                  pl.BlockSpec((tm, LANES), pos),
                  pl.BlockSpec((tm, LANES), pos),
                  pl.BlockSpec((tm, LANES), pos)],
        out_specs=[pl.BlockSpec((tm, hy3), row), tr_spec, tr_spec,
                   pl.BlockSpec((tm, ATT_WIDTH), row),
                   pl.BlockSpec((N_ATT_HEADS * VT_ROWS, tm), col)],
        out_shape=[jax.ShapeDtypeStruct((t, hy3), F32), tr_shape, tr_shape,
                   jax.ShapeDtypeStruct((t, ATT_WIDTH), BF16),
                   jax.ShapeDtypeStruct((N_ATT_HEADS * VT_ROWS, t), BF16)],
        compiler_params=_params(("parallel",)),
        name="inproj",
    )(x, g, w_bf, rope_c, rope_sa, rope_sb)


def _store_time_blocks(dst_ref, x, lead=()):
    rows, c = x.shape
    for j in range(rows // LANES):
        for cb in range(c // LANES):
            dst_ref[lead + (j, slice(cb * LANES, (cb + 1) * LANES), slice(None))] = (
                x[j * LANES:(j + 1) * LANES, cb * LANES:(cb + 1) * LANES].T)


def _shortconv_kernel(z_ref, zp_ref, zn_ref, w_ref, b_ref, ut_ref, x0_ref, *, blocks_per_seq):
    i = pl.program_id(0)
    z = z_ref[...]
    tl = z.shape[0]
    row = lax.broadcasted_iota(jnp.int32, z.shape, 0)
    at_start = (i % blocks_per_seq) == 0
    at_end = (i % blocks_per_seq) == blocks_per_seq - 1
    prev_row = jnp.where(at_start, 0.0, zp_ref[SUBLANES - 1:SUBLANES, :])
    next_row = jnp.where(at_end, 0.0, zn_ref[0:1, :])
    zm = jnp.where(row == 0, prev_row, pltpu.roll(z, 1, 0))
    zp = jnp.where(row == tl - 1, next_row, pltpu.roll(z, tl - 1, 0))
    w = w_ref[...]
    conv = zm * w[0:1] + z * w[1:2] + zp * w[2:3] + b_ref[...]
    x0_ref[...] = conv[:, :HY_WIDTH]
    _store_time_blocks(ut_ref, conv[:, 2 * HY_WIDTH:] * conv[:, HY_WIDTH:2 * HY_WIDTH])


def _shortconv(zh, w, b, seq_len, tl=1024):
    t, hy3 = zh.shape
    per8 = tl // SUBLANES
    last8 = t // SUBLANES - 1
    return pl.pallas_call(
        functools.partial(_shortconv_kernel, blocks_per_seq=seq_len // tl),
        grid=(t // tl,),
        in_specs=[pl.BlockSpec((tl, hy3), lambda i: (i, 0)),
                  pl.BlockSpec((SUBLANES, hy3), lambda i: (jnp.maximum(i * per8 - 1, 0), 0)),
                  pl.BlockSpec((SUBLANES, hy3), lambda i: (jnp.minimum((i + 1) * per8, last8), 0)),
                  pl.BlockSpec((3, hy3), lambda i: (0, 0)),
                  pl.BlockSpec((1, hy3), lambda i: (0, 0))],
        out_specs=[pl.BlockSpec((tl // LANES, HY_WIDTH, LANES), lambda i: (i, 0, 0)),
                   pl.BlockSpec((tl, HY_WIDTH), lambda i: (i, 0))],
        out_shape=[jax.ShapeDtypeStruct((t // LANES, HY_WIDTH, LANES), F32),
                   jax.ShapeDtypeStruct((t, HY_WIDTH), F32)],
        compiler_params=_params(("parallel",)),
        name="shortconv",
    )(zh, zh, zh, w, b)


def _filter_kernel(f_ref, w1_ref, b1_ref, f1_ref, w2_ref, b2_ref, f2_ref, w3_ref, b3_ref,
                   dl_ref, h_ref):
    i = pl.program_id(0)
    hp = lax.Precision.HIGHEST
    feats = f_ref[...]
    tl = feats.shape[0]
    h = jnp.sin(f1_ref[...] * (jnp.dot(feats, w1_ref[...], precision=hp,
                                       preferred_element_type=F32) + b1_ref[...]))
    h = jnp.sin(f2_ref[...] * (jnp.dot(h, w2_ref[...], precision=hp,
                                       preferred_element_type=F32) + b2_ref[...]))
    h3 = jnp.dot(h, w3_ref[...], precision=hp, preferred_element_type=F32) + b3_ref[...]
    decay = jnp.exp(-feats[:, 0:1] * dl_ref[...])
    _store_time_blocks(h_ref, h3[:, :HY_WIDTH] * decay, lead=(0,))
    row = lax.broadcasted_iota(jnp.int32, (tl, HY_WIDTH), 0) + i * tl
    _store_time_blocks(h_ref, jnp.where(row == 0, 0.0, h3[:, HY_WIDTH:] * decay), lead=(1,))


def _pad2(a, rows, cols):
    return jnp.pad(a, ((0, rows - a.shape[0]), (0, cols - a.shape[1])))


def _hyena_filters(seq_len, w1, b1, f1, w2, b2, f2, w3, b3, tl=1024):
    f32 = np.float32
    t = np.linspace(0.0, 1.0, seq_len).astype(f32)[:, None]
    w = f32(2.0 * math.pi) * np.arange(seq_len, dtype=f32) / f32(seq_len)
    bands = np.linspace(1e-4, FILTER_BANDS - 1, FILTER_BANDS).astype(f32)
    ang = (w[:, None] * bands[None, :]).astype(np.float64)
    feats_np = np.zeros((seq_len, LANES), f32)
    feats_np[:, :FILTER_EMB] = np.concatenate([t, np.cos(ang), -np.sin(ang)], axis=-1)
    feats = jnp.asarray(feats_np)
    max_decay = math.log(DECAY_TARGET) / FAST_DECAY_PCT
    min_decay = math.log(DECAY_TARGET) / SLOW_DECAY_PCT
    deltas = jnp.abs(jnp.linspace(min_decay, max_decay, HY_WIDTH, dtype=F32))[None, :]
    w1p = _pad2(w1, LANES, LANES)
    w2p = _pad2(w2, LANES, LANES)
    w3p = _pad2(w3, LANES, 2 * HY_WIDTH)
    pad1 = lambda v: _pad2(v[None, :], 1, LANES)
    fixed = lambda i: (0, 0)
    vec = pl.BlockSpec((1, LANES), fixed)
    mat = pl.BlockSpec((LANES, LANES), fixed)
    return pl.pallas_call(
        _filter_kernel,
        grid=(seq_len // tl,),
        in_specs=[pl.BlockSpec((tl, LANES), lambda i: (i, 0)), mat, vec, vec, mat, vec, vec,
                  pl.BlockSpec((LANES, 2 * HY_WIDTH), fixed),
                  pl.BlockSpec((1, 2 * HY_WIDTH), fixed),
                  pl.BlockSpec((1, HY_WIDTH), fixed)],
        out_specs=pl.BlockSpec((2, tl // LANES, HY_WIDTH, LANES), lambda i: (0, i, 0, 0)),
        out_shape=jax.ShapeDtypeStruct((2, seq_len // LANES, HY_WIDTH, LANES), F32),
        compiler_params=_params(("parallel",)),
        name="hyena_filter",
    )(feats, w1p, pad1(b1), pad1(f1), w2p, pad1(b2), pad1(f2), w3p, b3[None, :], deltas)


def _dft_tables(seq_len):
    n = 2 * seq_len
    n2 = FFT_N2
    n1 = n // n2
    n1h = n1 // 2
    a1 = (2.0 * np.pi / n1) * np.outer(np.arange(n1), np.arange(n1))
    f1r, f1i = np.cos(a1), -np.sin(a1)
    fwd1 = np.concatenate([f1r[:, :n1h], f1i[:, :n1h]], axis=0)
    inv1 = np.concatenate([f1r[:n1h, :], f1i[:n1h, :]], axis=1) / n
    a2 = (2.0 * np.pi / n2) * np.outer(np.arange(n2), np.arange(n2))
    f2 = np.concatenate([np.cos(a2), -np.sin(a2)], axis=1)
    at = (2.0 * np.pi / n) * np.outer(np.arange(n1), np.arange(n2))
    return (jnp.asarray(fwd1, BF16), jnp.asarray(inv1, BF16), jnp.asarray(f2, BF16),
            jnp.asarray(np.cos(at), F32), jnp.asarray(-np.sin(at), F32))


def _forward_spectra(tiles, fwd1, twr, twi, f2):
    n1, w = twr.shape
    tr, ti = _tile_lanes(twr, 2 * w), _tile_lanes(twi, 2 * w)
    firsts = [jnp.dot(fwd1, jnp.concatenate([xa, xb], axis=1).astype(BF16),
                      preferred_element_type=F32) for xa, xb in tiles]
    twiddled = []
    for a in firsts:
        ar, ai = a[:n1], a[n1:]
        zr, zi = ar * tr - ai * ti, ar * ti + ai * tr
        twiddled.append(jnp.concatenate([zr[:, :w], zi[:, :w], zr[:, w:], zi[:, w:]],
                                        axis=0).astype(BF16))
    seconds = [jnp.dot(z, f2, preferred_element_type=F32) for z in twiddled]

    def spectrum(q):
        qr, qi = q[:n1], q[n1:]
        return qr[:, :w] - qi[:, w:], qr[:, w:] + qi[:, :w]

    return [(spectrum(p[:2 * n1]), spectrum(p[2 * n1:])) for p in seconds]


def _inverse_signals(spectra, inv1, twr, twi, f2):
    n1, w = twr.shape
    firsts = [jnp.dot(jnp.concatenate([wa[0], wa[1], wb[0], wb[1]], axis=0).astype(BF16), f2,
                      preferred_element_type=F32) for wa, wb in spectra]

    def untwiddle(r):
        rr, ri = r[:n1], r[n1:]
        vr, vi = rr[:, :w] + ri[:, w:], ri[:, :w] - rr[:, w:]
        return vr * twr + vi * twi, vi * twr - vr * twi

    stacked = []
    for q in firsts:
        (bra, bia), (brb, bib) = untwiddle(q[:2 * n1]), untwiddle(q[2 * n1:])
        stacked.append(jnp.concatenate([jnp.concatenate([bra, brb], axis=1),
                                        jnp.concatenate([bia, bib], axis=1)],
                                       axis=0).astype(BF16))
    outs = [jnp.dot(inv1, r, preferred_element_type=F32) for r in stacked]
    return [(y[:, :w], y[:, w:]) for y in outs]


def _tile_lanes(x, width):
    return jnp.concatenate([x] * (width // x.shape[1]), axis=1)


def _spectrum_kernel(h_ref, fwd1_ref, f2_ref, twr_ref, twi_ref, kr_ref, ki_ref):
    fwd1, f2, twr, twi = fwd1_ref[...], f2_ref[...], twr_ref[...], twi_ref[...]
    cs = range(h_ref.shape[2])
    out = _forward_spectra([(h_ref[0, :, c, :], h_ref[1, :, c, :]) for c in cs], fwd1, twr, twi, f2)
    for c, ((fr, fi), (gr, gi)) in zip(cs, out):
        kr_ref[c] = (fr + gr).astype(BF16)
        ki_ref[c] = (fi - gi).astype(BF16)


def _filter_spectrum(h3, tables, cb=8):
    fwd1, _, f2, twr, twi = tables
    _, n1h, c, w = h3.shape
    n1 = 2 * n1h
    const = lambda a: pl.BlockSpec(a.shape, lambda i: (0,) * a.ndim)
    o_spec = pl.BlockSpec((cb, n1, w), lambda i: (i, 0, 0))
    return pl.pallas_call(
        _spectrum_kernel,
        grid=(c // cb,),
        in_specs=[pl.BlockSpec((2, n1h, cb, w), lambda i: (0, 0, i, 0)),
                  const(fwd1), const(f2), const(twr), const(twi)],
        out_specs=[o_spec, o_spec],
        out_shape=[jax.ShapeDtypeStruct((c, n1, w), BF16)] * 2,
        compiler_params=_params(("parallel",)),
        name="filter_spectrum",
    )(h3, fwd1, f2, twr, twi)


def _longconv_kernel(u_ref, kr_ref, ki_ref, skip_ref, fwd1_ref, inv1_ref, f2_ref, twr_ref,
                     twi_ref, y_ref):
    fwd1, inv1, f2 = fwd1_ref[...], inv1_ref[...], f2_ref[...]
    twr, twi = twr_ref[...], twi_ref[...]

    def filtered(c, spec):
        kr, ki = kr_ref[c].astype(F32), ki_ref[c].astype(F32)
        return spec[0] * kr - spec[1] * ki, spec[0] * ki + spec[1] * kr

    cs = range(0, u_ref.shape[1], 2)
    xs = [(u_ref[:, c, :], u_ref[:, c + 1, :]) for c in cs]
    spectra = _forward_spectra(xs, fwd1, twr, twi, f2)
    prods = [(filtered(c, sa), filtered(c + 1, sb)) for c, (sa, sb) in zip(cs, spectra)]
    for c, (xa, xb), (ya, yb) in zip(cs, xs, _inverse_signals(prods, inv1, twr, twi, f2)):
        y_ref[c] = ya + skip_ref[c] * xa
        y_ref[c + 1] = yb + skip_ref[c + 1] * xb


def _longconv(ut3, kr, ki, skip3, tables, batch, cb=16):
    fwd1, inv1, f2, twr, twi = tables
    rows, c, w = ut3.shape
    n1h = rows // batch
    n1 = 2 * n1h
    const = lambda a: pl.BlockSpec(a.shape, lambda i, b: (0,) * a.ndim)
    spec = pl.BlockSpec((cb, n1, w), lambda i, b: (i, 0, 0))
    return pl.pallas_call(
        _longconv_kernel,
        grid=(c // cb, batch),
        in_specs=[pl.BlockSpec((n1h, cb, w), lambda i, b: (b, i, 0)), spec, spec,
                  pl.BlockSpec((cb, 1, w), lambda i, b: (i, 0, 0)),
                  const(fwd1), const(inv1), const(f2), const(twr), const(twi)],
        out_specs=pl.BlockSpec((cb, n1h, w), lambda i, b: (i, b, 0)),
        out_shape=jax.ShapeDtypeStruct((c, rows, w), F32),
        compiler_params=_params(("parallel", "parallel")),
        name="long_conv",
    )(ut3, kr, ki, skip3, fwd1, inv1, f2, twr, twi)


def _gate_kernel(y_ref, x0_ref, o_ref):
    rows, c = x0_ref.shape
    for j in range(rows // LANES):
        for cb in range(c // LANES):
            r, l = slice(j * LANES, (j + 1) * LANES), slice(cb * LANES, (cb + 1) * LANES)
            o_ref[r, l] = (x0_ref[r, l] * y_ref[l, j, :].T).astype(BF16)


def _gate(y3, x0, tl=1024):
    t, c = x0.shape
    return pl.pallas_call(
        _gate_kernel,
        grid=(t // tl,),
        in_specs=[pl.BlockSpec((c, tl // LANES, LANES), lambda i: (0, i, 0)),
                  pl.BlockSpec((tl, c), lambda i: (i, 0))],
        out_specs=pl.BlockSpec((tl, c), lambda i: (i, 0)),
        out_shape=jax.ShapeDtypeStruct((t, c), BF16),
        compiler_params=_params(("parallel",)),
        name="hyena_gate",
    )(y3, x0)


def _hyena_long_conv(ut3, x0, filters, skip, batch, seq_len):
    tables = _dft_tables(seq_len)
    kr, ki = _filter_spectrum(filters, tables)
    skip3 = jnp.broadcast_to(skip[:, None, None], (HY_WIDTH, 1, LANES))
    return _gate(_longconv(ut3, kr, ki, skip3, tables, batch), x0)


def _pipelined(units, scores, softmax, values):
    s_next = scores(units[0])
    pending = None
    for i, unit in enumerate(units):
        s_cur = s_next
        if i + 1 < len(units):
            s_next = scores(units[i + 1])
        if pending is not None:
            values(*pending)
        pending = (unit,) + softmax(unit, s_cur)
    values(*pending)


def _attn_kernel(q1_ref, q2_ref, k_ref, vt_ref, lam_ref, sg_ref, o_ref,
                 m1, a1, m2, a2, *, chunk):
    ki = pl.program_id(3)
    q_refs = (q1_ref, q2_ref)
    m_refs = (m1, m2)
    a_refs = (a1, a2)
    n_chunks = k_ref.shape[0] // chunk
    units = [(c, comp) for c in range(n_chunks) for comp in (0, 1)]

    def scores(unit):
        c, comp = unit
        return jnp.dot(k_ref[c * chunk:(c + 1) * chunk, :], q_refs[comp][...],
                       preferred_element_type=F32)

    def value_dot(unit, p):
        c = unit[0]
        return jnp.dot(vt_ref[:, c * chunk:(c + 1) * chunk], p, preferred_element_type=F32)

    @pl.when(ki == 0)
    def _():
        for comp in (0, 1):
            head = jnp.dot(k_ref[0:2 * SUBLANES, :], q_refs[comp][...], preferred_element_type=F32)
            m_refs[comp][...] = jnp.max(head, axis=0, keepdims=True)
            a_refs[comp][...] = jnp.zeros(a_refs[comp].shape, F32)

    m_start = [m1[...], m2[...]]
    step_max = [None, None]
    step_acc = [None, None]

    def fast_softmax(unit, s):
        comp = unit[1]
        cm = jnp.max(s, axis=0, keepdims=True)
        step_max[comp] = cm if step_max[comp] is None else jnp.maximum(step_max[comp], cm)
        return (jnp.exp2(s - m_start[comp]).astype(BF16),)

    def fast_values(unit, p):
        comp = unit[1]
        d = value_dot(unit, p)
        step_acc[comp] = d if step_acc[comp] is None else step_acc[comp] + d

    _pipelined(units, scores, fast_softmax, fast_values)
    excess = jnp.maximum(step_max[0] - m_start[0], step_max[1] - m_start[1])
    fast_ok = jnp.max(excess) <= STABILISER_HEADROOM

    @pl.when(fast_ok)
    def _():
        for comp in (0, 1):
            m_new = jnp.maximum(m_start[comp], step_max[comp])
            a_refs[comp][...] = jnp.exp2(m_start[comp] - m_new) * (a_refs[comp][...] + step_acc[comp])
            m_refs[comp][...] = m_new

    @pl.when(jnp.logical_not(fast_ok))
    def _():
        stats = [[m1[...], a1[...]], [m2[...], a2[...]]]

        def softmax(unit, s):
            m_prev = stats[unit[1]][0]
            m_new = jnp.maximum(m_prev, jnp.max(s, axis=0, keepdims=True))
            stats[unit[1]][0] = m_new
            return jnp.exp2(s - m_new).astype(BF16), jnp.exp2(m_prev - m_new)

        def values(unit, p, alpha):
            comp = unit[1]
            stats[comp][1] = alpha * stats[comp][1] + value_dot(unit, p)

        _pipelined(units, scores, softmax, values)
        for comp in (0, 1):
            m_refs[comp][...] = stats[comp][0]
            a_refs[comp][...] = stats[comp][1]

    @pl.when(ki == pl.num_programs(3) - 1)
    def _():
        lv = lam_ref[...]
        lam = (jnp.exp(jnp.sum(lv[0:1] * lv[1:2], axis=1, keepdims=True))
               - jnp.exp(jnp.sum(lv[2:3] * lv[3:4], axis=1, keepdims=True)) + LAMBDA_INIT)
        d = ATT_V_DIM
        o = (a1[:d, :] / a1[d:d + 1, :] - lam * (a2[:d, :] / a2[d:d + 1, :]))
        inv = lax.rsqrt(jnp.mean(o * o, axis=0, keepdims=True) + SUBLN_EPS)
        g = pltpu.repeat(sg_ref[...], o.shape[1] // LANES, 1)
        o_ref[...] = (o * inv * g * (1.0 - LAMBDA_INIT)).astype(BF16)


def _attention(qt1, qt2, k, vt, lam_vecs, subln_g, batch, seq_len, tq=1024, tk=4096, chunk=512):
    tk = min(tk, seq_len)
    nq, nk = seq_len // tq, seq_len // tk
    q_spec = pl.BlockSpec((LANES, tq), lambda b, h, i, j: (h, b * nq + i))
    sg = jnp.broadcast_to(subln_g[:, None], (ATT_V_DIM, LANES))
    return pl.pallas_call(
        functools.partial(_attn_kernel, chunk=chunk),
        grid=(batch, N_ATT_HEADS, nq, nk),
        in_specs=[q_spec, q_spec,
                  pl.BlockSpec((tk, LANES), lambda b, h, i, j: (b * nk + j, h)),
                  pl.BlockSpec((VT_ROWS, tk), lambda b, h, i, j: (h, b * nk + j)),
                  pl.BlockSpec((4, ATT_HEAD_DIM), lambda b, h, i, j: (0, 0)),
                  pl.BlockSpec((ATT_V_DIM, LANES), lambda b, h, i, j: (0, 0))],
        out_specs=q_spec,
        out_shape=jax.ShapeDtypeStruct((ATT_WIDTH, batch * seq_len), BF16),
        scratch_shapes=[pltpu.VMEM((1, tq), F32), pltpu.VMEM((VT_ROWS, tq), F32)] * 2,
        compiler_params=_params(("parallel", "parallel", "parallel", "arbitrary")),
        name="diff_attention",
    )(qt1, qt2, k, vt, lam_vecs, sg)


def _split_bf16(x):
    hi = x.astype(BF16)
    return hi, (x - hi.astype(F32)).astype(BF16)


def _outproj_kernel(yh_ref, ya_ref, x_ref, wo_ref, g_ref, wqt_ref, kh_ref, kl_ref,
                    x2_ref, xnt_ref, st_ref):
    wo = wo_ref[...]
    x2 = (x_ref[...] + jnp.dot(yh_ref[...], wo[:HY_WIDTH], preferred_element_type=F32)
          + lax.dot_general(ya_ref[...], wo[HY_WIDTH:], (((0,), (0,)), ((), ())),
                            preferred_element_type=F32))
    x2_ref[...] = x2
    xn = _rms(x2, g_ref[...], EPS)
    xnt = xn.T.astype(BF16)
    xnt_ref[...] = xnt
    qt = jnp.dot(wqt_ref[...], xnt, preferred_element_type=F32)
    for hc in range(2 * PEER_HEADS):
        qh, ql = _split_bf16(qt[hc * PEER_HALF:(hc + 1) * PEER_HALF, :])
        kh, kl = kh_ref[hc], kl_ref[hc]
        st_ref[hc * PEER_KEYS:(hc + 1) * PEER_KEYS, :] = (
            jnp.dot(kh, qh, preferred_element_type=F32)
            + jnp.dot(kh, ql, preferred_element_type=F32)
            + jnp.dot(kl, qh, preferred_element_type=F32))


def _outproj(yh, ya, x, wo_bf, g2, wqt_bf, keys_hi, keys_lo, tm=512):
    t = x.shape[0]
    qw = 2 * PEER_HEADS * PEER_HALF
    sw = 2 * PEER_HEADS * PEER_KEYS
    row = lambda i: (i, 0)
    fixed = lambda i: (0, 0)
    fixed3 = lambda i: (0, 0, 0)
    return pl.pallas_call(
        _outproj_kernel,
        grid=(t // tm,),
        in_specs=[pl.BlockSpec((tm, HY_WIDTH), row),
                  pl.BlockSpec((ATT_WIDTH, tm), lambda i: (0, i)),
                  pl.BlockSpec((tm, D_MODEL), row),
                  pl.BlockSpec((HY_WIDTH + ATT_WIDTH, D_MODEL), fixed),
                  pl.BlockSpec((1, D_MODEL), fixed),
                  pl.BlockSpec((qw, D_MODEL), fixed),
                  pl.BlockSpec((2 * PEER_HEADS, PEER_KEYS, PEER_HALF), fixed3),
                  pl.BlockSpec((2 * PEER_HEADS, PEER_KEYS, PEER_HALF), fixed3)],
        out_specs=[pl.BlockSpec((tm, D_MODEL), row), pl.BlockSpec((D_MODEL, tm), lambda i: (0, i)),
                   pl.BlockSpec((sw, tm), lambda i: (0, i))],
        out_shape=[jax.ShapeDtypeStruct((t, D_MODEL), F32),
                   jax.ShapeDtypeStruct((D_MODEL, t), BF16),
                   jax.ShapeDtypeStruct((sw, t), F32)],
        compiler_params=_params(("parallel",)),
        name="outproj_peer_query",
    )(yh, ya, x, wo_bf, g2, wqt_bf, keys_hi, keys_lo)


def _extract_max(pieces, iotas, total, exact):
    m = pieces[0]
    for p in pieces[1:]:
        m = jnp.maximum(m, p)
    m = jnp.max(m, axis=0, keepdims=True)
    hot = [p == m for p in pieces]
    if exact:
        cand = [jnp.where(h, io, total) for h, io in zip(hot, iotas)]
        idx = cand[0]
        for c in cand[1:]:
            idx = jnp.minimum(idx, c)
        idx = jnp.min(idx, axis=0, keepdims=True)
        hot = [io == idx for io in iotas]
    return m, hot, [jnp.where(h, NEG_INF, p) for h, p in zip(hot, pieces)]


def _topk_rank(s, exact, want_rank):
    rank = jnp.full(s.shape, float(PEER_TOPK), F32) if want_rank else None
    iota = [lax.broadcasted_iota(jnp.int32, s.shape, 0)]
    vals = []
    for r in range(PEER_TOPK):
        m, hot, (s,) = _extract_max([s], iota, s.shape[0], exact)
        vals.append(m)
        if want_rank:
            rank = jnp.where(hot[0], float(r), rank)
    return vals, rank


_PAIRS = [(k0, k1) for k0 in range(PEER_TOPK) for k1 in range(PEER_TOPK // (k0 + 1))]
_PAIR_ROWS = -(-len(_PAIRS) // SUBLANES) * SUBLANES


def _rows_to_tiles(rows, width):
    sub = lax.broadcasted_iota(jnp.int32, (SUBLANES, width), 0)
    tiles = []
    for t0 in range(0, len(rows), SUBLANES):
        tile = jnp.full((SUBLANES, width), NEG_INF, F32)
        for r in range(SUBLANES):
            if rows[t0 + r] is not None:
                tile = jnp.where(sub == r, rows[t0 + r], tile)
        tiles.append(tile)
    return tiles


def _route_tables(s0, s1, exact):
    tt = s0.shape[1]
    v0, rank0 = _topk_rank(s0, exact, want_rank=exact)
    v1, rank1 = _topk_rank(s1, exact, want_rank=True)
    pad = [None] * (_PAIR_ROWS - len(_PAIRS))
    tiles = [a + b for a, b in zip(_rows_to_tiles([v0[k0] for k0, _ in _PAIRS] + pad, tt),
                                   _rows_to_tiles([v1[k1] for _, k1 in _PAIRS] + pad, tt))]
    iotas = [lax.broadcasted_iota(jnp.int32, (SUBLANES, tt), 0) + t0
             for t0 in range(0, _PAIR_ROWS, SUBLANES)]
    cmax = v0[0] + v1[0]
    z = None
    for r in range(PEER_TOPK):
        m, _, tiles = _extract_max(tiles, iotas, _PAIR_ROWS, exact)
        z = jnp.exp(m - cmax) if z is None else z + jnp.exp(m - cmax)
    cnt = jnp.zeros(s0.shape, F32)
    total = None
    for k0 in range(PEER_TOPK):
        rows = [r for r, (a, _) in enumerate(_PAIRS) if a == k0]
        taken = None
        for ti in sorted({r // SUBLANES for r in rows}):
            inside = (iotas[ti] >= rows[0]) & (iotas[ti] <= rows[-1])
            part = jnp.sum(jnp.where(inside & (tiles[ti] == NEG_INF), 1.0, 0.0),
                           axis=0, keepdims=True)
            taken = part if taken is None else taken + part
        total = taken if total is None else total + taken
        cnt = jnp.where((rank0 == float(k0)) if exact else (s0 == v0[k0]), taken, cnt)
    k = float(PEER_TOPK)
    count = lambda mask: jnp.sum(jnp.where(mask, 1.0, 0.0), axis=0, keepdims=True)
    top0 = (rank0 < k) if exact else (s0 >= v0[PEER_TOPK - 1])
    ok = (total == k) & (count(top0) == k) & (count(rank1 < k) == k)
    return cnt, jnp.exp(s0 - v0[0]) / z, rank1, jnp.exp(s1 - v1[0]), ok


def _route_kernel(st_ref, cnt_ref, w0_ref, r1_ref, e1_ref):
    def emit(exact):
        cnt, w0, rank1, e1, ok = _route_tables(st_ref[:PEER_KEYS, :], st_ref[PEER_KEYS:, :], exact)
        cnt_ref[0] = cnt
        w0_ref[0] = w0 * GELU_OUT_SCALE
        r1_ref[0] = rank1.astype(BF16)
        e1_ref[0] = e1.astype(BF16)
        return ok

    ok = emit(exact=False)
    clean = jnp.min(jnp.where(ok, 1, 0))

    @pl.when(clean == 0)
    def _():
        emit(exact=True)


def _route(st, tt=512):
    sw, t = st.shape
    o_spec = pl.BlockSpec((1, PEER_KEYS, tt), lambda i, h: (h, 0, i))
    tab = lambda dt: jax.ShapeDtypeStruct((PEER_HEADS, PEER_KEYS, t), dt)
    return pl.pallas_call(
        _route_kernel,
        grid=(t // tt, PEER_HEADS),
        in_specs=[pl.BlockSpec((2 * PEER_KEYS, tt), lambda i, h: (h, i))],
        out_specs=[o_spec] * 4,
        out_shape=[tab(F32), tab(F32), tab(BF16), tab(BF16)],
        compiler_params=_params(("parallel", "parallel")),
        name="peer_route",
    )(st)


GELU_IN_SCALE = 2.0 ** -0.5
GELU_OUT_SCALE = 0.5 * 2.0 ** 0.5


def _gelu_core(y):
    return y * (1.0 + lax.erf(y))


def _peer_kernel(xnt_ref, u_ref, vt_ref, cnt_ref, w0_ref, r1_ref, e1_ref, x2_ref, gf_ref,
                 y_ref, acc_ref, *, rows_per_step, rows_per_unit):
    e = pl.program_id(1)

    @pl.when(e == 0)
    def _():
        acc_ref[...] = jnp.zeros(acc_ref.shape, F32)

    tt = xnt_ref.shape[1]
    unit = rows_per_unit * PEER_KEYS
    n_units = rows_per_step // rows_per_unit

    def hidden(s):
        return jnp.dot(u_ref[s * unit:(s + 1) * unit, :], xnt_ref[...],
                       preferred_element_type=F32)

    def activation(s, hid):
        gates = []
        for ii in range(s * rows_per_unit, (s + 1) * rows_per_unit):
            g = None
            for h in range(PEER_HEADS):
                cnt_row = jnp.broadcast_to(cnt_ref[h, ii:ii + 1, :], (PEER_KEYS, tt)).astype(BF16)
                w_row = jnp.broadcast_to(w0_ref[h, ii:ii + 1, :], (PEER_KEYS, tt)).astype(BF16)
                term = jnp.where(r1_ref[h] < cnt_row, e1_ref[h] * w_row, jnp.zeros((), BF16))
                g = term if g is None else g + term
            gates.append(g)
        return _gelu_core(hid).astype(BF16) * jnp.concatenate(gates, axis=0)

    def mix(s, act):
        return jnp.dot(vt_ref[:, s * unit:(s + 1) * unit], act, preferred_element_type=F32)

    acc = acc_ref[...]
    h_next = hidden(0)
    pending = None
    for s in range(n_units):
        h_cur = h_next
        if s + 1 < n_units:
            h_next = hidden(s + 1)
        if pending is not None:
            acc = acc + mix(*pending)
        pending = (s, activation(s, h_cur))
    acc_ref[...] = acc + mix(*pending)

    @pl.when(e == pl.num_programs(1) - 1)
    def _():
        x3 = x2_ref[...] + acc_ref[...].T
        y_ref[...] = _rms(x3, gf_ref[...], EPS)


def _peer(xnt, u_bf, vt_bf, cnt, w0, r1, e1, x2, gf, tt=512, rows_per_step=16, rows_per_unit=2):
    t = xnt.shape[1]
    te = rows_per_step * PEER_KEYS
    tok = lambda i, e: (i, 0)
    tab_i = pl.BlockSpec((PEER_HEADS, rows_per_step, tt), lambda i, e: (0, e, i))
    tab_j = pl.BlockSpec((PEER_HEADS, PEER_KEYS, tt), lambda i, e: (0, 0, i))
    return pl.pallas_call(
        functools.partial(_peer_kernel, rows_per_step=rows_per_step,
                          rows_per_unit=rows_per_unit),
        grid=(t // tt, PEER_EXPERTS // te),
        in_specs=[pl.BlockSpec((D_MODEL, tt), lambda i, e: (0, i)),
                  pl.BlockSpec((te, D_MODEL), lambda i, e: (e, 0)),
                  pl.BlockSpec((D_MODEL, te), lambda i, e: (0, e)),
                  tab_i, tab_i, tab_j, tab_j,
                  pl.BlockSpec((tt, D_MODEL), tok),
                  pl.BlockSpec((1, D_MODEL), lambda i, e: (0, 0))],
        out_specs=pl.BlockSpec((tt, D_MODEL), tok),
        out_shape=jax.ShapeDtypeStruct((t, D_MODEL), F32),
        scratch_shapes=[pltpu.VMEM((D_MODEL, tt), F32)],
        compiler_params=_params(("parallel", "arbitrary")),
        name="peer_dense",
    )(xnt, u_bf, vt_bf, cnt, w0, r1, e1, x2, gf)


def _rope_tables(seq_len):
    f32 = np.float32
    inv = f32(ROPE_THETA) ** (-np.arange(0, ROT_DIM, 2, dtype=f32) / f32(ROT_DIM))
    ang = (np.arange(seq_len, dtype=f32)[:, None] * inv[None, :]).astype(np.float64)
    cos, sin = np.cos(ang), np.sin(ang)
    half = ROT_DIM // 2
    comp = np.zeros((3, seq_len, ATT_HEAD_DIM))
    comp[0, :, :half], comp[0, :, half:ROT_DIM], comp[0, :, ROT_DIM:] = cos, cos, 1.0
    comp[1, :, :half] = -sin
    comp[2, :, half:ROT_DIM] = sin
    both = np.concatenate([comp, comp], axis=2).astype(f32)
    return jnp.asarray(both[0]), jnp.asarray(both[1]), jnp.asarray(both[2])


def _trunk(x, p):
    batch, seq_len, _ = x.shape
    xf = x.reshape(batch * seq_len, D_MODEL)
    rope_c, rope_sa, rope_sb = _rope_tables(seq_len)
    zh, qt1, qt2, k, vt = _inproj(xf, p["norm1_g"], p["w_in"], rope_c, rope_sa, rope_sb, seq_len)
    u, x0 = _shortconv(zh, p["hy_conv_w"], p["hy_conv_b"], seq_len)
    filters = _hyena_filters(seq_len, *p["filter"])
    yh = _hyena_long_conv(u, x0, filters, p["hy_skip"], batch, seq_len)
    ya = _attention(qt1, qt2, k, vt, p["lam_vecs"], p["subln_g"], batch, seq_len)
    x2, xnt, st = _outproj(yh, ya, xf, p["w_out"], p["norm2_g"], p["wqt"],
                          p["keys_hi"], p["keys_lo"])
    cnt, w0, r1, e1 = _route(st)
    y = _peer(xnt, p["peer_u"], p["peer_vt"], cnt, w0, r1, e1, x2, p["normf_g"])
    return y.reshape(batch, seq_len, D_MODEL)


def kernel(x_prompt, x_sample, norm1_g, w_in, hy_conv_w, hy_conv_b, hf_w1, hf_b1, hf_freq1,
           hf_w2, hf_b2, hf_freq2, hf_w3, hf_b3, hy_skip, lam_q1, lam_k1, lam_q2, lam_k2,
           subln_g, w_out, norm2_g, peer_wq, peer_keys, peer_u, peer_v, normf_g):
    l = 0
    keys = peer_keys[l].reshape(2 * PEER_HEADS, PEER_KEYS, PEER_HALF)
    keys_hi = keys.astype(BF16)
    p = {
        "norm1_g": norm1_g[l][None, :],
        "w_in": w_in[l].astype(BF16),
        "hy_conv_w": hy_conv_w[l],
        "hy_conv_b": hy_conv_b[l][None, :],
        "filter": (hf_w1[l], hf_b1[l], hf_freq1[l], hf_w2[l], hf_b2[l], hf_freq2[l],
                   hf_w3[l], hf_b3[l]),
        "hy_skip": hy_skip[l],
        "lam_vecs": jnp.stack([lam_q1[l], lam_k1[l], lam_q2[l], lam_k2[l]]),
        "subln_g": subln_g[l],
        "w_out": w_out[l].astype(BF16),
        "norm2_g": norm2_g[l][None, :],
        "wqt": peer_wq[l].astype(BF16).T,
        "keys_hi": keys_hi,
        "keys_lo": (keys - keys_hi.astype(F32)).astype(BF16),
        "peer_u": (peer_u[l] * GELU_IN_SCALE).astype(BF16),
        "peer_vt": peer_v[l].astype(BF16).T,
        "normf_g": normf_g[None, :],
    }
    return (_trunk(x_prompt, p), _trunk(x_sample, p))
```

```python
import functools
import math

import jax
import jax.numpy as jnp
import numpy as np
from jax import lax
from jax.experimental import pallas as pl
from jax.experimental.pallas import tpu as pltpu

F32 = jnp.float32
BF16 = jnp.bfloat16

D_MODEL = 1024
HY_WIDTH = 512
ATT_WIDTH = 512
N_ATT_HEADS = 4
ATT_HEAD_DIM = 64
ATT_V_DIM = 2 * ATT_HEAD_DIM
ROT_DIM = ATT_HEAD_DIM // 4
ROPE_THETA = 500000.0
IN_WIDTH = 3 * HY_WIDTH + 3 * ATT_WIDTH
FILTER_BANDS = 8
FILTER_EMB = 1 + 2 * FILTER_BANDS
FILTER_HIDDEN = 64
FAST_DECAY_PCT = 0.3
SLOW_DECAY_PCT = 1.5
DECAY_TARGET = 1e-2
PEER_HEADS = 8
PEER_KEYS = 128
PEER_EXPERTS = PEER_KEYS * PEER_KEYS
PEER_TOPK = 16
PEER_HALF = 128
EPS = 1e-6
SUBLN_EPS = 1e-5
LAMBDA_INIT = 0.8 - 0.6 * math.exp(-0.3 * 0)

LANES = 128
SUBLANES = 8
VMEM_LIMIT = 56 * 1024 * 1024
FFT_N2 = 128
VT_ROWS = ATT_V_DIM + 2 * SUBLANES
STABILISER_HEADROOM = 64.0
NEG_INF = float("-inf")


def _params(semantics, vmem=VMEM_LIMIT):
    return pltpu.CompilerParams(dimension_semantics=semantics, vmem_limit_bytes=vmem)


def _rms(x, g, eps):
    return x * lax.rsqrt(jnp.mean(x * x, axis=-1, keepdims=True) + eps) * g


def _dot_nt(a, b):
    return lax.dot_general(a, b, (((1,), (1,)), ((), ())), preferred_element_type=F32)


def _store_time_blocks(dst_ref, x, lead=()):
    rows, c = x.shape
    for j in range(rows // LANES):
        for cb in range(c // LANES):
            dst_ref[lead + (j, slice(cb * LANES, (cb + 1) * LANES), slice(None))] = (
                x[j * LANES:(j + 1) * LANES, cb * LANES:(cb + 1) * LANES].T)


def _inproj_kernel(x_ref, xp_ref, xn_ref, g_ref, w_ref, c_ref, sa_ref, sb_ref, cw_ref, cb_ref,
                   ut_ref, x0_ref, qt1_ref, qt2_ref, k_ref, vt_ref, *, blocks_per_seq):
    i = pl.program_id(0)
    tm = x_ref.shape[0]
    hy3 = 3 * HY_WIDTH
    x_all = jnp.concatenate([xp_ref[...], x_ref[...], xn_ref[...]], axis=0)
    h = _rms(x_all, g_ref[...], EPS).astype(BF16)
    z_all = jnp.dot(h, w_ref[...], preferred_element_type=F32)
    z = z_all[SUBLANES:SUBLANES + tm]

    zh = z[:, :hy3]
    row = lax.broadcasted_iota(jnp.int32, zh.shape, 0)
    at_start = (i % blocks_per_seq) == 0
    at_end = (i % blocks_per_seq) == blocks_per_seq - 1
    prev_row = jnp.where(at_start, 0.0, z_all[SUBLANES - 1:SUBLANES, :hy3])
    next_row = jnp.where(at_end, 0.0, z_all[SUBLANES + tm:SUBLANES + tm + 1, :hy3])
    zm = jnp.where(row == 0, prev_row, pltpu.roll(zh, 1, 0))
    zp = jnp.where(row == tm - 1, next_row, pltpu.roll(zh, tm - 1, 0))
    cw = cw_ref[...]
    conv = zm * cw[0:1] + zh * cw[1:2] + zp * cw[2:3] + cb_ref[...]
    x0_ref[...] = conv[:, :HY_WIDTH]
    _store_time_blocks(ut_ref, conv[:, 2 * HY_WIDTH:] * conv[:, HY_WIDTH:2 * HY_WIDTH])

    c, sa, sb = c_ref[...], sa_ref[...], sb_ref[...]
    scale = ATT_HEAD_DIM ** -0.5 * math.log2(math.e)

    def rope(xb):
        return (xb * c + pltpu.roll(xb, LANES - ROT_DIM // 2, 1) * sa
                + pltpu.roll(xb, ROT_DIM // 2, 1) * sb)

    first = lax.broadcasted_iota(jnp.int32, (LANES, tm), 0) < ATT_HEAD_DIM
    for hd in range(N_ATT_HEADS):
        lo, hi = hd * LANES, (hd + 1) * LANES
        qt = (rope(z[:, hy3 + lo:hy3 + hi]) * scale).T
        qt1_ref[lo:hi, :] = jnp.where(first, qt, 0.0).astype(BF16)
        qt2_ref[lo:hi, :] = jnp.where(first, 0.0, qt).astype(BF16)
        k_ref[:, lo:hi] = rope(z[:, hy3 + ATT_WIDTH + lo:hy3 + ATT_WIDTH + hi]).astype(BF16)
        vlo = hd * VT_ROWS
        vt_ref[vlo:vlo + ATT_V_DIM, :] = (
            z[:, hy3 + 2 * ATT_WIDTH + lo:hy3 + 2 * ATT_WIDTH + hi].T.astype(BF16))
        vt_ref[vlo + ATT_V_DIM:vlo + VT_ROWS, :] = jnp.ones((VT_ROWS - ATT_V_DIM, tm), BF16)


def _inproj(x, g, w_bf, rope_c, rope_sa, rope_sb, conv_w, conv_b, seq_len, tm=512):
    t = x.shape[0]
    nseq = seq_len // tm
    per8 = tm // SUBLANES
    last8 = t // SUBLANES - 1
    row = lambda i: (i, 0)
    col = lambda i: (0, i)
    fixed = lambda i: (0, 0)
    pos = lambda i: (i % nseq, 0)
    hy3 = 3 * HY_WIDTH
    tr_spec = pl.BlockSpec((ATT_WIDTH, tm), col)
    tr_shape = jax.ShapeDtypeStruct((ATT_WIDTH, t), BF16)
    return pl.pallas_call(
        functools.partial(_inproj_kernel, blocks_per_seq=nseq),
        grid=(t // tm,),
        in_specs=[pl.BlockSpec((tm, D_MODEL), row),
                  pl.BlockSpec((SUBLANES, D_MODEL), lambda i: (jnp.maximum(i * per8 - 1, 0), 0)),
                  pl.BlockSpec((SUBLANES, D_MODEL), lambda i: (jnp.minimum((i + 1) * per8, last8), 0)),
                  pl.BlockSpec((1, D_MODEL), fixed),
                  pl.BlockSpec((D_MODEL, IN_WIDTH), fixed),
                  pl.BlockSpec((tm, LANES), pos),
                  pl.BlockSpec((tm, LANES), pos),
                  pl.BlockSpec((tm, LANES), pos),
                  pl.BlockSpec((3, hy3), fixed),
                  pl.BlockSpec((1, hy3), fixed)],
        out_specs=[pl.BlockSpec((tm // LANES, HY_WIDTH, LANES), lambda i: (i, 0, 0)),
                   pl.BlockSpec((tm, HY_WIDTH), row), tr_spec, tr_spec,
                   pl.BlockSpec((tm, ATT_WIDTH), row),
                   pl.BlockSpec((N_ATT_HEADS * VT_ROWS, tm), col)],
        out_shape=[jax.ShapeDtypeStruct((t // LANES, HY_WIDTH, LANES), F32),
                   jax.ShapeDtypeStruct((t, HY_WIDTH), F32), tr_shape, tr_shape,
                   jax.ShapeDtypeStruct((t, ATT_WIDTH), BF16),
                   jax.ShapeDtypeStruct((N_ATT_HEADS * VT_ROWS, t), BF16)],
        compiler_params=_params(("parallel",)),
        name="inproj",
    )(x, x, x, g, w_bf, rope_c, rope_sa, rope_sb, conv_w, conv_b)


def _filter_kernel(f_ref, w1_ref, b1_ref, f1_ref, w2_ref, b2_ref, f2_ref, w3_ref, b3_ref,
                   dl_ref, h_ref):
    i = pl.program_id(0)
    hp = lax.Precision.HIGHEST
    feats = f_ref[...]
    tl = feats.shape[0]
    h = jnp.sin(f1_ref[...] * (jnp.dot(feats, w1_ref[...], precision=hp,
                                       preferred_element_type=F32) + b1_ref[...]))
    h = jnp.sin(f2_ref[...] * (jnp.dot(h, w2_ref[...], precision=hp,
                                       preferred_element_type=F32) + b2_ref[...]))
    h3 = jnp.dot(h, w3_ref[...], precision=hp, preferred_element_type=F32) + b3_ref[...]
    decay = jnp.exp(-feats[:, 0:1] * dl_ref[...])
    _store_time_blocks(h_ref, h3[:, :HY_WIDTH] * decay, lead=(0,))
    row = lax.broadcasted_iota(jnp.int32, (tl, HY_WIDTH), 0) + i * tl
    _store_time_blocks(h_ref, jnp.where(row == 0, 0.0, h3[:, HY_WIDTH:] * decay), lead=(1,))


def _pad2(a, rows, cols):
    return jnp.pad(a, ((0, rows - a.shape[0]), (0, cols - a.shape[1])))


def _hyena_filters(seq_len, w1, b1, f1, w2, b2, f2, w3, b3, tl=1024):
    f32 = np.float32
    t = np.linspace(0.0, 1.0, seq_len).astype(f32)[:, None]
    w = f32(2.0 * math.pi) * np.arange(seq_len, dtype=f32) / f32(seq_len)
    bands = np.linspace(1e-4, FILTER_BANDS - 1, FILTER_BANDS).astype(f32)
    ang = (w[:, None] * bands[None, :]).astype(np.float64)
    feats_np = np.zeros((seq_len, LANES), f32)
    feats_np[:, :FILTER_EMB] = np.concatenate([t, np.cos(ang), -np.sin(ang)], axis=-1)
    feats = jnp.asarray(feats_np)
    max_decay = math.log(DECAY_TARGET) / FAST_DECAY_PCT
    min_decay = math.log(DECAY_TARGET) / SLOW_DECAY_PCT
    deltas = jnp.abs(jnp.linspace(min_decay, max_decay, HY_WIDTH, dtype=F32))[None, :]
    w1p = _pad2(w1, LANES, LANES)
    w2p = _pad2(w2, LANES, LANES)
    w3p = _pad2(w3, LANES, 2 * HY_WIDTH)
    pad1 = lambda v: _pad2(v[None, :], 1, LANES)
    fixed = lambda i: (0, 0)
    vec = pl.BlockSpec((1, LANES), fixed)
    mat = pl.BlockSpec((LANES, LANES), fixed)
    return pl.pallas_call(
        _filter_kernel,
        grid=(seq_len // tl,),
        in_specs=[pl.BlockSpec((tl, LANES), lambda i: (i, 0)), mat, vec, vec, mat, vec, vec,
                  pl.BlockSpec((LANES, 2 * HY_WIDTH), fixed),
                  pl.BlockSpec((1, 2 * HY_WIDTH), fixed),
                  pl.BlockSpec((1, HY_WIDTH), fixed)],
        out_specs=pl.BlockSpec((2, tl // LANES, HY_WIDTH, LANES), lambda i: (0, i, 0, 0)),
        out_shape=jax.ShapeDtypeStruct((2, seq_len // LANES, HY_WIDTH, LANES), F32),
        compiler_params=_params(("parallel",)),
        name="hyena_filter",
    )(feats, w1p, pad1(b1), pad1(f1), w2p, pad1(b2), pad1(f2), w3p, b3[None, :], deltas)


def _dft_tables(seq_len):
    n = 2 * seq_len
    n2 = FFT_N2
    n1 = n // n2
    n1h = n1 // 2
    a1 = (2.0 * np.pi / n1) * np.outer(np.arange(n1), np.arange(n1))
    f1r, f1i = np.cos(a1), -np.sin(a1)
    fwd1 = np.concatenate([f1r[:, :n1h], f1i[:, :n1h]], axis=0)
    inv1 = np.concatenate([f1r[:n1h, :], f1i[:n1h, :]], axis=1) / n
    a2 = (2.0 * np.pi / n2) * np.outer(np.arange(n2), np.arange(n2))
    f2 = np.concatenate([np.cos(a2), -np.sin(a2)], axis=1)
    at = (2.0 * np.pi / n) * np.outer(np.arange(n1), np.arange(n2))
    return (jnp.asarray(fwd1, BF16), jnp.asarray(inv1, BF16), jnp.asarray(f2, BF16),
            jnp.asarray(np.cos(at), F32), jnp.asarray(-np.sin(at), F32))


def _forward_spectra(tiles, fwd1, twr, twi, f2):
    n1, w = twr.shape
    tr, ti = _tile_lanes(twr, 2 * w), _tile_lanes(twi, 2 * w)
    firsts = [jnp.dot(fwd1, jnp.concatenate([xa, xb], axis=1).astype(BF16),
                      preferred_element_type=F32) for xa, xb in tiles]
    twiddled = []
    for a in firsts:
        ar, ai = a[:n1], a[n1:]
        zr, zi = ar * tr - ai * ti, ar * ti + ai * tr
        twiddled.append(jnp.concatenate([zr[:, :w], zi[:, :w], zr[:, w:], zi[:, w:]],
                                        axis=0).astype(BF16))
    seconds = [jnp.dot(z, f2, preferred_element_type=F32) for z in twiddled]

    def spectrum(q):
        qr, qi = q[:n1], q[n1:]
        return qr[:, :w] - qi[:, w:], qr[:, w:] + qi[:, :w]

    return [(spectrum(p[:2 * n1]), spectrum(p[2 * n1:])) for p in seconds]


def _inverse_signals(spectra, inv1, twr, twi, f2):
    n1, w = twr.shape
    firsts = [jnp.dot(jnp.concatenate([wa[0], wa[1], wb[0], wb[1]], axis=0).astype(BF16), f2,
                      preferred_element_type=F32) for wa, wb in spectra]

    def untwiddle(r):
        rr, ri = r[:n1], r[n1:]
        vr, vi = rr[:, :w] + ri[:, w:], ri[:, :w] - rr[:, w:]
        return vr * twr + vi * twi, vi * twr - vr * twi

    stacked = []
    for q in firsts:
        (bra, bia), (brb, bib) = untwiddle(q[:2 * n1]), untwiddle(q[2 * n1:])
        stacked.append(jnp.concatenate([jnp.concatenate([bra, brb], axis=1),
                                        jnp.concatenate([bia, bib], axis=1)],
                                       axis=0).astype(BF16))
    outs = [jnp.dot(inv1, r, preferred_element_type=F32) for r in stacked]
    return [(y[:, :w], y[:, w:]) for y in outs]


def _tile_lanes(x, width):
    return jnp.concatenate([x] * (width // x.shape[1]), axis=1)


def _spectrum_kernel(h_ref, fwd1_ref, f2_ref, twr_ref, twi_ref, kr_ref, ki_ref):
    fwd1, f2, twr, twi = fwd1_ref[...], f2_ref[...], twr_ref[...], twi_ref[...]
    cs = range(h_ref.shape[2])
    out = _forward_spectra([(h_ref[0, :, c, :], h_ref[1, :, c, :]) for c in cs], fwd1, twr, twi, f2)
    for c, ((fr, fi), (gr, gi)) in zip(cs, out):
        kr_ref[c] = (fr + gr).astype(BF16)
        ki_ref[c] = (fi - gi).astype(BF16)


def _filter_spectrum(h3, tables, cb=8):
    fwd1, _, f2, twr, twi = tables
    _, n1h, c, w = h3.shape
    n1 = 2 * n1h
    const = lambda a: pl.BlockSpec(a.shape, lambda i: (0,) * a.ndim)
    o_spec = pl.BlockSpec((cb, n1, w), lambda i: (i, 0, 0))
    return pl.pallas_call(
        _spectrum_kernel,
        grid=(c // cb,),
        in_specs=[pl.BlockSpec((2, n1h, cb, w), lambda i: (0, 0, i, 0)),
                  const(fwd1), const(f2), const(twr), const(twi)],
        out_specs=[o_spec, o_spec],
        out_shape=[jax.ShapeDtypeStruct((c, n1, w), BF16)] * 2,
        compiler_params=_params(("parallel",)),
        name="filter_spectrum",
    )(h3, fwd1, f2, twr, twi)


def _longconv_kernel(u_ref, kr_ref, ki_ref, skip_ref, fwd1_ref, inv1_ref, f2_ref, twr_ref,
                     twi_ref, y_ref):
    fwd1, inv1, f2 = fwd1_ref[...], inv1_ref[...], f2_ref[...]
    twr, twi = twr_ref[...], twi_ref[...]

    def filtered(c, spec):
        kr, ki = kr_ref[c].astype(F32), ki_ref[c].astype(F32)
        return spec[0] * kr - spec[1] * ki, spec[0] * ki + spec[1] * kr

    cs = range(0, u_ref.shape[1], 2)
    xs = [(u_ref[:, c, :], u_ref[:, c + 1, :]) for c in cs]
    spectra = _forward_spectra(xs, fwd1, twr, twi, f2)
    prods = [(filtered(c, sa), filtered(c + 1, sb)) for c, (sa, sb) in zip(cs, spectra)]
    for c, (xa, xb), (ya, yb) in zip(cs, xs, _inverse_signals(prods, inv1, twr, twi, f2)):
        y_ref[c] = ya + skip_ref[c] * xa
        y_ref[c + 1] = yb + skip_ref[c + 1] * xb


def _longconv(ut3, kr, ki, skip3, tables, batch, cb=16):
    fwd1, inv1, f2, twr, twi = tables
    rows, c, w = ut3.shape
    n1h = rows // batch
    n1 = 2 * n1h
    const = lambda a: pl.BlockSpec(a.shape, lambda i, b: (0,) * a.ndim)
    spec = pl.BlockSpec((cb, n1, w), lambda i, b: (i, 0, 0))
    return pl.pallas_call(
        _longconv_kernel,
        grid=(c // cb, batch),
        in_specs=[pl.BlockSpec((n1h, cb, w), lambda i, b: (b, i, 0)), spec, spec,
                  pl.BlockSpec((cb, 1, w), lambda i, b: (i, 0, 0)),
                  const(fwd1), const(inv1), const(f2), const(twr), const(twi)],
        out_specs=pl.BlockSpec((cb, n1h, w), lambda i, b: (i, b, 0)),
        out_shape=jax.ShapeDtypeStruct((c, rows, w), F32),
        compiler_params=_params(("parallel", "parallel")),
        name="long_conv",
    )(ut3, kr, ki, skip3, fwd1, inv1, f2, twr, twi)


def _gate_kernel(y_ref, x0_ref, o_ref):
    rows, c = x0_ref.shape
    for j in range(rows // LANES):
        for cb in range(c // LANES):
            r, l = slice(j * LANES, (j + 1) * LANES), slice(cb * LANES, (cb + 1) * LANES)
            o_ref[r, l] = (x0_ref[r, l] * y_ref[l, j, :].T).astype(BF16)


def _gate(y3, x0, tl=1024):
    t, c = x0.shape
    return pl.pallas_call(
        _gate_kernel,
        grid=(t // tl,),
        in_specs=[pl.BlockSpec((c, tl // LANES, LANES), lambda i: (0, i, 0)),
                  pl.BlockSpec((tl, c), lambda i: (i, 0))],
        out_specs=pl.BlockSpec((tl, c), lambda i: (i, 0)),
        out_shape=jax.ShapeDtypeStruct((t, c), BF16),
        compiler_params=_params(("parallel",)),
        name="hyena_gate",
    )(y3, x0)


def _hyena_long_conv(ut3, x0, filters, skip, batch, seq_len):
    tables = _dft_tables(seq_len)
    kr, ki = _filter_spectrum(filters, tables)
    skip3 = jnp.broadcast_to(skip[:, None, None], (HY_WIDTH, 1, LANES))
    return _gate(_longconv(ut3, kr, ki, skip3, tables, batch), x0)


def _pipelined(units, scores, softmax, values):
    s_next = scores(units[0])
    pending = None
    for i, unit in enumerate(units):
        s_cur = s_next
        if i + 1 < len(units):
            s_next = scores(units[i + 1])
        if pending is not None:
            values(*pending)
        pending = (unit,) + softmax(unit, s_cur)
    values(*pending)


def _attn_kernel(q1_ref, q2_ref, k_ref, vt_ref, lam_ref, sg_ref, o_ref,
                 m1, a1, m2, a2, *, chunk):
    ki = pl.program_id(3)
    q_refs = (q1_ref, q2_ref)
    m_refs = (m1, m2)
    a_refs = (a1, a2)
    n_chunks = k_ref.shape[0] // chunk
    units = [(c, comp) for c in range(n_chunks) for comp in (0, 1)]

    def scores(unit):
        c, comp = unit
        return jnp.dot(k_ref[c * chunk:(c + 1) * chunk, :], q_refs[comp][...],
                       preferred_element_type=F32)

    def value_dot(unit, p):
        c = unit[0]
        return jnp.dot(vt_ref[:, c * chunk:(c + 1) * chunk], p, preferred_element_type=F32)

    @pl.when(ki == 0)
    def _():
        for comp in (0, 1):
            head = jnp.dot(k_ref[0:2 * SUBLANES, :], q_refs[comp][...], preferred_element_type=F32)
            m_refs[comp][...] = jnp.max(head, axis=0, keepdims=True)
            a_refs[comp][...] = jnp.zeros(a_refs[comp].shape, F32)

    m_start = [m1[...], m2[...]]
    step_max = [None, None]
    step_acc = [None, None]

    def fast_softmax(unit, s):
        comp = unit[1]
        cm = jnp.max(s, axis=0, keepdims=True)
        step_max[comp] = cm if step_max[comp] is None else jnp.maximum(step_max[comp], cm)
        return (jnp.exp2(s - m_start[comp]).astype(BF16),)

    def fast_values(unit, p):
        comp = unit[1]
        d = value_dot(unit, p)
        step_acc[comp] = d if step_acc[comp] is None else step_acc[comp] + d

    _pipelined(units, scores, fast_softmax, fast_values)
    excess = jnp.maximum(step_max[0] - m_start[0], step_max[1] - m_start[1])
    fast_ok = jnp.max(excess) <= STABILISER_HEADROOM

    @pl.when(fast_ok)
    def _():
        for comp in (0, 1):
            m_new = jnp.maximum(m_start[comp], step_max[comp])
            a_refs[comp][...] = jnp.exp2(m_start[comp] - m_new) * (a_refs[comp][...] + step_acc[comp])
            m_refs[comp][...] = m_new

    @pl.when(jnp.logical_not(fast_ok))
    def _():
        stats = [[m1[...], a1[...]], [m2[...], a2[...]]]

        def softmax(unit, s):
            m_prev = stats[unit[1]][0]
            m_new = jnp.maximum(m_prev, jnp.max(s, axis=0, keepdims=True))
            stats[unit[1]][0] = m_new
            return jnp.exp2(s - m_new).astype(BF16), jnp.exp2(m_prev - m_new)

        def values(unit, p, alpha):
            comp = unit[1]
            stats[comp][1] = alpha * stats[comp][1] + value_dot(unit, p)

        _pipelined(units, scores, softmax, values)
        for comp in (0, 1):
            m_refs[comp][...] = stats[comp][0]
            a_refs[comp][...] = stats[comp][1]

    @pl.when(ki == pl.num_programs(3) - 1)
    def _():
        lv = lam_ref[...]
        lam = (jnp.exp(jnp.sum(lv[0:1] * lv[1:2], axis=1, keepdims=True))
               - jnp.exp(jnp.sum(lv[2:3] * lv[3:4], axis=1, keepdims=True)) + LAMBDA_INIT)
        d = ATT_V_DIM
        o = (a1[:d, :] / a1[d:d + 1, :] - lam * (a2[:d, :] / a2[d:d + 1, :]))
        inv = lax.rsqrt(jnp.mean(o * o, axis=0, keepdims=True) + SUBLN_EPS)
        g = pltpu.repeat(sg_ref[...], o.shape[1] // LANES, 1)
        o_ref[...] = (o * inv * g * (1.0 - LAMBDA_INIT)).astype(BF16)


def _attention(qt1, qt2, k, vt, lam_vecs, subln_g, batch, seq_len, tq=1024, tk=4096, chunk=512):
    tk = min(tk, seq_len)
    nq, nk = seq_len // tq, seq_len // tk
    q_spec = pl.BlockSpec((LANES, tq), lambda b, h, i, j: (h, b * nq + i))
    sg = jnp.broadcast_to(subln_g[:, None], (ATT_V_DIM, LANES))
    return pl.pallas_call(
        functools.partial(_attn_kernel, chunk=chunk),
        grid=(batch, N_ATT_HEADS, nq, nk),
        in_specs=[q_spec, q_spec,
                  pl.BlockSpec((tk, LANES), lambda b, h, i, j: (b * nk + j, h)),
                  pl.BlockSpec((VT_ROWS, tk), lambda b, h, i, j: (h, b * nk + j)),
                  pl.BlockSpec((4, ATT_HEAD_DIM), lambda b, h, i, j: (0, 0)),
                  pl.BlockSpec((ATT_V_DIM, LANES), lambda b, h, i, j: (0, 0))],
        out_specs=q_spec,
        out_shape=jax.ShapeDtypeStruct((ATT_WIDTH, batch * seq_len), BF16),
        scratch_shapes=[pltpu.VMEM((1, tq), F32), pltpu.VMEM((VT_ROWS, tq), F32)] * 2,
        compiler_params=_params(("parallel", "parallel", "parallel", "arbitrary")),
        name="diff_attention",
    )(qt1, qt2, k, vt, lam_vecs, sg)


def _split_bf16(x):
    hi = x.astype(BF16)
    return hi, (x - hi.astype(F32)).astype(BF16)


def _outproj_kernel(yh_ref, ya_ref, x_ref, wo_ref, g_ref, wqt_ref, kh_ref, kl_ref,
                    x2_ref, xnt_ref, st_ref):
    wo = wo_ref[...]
    x2 = (x_ref[...] + jnp.dot(yh_ref[...], wo[:HY_WIDTH], preferred_element_type=F32)
          + lax.dot_general(ya_ref[...], wo[HY_WIDTH:], (((0,), (0,)), ((), ())),
                            preferred_element_type=F32))
    x2_ref[...] = x2
    xn = _rms(x2, g_ref[...], EPS)
    xnt = xn.T.astype(BF16)
    xnt_ref[...] = xnt
    qt = jnp.dot(wqt_ref[...], xnt, preferred_element_type=F32)
    for hc in range(2 * PEER_HEADS):
        qh, ql = _split_bf16(qt[hc * PEER_HALF:(hc + 1) * PEER_HALF, :])
        kh, kl = kh_ref[hc], kl_ref[hc]
        st_ref[hc * PEER_KEYS:(hc + 1) * PEER_KEYS, :] = (
            jnp.dot(kh, qh, preferred_element_type=F32)
            + jnp.dot(kh, ql, preferred_element_type=F32)
            + jnp.dot(kl, qh, preferred_element_type=F32))


def _outproj(yh, ya, x, wo_bf, g2, wqt_bf, keys_hi, keys_lo, tm=512):
    t = x.shape[0]
    qw = 2 * PEER_HEADS * PEER_HALF
    sw = 2 * PEER_HEADS * PEER_KEYS
    row = lambda i: (i, 0)
    fixed = lambda i: (0, 0)
    fixed3 = lambda i: (0, 0, 0)
    return pl.pallas_call(
        _outproj_kernel,
        grid=(t // tm,),
        in_specs=[pl.BlockSpec((tm, HY_WIDTH), row),
                  pl.BlockSpec((ATT_WIDTH, tm), lambda i: (0, i)),
                  pl.BlockSpec((tm, D_MODEL), row),
                  pl.BlockSpec((HY_WIDTH + ATT_WIDTH, D_MODEL), fixed),
                  pl.BlockSpec((1, D_MODEL), fixed),
                  pl.BlockSpec((qw, D_MODEL), fixed),
                  pl.BlockSpec((2 * PEER_HEADS, PEER_KEYS, PEER_HALF), fixed3),
                  pl.BlockSpec((2 * PEER_HEADS, PEER_KEYS, PEER_HALF), fixed3)],
        out_specs=[pl.BlockSpec((tm, D_MODEL), row), pl.BlockSpec((D_MODEL, tm), lambda i: (0, i)),
                   pl.BlockSpec((sw, tm), lambda i: (0, i))],
        out_shape=[jax.ShapeDtypeStruct((t, D_MODEL), F32),
                   jax.ShapeDtypeStruct((D_MODEL, t), BF16),
                   jax.ShapeDtypeStruct((sw, t), F32)],
        compiler_params=_params(("parallel",)),
        name="outproj_peer_query",
    )(yh, ya, x, wo_bf, g2, wqt_bf, keys_hi, keys_lo)


def _extract_max(pieces, iotas, total, exact):
    m = pieces[0]
    for p in pieces[1:]:
        m = jnp.maximum(m, p)
    m = jnp.max(m, axis=0, keepdims=True)
    hot = [p == m for p in pieces]
    if exact:
        cand = [jnp.where(h, io, total) for h, io in zip(hot, iotas)]
        idx = cand[0]
        for c in cand[1:]:
            idx = jnp.minimum(idx, c)
        idx = jnp.min(idx, axis=0, keepdims=True)
        hot = [io == idx for io in iotas]
    return m, hot, [jnp.where(h, NEG_INF, p) for h, p in zip(hot, pieces)]


def _topk_rank(s, exact, want_rank):
    rank = jnp.full(s.shape, float(PEER_TOPK), F32) if want_rank else None
    iota = [lax.broadcasted_iota(jnp.int32, s.shape, 0)]
    vals = []
    for r in range(PEER_TOPK):
        m, hot, (s,) = _extract_max([s], iota, s.shape[0], exact)
        vals.append(m)
        if want_rank:
            rank = jnp.where(hot[0], float(r), rank)
    return vals, rank


_PAIRS = [(k0, k1) for k0 in range(PEER_TOPK) for k1 in range(PEER_TOPK // (k0 + 1))]
_PAIR_ROWS = -(-len(_PAIRS) // SUBLANES) * SUBLANES


def _rows_to_tiles(rows, width):
    sub = lax.broadcasted_iota(jnp.int32, (SUBLANES, width), 0)
    tiles = []
    for t0 in range(0, len(rows), SUBLANES):
        tile = jnp.full((SUBLANES, width), NEG_INF, F32)
        for r in range(SUBLANES):
            if rows[t0 + r] is not None:
                tile = jnp.where(sub == r, rows[t0 + r], tile)
        tiles.append(tile)
    return tiles


def _route_tables(s0, s1, exact):
    tt = s0.shape[1]
    v0, rank0 = _topk_rank(s0, exact, want_rank=exact)
    v1, rank1 = _topk_rank(s1, exact, want_rank=True)
    pad = [None] * (_PAIR_ROWS - len(_PAIRS))
    tiles = [a + b for a, b in zip(_rows_to_tiles([v0[k0] for k0, _ in _PAIRS] + pad, tt),
                                   _rows_to_tiles([v1[k1] for _, k1 in _PAIRS] + pad, tt))]
    iotas = [lax.broadcasted_iota(jnp.int32, (SUBLANES, tt), 0) + t0
             for t0 in range(0, _PAIR_ROWS, SUBLANES)]
    cmax = v0[0] + v1[0]
    z = None
    for r in range(PEER_TOPK):
        m, _, tiles = _extract_max(tiles, iotas, _PAIR_ROWS, exact)
        z = jnp.exp(m - cmax) if z is None else z + jnp.exp(m - cmax)
    cnt = jnp.zeros(s0.shape, F32)
    total = None
    for k0 in range(PEER_TOPK):
        rows = [r for r, (a, _) in enumerate(_PAIRS) if a == k0]
        taken = None
        for ti in sorted({r // SUBLANES for r in rows}):
            inside = (iotas[ti] >= rows[0]) & (iotas[ti] <= rows[-1])
            part = jnp.sum(jnp.where(inside & (tiles[ti] == NEG_INF), 1.0, 0.0),
                           axis=0, keepdims=True)
            taken = part if taken is None else taken + part
        total = taken if total is None else total + taken
        cnt = jnp.where((rank0 == float(k0)) if exact else (s0 == v0[k0]), taken, cnt)
    k = float(PEER_TOPK)
    count = lambda mask: jnp.sum(jnp.where(mask, 1.0, 0.0), axis=0, keepdims=True)
    top0 = (rank0 < k) if exact else (s0 >= v0[PEER_TOPK - 1])
    ok = (total == k) & (count(top0) == k) & (count(rank1 < k) == k)
    return cnt, jnp.exp(s0 - v0[0]) / z, rank1, jnp.exp(s1 - v1[0]), ok


def _route_kernel(st_ref, cnt_ref, w0_ref, r1_ref, e1_ref):
    def emit(exact):
        cnt, w0, rank1, e1, ok = _route_tables(st_ref[:PEER_KEYS, :], st_ref[PEER_KEYS:, :], exact)
        cnt_ref[0] = cnt
        w0_ref[0] = w0 * GELU_OUT_SCALE
        r1_ref[0] = rank1.astype(BF16)
        e1_ref[0] = e1.astype(BF16)
        return ok

    ok = emit(exact=False)
    clean = jnp.min(jnp.where(ok, 1, 0))

    @pl.when(clean == 0)
    def _():
        emit(exact=True)


def _route(st, tt=512):
    sw, t = st.shape
    o_spec = pl.BlockSpec((1, PEER_KEYS, tt), lambda i, h: (h, 0, i))
    tab = lambda dt: jax.ShapeDtypeStruct((PEER_HEADS, PEER_KEYS, t), dt)
    return pl.pallas_call(
        _route_kernel,
        grid=(t // tt, PEER_HEADS),
        in_specs=[pl.BlockSpec((2 * PEER_KEYS, tt), lambda i, h: (h, i))],
        out_specs=[o_spec] * 4,
        out_shape=[tab(F32), tab(F32), tab(BF16), tab(BF16)],
        compiler_params=_params(("parallel", "parallel")),
        name="peer_route",
    )(st)


GELU_IN_SCALE = 2.0 ** -0.5
GELU_OUT_SCALE = 0.5 * 2.0 ** 0.5


def _gelu_core(y):
    return y * (1.0 + lax.erf(y))


def _peer_kernel(xnt_ref, u_ref, vt_ref, cnt_ref, w0_ref, r1_ref, e1_ref, x2_ref, gf_ref,
                 y_ref, acc_ref, *, rows_per_step, rows_per_unit):
    e = pl.program_id(1)

    @pl.when(e == 0)
    def _():
        acc_ref[...] = jnp.zeros(acc_ref.shape, F32)

    tt = xnt_ref.shape[1]
    unit = rows_per_unit * PEER_KEYS
    n_units = rows_per_step // rows_per_unit

    def hidden(s):
        return jnp.dot(u_ref[s * unit:(s + 1) * unit, :], xnt_ref[...],
                       preferred_element_type=F32)

    def activation(s, hid):
        gates = []
        for ii in range(s * rows_per_unit, (s + 1) * rows_per_unit):
            g = None
            for h in range(PEER_HEADS):
                cnt_row = jnp.broadcast_to(cnt_ref[h, ii:ii + 1, :], (PEER_KEYS, tt)).astype(BF16)
                w_row = jnp.broadcast_to(w0_ref[h, ii:ii + 1, :], (PEER_KEYS, tt)).astype(BF16)
                term = jnp.where(r1_ref[h] < cnt_row, e1_ref[h] * w_row, jnp.zeros((), BF16))
                g = term if g is None else g + term
            gates.append(g)
        return _gelu_core(hid).astype(BF16) * jnp.concatenate(gates, axis=0)

    def mix(s, act):
        return jnp.dot(vt_ref[:, s * unit:(s + 1) * unit], act, preferred_element_type=F32)

    acc = acc_ref[...]
    h_next = hidden(0)
    pending = None
    for s in range(n_units):
        h_cur = h_next
        if s + 1 < n_units:
            h_next = hidden(s + 1)
        if pending is not None:
            acc = acc + mix(*pending)
        pending = (s, activation(s, h_cur))
    acc_ref[...] = acc + mix(*pending)

    @pl.when(e == pl.num_programs(1) - 1)
    def _():
        x3 = x2_ref[...] + acc_ref[...].T
        y_ref[...] = _rms(x3, gf_ref[...], EPS)


def _peer(xnt, u_bf, vt_bf, cnt, w0, r1, e1, x2, gf, tt=512, rows_per_step=16, rows_per_unit=2):
    t = xnt.shape[1]
    te = rows_per_step * PEER_KEYS
    tok = lambda i, e: (i, 0)
    tab_i = pl.BlockSpec((PEER_HEADS, rows_per_step, tt), lambda i, e: (0, e, i))
    tab_j = pl.BlockSpec((PEER_HEADS, PEER_KEYS, tt), lambda i, e: (0, 0, i))
    return pl.pallas_call(
        functools.partial(_peer_kernel, rows_per_step=rows_per_step,
                          rows_per_unit=rows_per_unit),
        grid=(t // tt, PEER_EXPERTS // te),
        in_specs=[pl.BlockSpec((D_MODEL, tt), lambda i, e: (0, i)),
                  pl.BlockSpec((te, D_MODEL), lambda i, e: (e, 0)),
                  pl.BlockSpec((D_MODEL, te), lambda i, e: (0, e)),
                  tab_i, tab_i, tab_j, tab_j,
                  pl.BlockSpec((tt, D_MODEL), tok),
                  pl.BlockSpec((1, D_MODEL), lambda i, e: (0, 0))],
        out_specs=pl.BlockSpec((tt, D_MODEL), tok),
        out_shape=jax.ShapeDtypeStruct((t, D_MODEL), F32),
        scratch_shapes=[pltpu.VMEM((D_MODEL, tt), F32)],
        compiler_params=_params(("parallel", "arbitrary")),
        name="peer_dense",
    )(xnt, u_bf, vt_bf, cnt, w0, r1, e1, x2, gf)


def _rope_tables(seq_len):
    f32 = np.float32
    inv = f32(ROPE_THETA) ** (-np.arange(0, ROT_DIM, 2, dtype=f32) / f32(ROT_DIM))
    ang = (np.arange(seq_len, dtype=f32)[:, None] * inv[None, :]).astype(np.float64)
    cos, sin = np.cos(ang), np.sin(ang)
    half = ROT_DIM // 2
    comp = np.zeros((3, seq_len, ATT_HEAD_DIM))
    comp[0, :, :half], comp[0, :, half:ROT_DIM], comp[0, :, ROT_DIM:] = cos, cos, 1.0
    comp[1, :, :half] = -sin
    comp[2, :, half:ROT_DIM] = sin
    both = np.concatenate([comp, comp], axis=2).astype(f32)
    return jnp.asarray(both[0]), jnp.asarray(both[1]), jnp.asarray(both[2])


def _trunk(x, p):
    batch, seq_len, _ = x.shape
    xf = x.reshape(batch * seq_len, D_MODEL)
    rope_c, rope_sa, rope_sb = _rope_tables(seq_len)
    u, x0, qt1, qt2, k, vt = _inproj(xf, p["norm1_g"], p["w_in"], rope_c, rope_sa, rope_sb,
                                     p["hy_conv_w"], p["hy_conv_b"], seq_len)
    filters = _hyena_filters(seq_len, *p["filter"])
    yh = _hyena_long_conv(u, x0, filters, p["hy_skip"], batch, seq_len)
    ya = _attention(qt1, qt2, k, vt, p["lam_vecs"], p["subln_g"], batch, seq_len)
    x2, xnt, st = _outproj(yh, ya, xf, p["w_out"], p["norm2_g"], p["wqt"],
                          p["keys_hi"], p["keys_lo"])
    cnt, w0, r1, e1 = _route(st)
    y = _peer(xnt, p["peer_u"], p["peer_vt"], cnt, w0, r1, e1, x2, p["normf_g"])
    return y.reshape(batch, seq_len, D_MODEL)


def kernel(x_prompt, x_sample, norm1_g, w_in, hy_conv_w, hy_conv_b, hf_w1, hf_b1, hf_freq1,
           hf_w2, hf_b2, hf_freq2, hf_w3, hf_b3, hy_skip, lam_q1, lam_k1, lam_q2, lam_k2,
           subln_g, w_out, norm2_g, peer_wq, peer_keys, peer_u, peer_v, normf_g):
    l = 0
    keys = peer_keys[l].reshape(2 * PEER_HEADS, PEER_KEYS, PEER_HALF)
    keys_hi = keys.astype(BF16)
    p = {
        "norm1_g": norm1_g[l][None, :],
        "w_in": w_in[l].astype(BF16),
        "hy_conv_w": hy_conv_w[l],
        "hy_conv_b": hy_conv_b[l][None, :],
        "filter": (hf_w1[l], hf_b1[l], hf_freq1[l], hf_w2[l], hf_b2[l], hf_freq2[l],
                   hf_w3[l], hf_b3[l]),
        "hy_skip": hy_skip[l],
        "lam_vecs": jnp.stack([lam_q1[l], lam_k1[l], lam_q2[l], lam_k2[l]]),
        "subln_g": subln_g[l],
        "w_out": w_out[l].astype(BF16),
        "norm2_g": norm2_g[l][None, :],
        "wqt": peer_wq[l].astype(BF16).T,
        "keys_hi": keys_hi,
        "keys_lo": (keys - keys_hi.astype(F32)).astype(BF16),
        "peer_u": (peer_u[l] * GELU_IN_SCALE).astype(BF16),
        "peer_vt": peer_v[l].astype(BF16).T,
        "normf_g": normf_g[None, :],
    }
    return (_trunk(x_prompt, p), _trunk(x_sample, p))
```

```python
import functools
import math

import jax
import jax.numpy as jnp
import numpy as np
from jax import lax
from jax.experimental import pallas as pl
from jax.experimental.pallas import tpu as pltpu

F32 = jnp.float32
BF16 = jnp.bfloat16

D_MODEL = 1024
HY_WIDTH = 512
ATT_WIDTH = 512
N_ATT_HEADS = 4
ATT_HEAD_DIM = 64
ATT_V_DIM = 2 * ATT_HEAD_DIM
ROT_DIM = ATT_HEAD_DIM // 4
ROPE_THETA = 500000.0
IN_WIDTH = 3 * HY_WIDTH + 3 * ATT_WIDTH
FILTER_BANDS = 8
FILTER_EMB = 1 + 2 * FILTER_BANDS
FAST_DECAY_PCT = 0.3
SLOW_DECAY_PCT = 1.5
DECAY_TARGET = 1e-2
PEER_HEADS = 8
PEER_KEYS = 128
PEER_EXPERTS = PEER_KEYS * PEER_KEYS
PEER_TOPK = 16
PEER_HALF = 128
EPS = 1e-6
SUBLN_EPS = 1e-5
LAMBDA_INIT = 0.8 - 0.6 * math.exp(-0.3 * 0)

LANES = 128
SUBLANES = 8
VMEM_LIMIT = 56 * 1024 * 1024
FFT_N2 = 128
VT_ROWS = ATT_V_DIM + 2 * SUBLANES
STABILISER_HEADROOM = 64.0
NEG_INF = float("-inf")


def _params(semantics, vmem=VMEM_LIMIT):
    return pltpu.CompilerParams(dimension_semantics=semantics, vmem_limit_bytes=vmem)


def _rms(x, g, eps):
    return x * lax.rsqrt(jnp.mean(x * x, axis=-1, keepdims=True) + eps) * g


def _store_time_blocks(dst_ref, x, lead=()):
    rows, c = x.shape
    for j in range(rows // LANES):
        for cb in range(c // LANES):
            dst_ref[lead + (j, slice(cb * LANES, (cb + 1) * LANES), slice(None))] = (
                x[j * LANES:(j + 1) * LANES, cb * LANES:(cb + 1) * LANES].T)


def _inproj_kernel(x_ref, xp_ref, xn_ref, g_ref, w_ref, c_ref, sa_ref, sb_ref, cw_ref, cb_ref,
                   ut_ref, x0_ref, qt1_ref, qt2_ref, k_ref, vt_ref, *, blocks_per_seq):
    i = pl.program_id(0)
    tm = x_ref.shape[0]
    hy3 = 3 * HY_WIDTH
    x_all = jnp.concatenate([xp_ref[...], x_ref[...], xn_ref[...]], axis=0)
    h = _rms(x_all, g_ref[...], EPS).astype(BF16)
    z_all = jnp.dot(h, w_ref[...], preferred_element_type=F32)
    z = z_all[SUBLANES:SUBLANES + tm]

    zh = z[:, :hy3]
    row = lax.broadcasted_iota(jnp.int32, zh.shape, 0)
    at_start = (i % blocks_per_seq) == 0
    at_end = (i % blocks_per_seq) == blocks_per_seq - 1
    prev_row = jnp.where(at_start, 0.0, z_all[SUBLANES - 1:SUBLANES, :hy3])
    next_row = jnp.where(at_end, 0.0, z_all[SUBLANES + tm:SUBLANES + tm + 1, :hy3])
    zm = jnp.where(row == 0, prev_row, pltpu.roll(zh, 1, 0))
    zp = jnp.where(row == tm - 1, next_row, pltpu.roll(zh, tm - 1, 0))
    cw = cw_ref[...]
    conv = zm * cw[0:1] + zh * cw[1:2] + zp * cw[2:3] + cb_ref[...]
    x0_ref[...] = conv[:, :HY_WIDTH]
    _store_time_blocks(ut_ref, conv[:, 2 * HY_WIDTH:] * conv[:, HY_WIDTH:2 * HY_WIDTH])

    c, sa, sb = c_ref[...], sa_ref[...], sb_ref[...]
    scale = ATT_HEAD_DIM ** -0.5 * math.log2(math.e)

    def rope(xb):
        return (xb * c + pltpu.roll(xb, LANES - ROT_DIM // 2, 1) * sa
                + pltpu.roll(xb, ROT_DIM // 2, 1) * sb)

    first = lax.broadcasted_iota(jnp.int32, (LANES, tm), 0) < ATT_HEAD_DIM
    for hd in range(N_ATT_HEADS):
        lo, hi = hd * LANES, (hd + 1) * LANES
        qt = (rope(z[:, hy3 + lo:hy3 + hi]) * scale).T
        qt1_ref[lo:hi, :] = jnp.where(first, qt, 0.0).astype(BF16)
        qt2_ref[lo:hi, :] = jnp.where(first, 0.0, qt).astype(BF16)
        k_ref[:, lo:hi] = rope(z[:, hy3 + ATT_WIDTH + lo:hy3 + ATT_WIDTH + hi]).astype(BF16)
        vlo = hd * VT_ROWS
        vt_ref[vlo:vlo + ATT_V_DIM, :] = (
            z[:, hy3 + 2 * ATT_WIDTH + lo:hy3 + 2 * ATT_WIDTH + hi].T.astype(BF16))
        vt_ref[vlo + ATT_V_DIM:vlo + VT_ROWS, :] = jnp.ones((VT_ROWS - ATT_V_DIM, tm), BF16)


def _inproj(x, g, w_bf, rope_c, rope_sa, rope_sb, conv_w, conv_b, seq_len, tm=512):
    t = x.shape[0]
    nseq = seq_len // tm
    per8 = tm // SUBLANES
    last8 = t // SUBLANES - 1
    row = lambda i: (i, 0)
    col = lambda i: (0, i)
    fixed = lambda i: (0, 0)
    pos = lambda i: (i % nseq, 0)
    hy3 = 3 * HY_WIDTH
    tr_spec = pl.BlockSpec((ATT_WIDTH, tm), col)
    tr_shape = jax.ShapeDtypeStruct((ATT_WIDTH, t), BF16)
    return pl.pallas_call(
        functools.partial(_inproj_kernel, blocks_per_seq=nseq),
        grid=(t // tm,),
        in_specs=[pl.BlockSpec((tm, D_MODEL), row),
                  pl.BlockSpec((SUBLANES, D_MODEL), lambda i: (jnp.maximum(i * per8 - 1, 0), 0)),
                  pl.BlockSpec((SUBLANES, D_MODEL), lambda i: (jnp.minimum((i + 1) * per8, last8), 0)),
                  pl.BlockSpec((1, D_MODEL), fixed),
                  pl.BlockSpec((D_MODEL, IN_WIDTH), fixed),
                  pl.BlockSpec((tm, LANES), pos),
                  pl.BlockSpec((tm, LANES), pos),
                  pl.BlockSpec((tm, LANES), pos),
                  pl.BlockSpec((3, hy3), fixed),
                  pl.BlockSpec((1, hy3), fixed)],
        out_specs=[pl.BlockSpec((tm // LANES, HY_WIDTH, LANES), lambda i: (i, 0, 0)),
                   pl.BlockSpec((tm, HY_WIDTH), row), tr_spec, tr_spec,
                   pl.BlockSpec((tm, ATT_WIDTH), row),
                   pl.BlockSpec((N_ATT_HEADS * VT_ROWS, tm), col)],
        out_shape=[jax.ShapeDtypeStruct((t // LANES, HY_WIDTH, LANES), F32),
                   jax.ShapeDtypeStruct((t, HY_WIDTH), F32), tr_shape, tr_shape,
                   jax.ShapeDtypeStruct((t, ATT_WIDTH), BF16),
                   jax.ShapeDtypeStruct((N_ATT_HEADS * VT_ROWS, t), BF16)],
        compiler_params=_params(("parallel",)),
        name="inproj",
    )(x, x, x, g, w_bf, rope_c, rope_sa, rope_sb, conv_w, conv_b)


def _filter_kernel(f_ref, w1_ref, b1_ref, f1_ref, w2_ref, b2_ref, f2_ref, w3_ref, b3_ref,
                   dl_ref, h_ref):
    i = pl.program_id(0)
    hp = lax.Precision.HIGHEST
    feats = f_ref[...]
    tl = feats.shape[0]
    h = jnp.sin(f1_ref[...] * (jnp.dot(feats, w1_ref[...], precision=hp,
                                       preferred_element_type=F32) + b1_ref[...]))
    h = jnp.sin(f2_ref[...] * (jnp.dot(h, w2_ref[...], precision=hp,
                                       preferred_element_type=F32) + b2_ref[...]))
    h3 = jnp.dot(h, w3_ref[...], precision=hp, preferred_element_type=F32) + b3_ref[...]
    decay = jnp.exp(-feats[:, 0:1] * dl_ref[...])
    _store_time_blocks(h_ref, h3[:, :HY_WIDTH] * decay, lead=(0,))
    row = lax.broadcasted_iota(jnp.int32, (tl, HY_WIDTH), 0) + i * tl
    _store_time_blocks(h_ref, jnp.where(row == 0, 0.0, h3[:, HY_WIDTH:] * decay), lead=(1,))


def _pad2(a, rows, cols):
    return jnp.pad(a, ((0, rows - a.shape[0]), (0, cols - a.shape[1])))


def _hyena_filters(seq_len, w1, b1, f1, w2, b2, f2, w3, b3, tl=1024):
    f32 = np.float32
    t = np.linspace(0.0, 1.0, seq_len).astype(f32)[:, None]
    w = f32(2.0 * math.pi) * np.arange(seq_len, dtype=f32) / f32(seq_len)
    bands = np.linspace(1e-4, FILTER_BANDS - 1, FILTER_BANDS).astype(f32)
    ang = (w[:, None] * bands[None, :]).astype(np.float64)
    feats_np = np.zeros((seq_len, LANES), f32)
    feats_np[:, :FILTER_EMB] = np.concatenate([t, np.cos(ang), -np.sin(ang)], axis=-1)
    feats = jnp.asarray(feats_np)
    max_decay = math.log(DECAY_TARGET) / FAST_DECAY_PCT
    min_decay = math.log(DECAY_TARGET) / SLOW_DECAY_PCT
    deltas = jnp.abs(jnp.linspace(min_decay, max_decay, HY_WIDTH, dtype=F32))[None, :]
    w1p = _pad2(w1, LANES, LANES)
    w2p = _pad2(w2, LANES, LANES)
    w3p = _pad2(w3, LANES, 2 * HY_WIDTH)
    pad1 = lambda v: _pad2(v[None, :], 1, LANES)
    fixed = lambda i: (0, 0)
    vec = pl.BlockSpec((1, LANES), fixed)
    mat = pl.BlockSpec((LANES, LANES), fixed)
    return pl.pallas_call(
        _filter_kernel,
        grid=(seq_len // tl,),
        in_specs=[pl.BlockSpec((tl, LANES), lambda i: (i, 0)), mat, vec, vec, mat, vec, vec,
                  pl.BlockSpec((LANES, 2 * HY_WIDTH), fixed),
                  pl.BlockSpec((1, 2 * HY_WIDTH), fixed),
                  pl.BlockSpec((1, HY_WIDTH), fixed)],
        out_specs=pl.BlockSpec((2, tl // LANES, HY_WIDTH, LANES), lambda i: (0, i, 0, 0)),
        out_shape=jax.ShapeDtypeStruct((2, seq_len // LANES, HY_WIDTH, LANES), F32),
        compiler_params=_params(("parallel",)),
        name="hyena_filter",
    )(feats, w1p, pad1(b1), pad1(f1), w2p, pad1(b2), pad1(f2), w3p, b3[None, :], deltas)


def _dft_tables(seq_len):
    n = 2 * seq_len
    n2 = FFT_N2
    n1 = n // n2
    n1h = n1 // 2
    a1 = (2.0 * np.pi / n1) * np.outer(np.arange(n1), np.arange(n1))
    f1r, f1i = np.cos(a1), -np.sin(a1)
    fwd1 = np.concatenate([f1r[:, :n1h], f1i[:, :n1h]], axis=0)
    inv1 = np.concatenate([f1r[:n1h, :], f1i[:n1h, :]], axis=1) / n
    a2 = (2.0 * np.pi / n2) * np.outer(np.arange(n2), np.arange(n2))
    f2 = np.concatenate([np.cos(a2), -np.sin(a2)], axis=1)
    at = (2.0 * np.pi / n) * np.outer(np.arange(n1), np.arange(n2))
    return (jnp.asarray(fwd1, BF16), jnp.asarray(inv1, BF16), jnp.asarray(f2, BF16),
            jnp.asarray(np.cos(at), F32), jnp.asarray(-np.sin(at), F32))


def _forward_spectra(tiles, fwd1, twr, twi, f2):
    n1, w = twr.shape
    tr, ti = _tile_lanes(twr, 2 * w), _tile_lanes(twi, 2 * w)
    firsts = [jnp.dot(fwd1, jnp.concatenate([xa, xb], axis=1).astype(BF16),
                      preferred_element_type=F32) for xa, xb in tiles]
    twiddled = []
    for a in firsts:
        ar, ai = a[:n1], a[n1:]
        zr, zi = ar * tr - ai * ti, ar * ti + ai * tr
        twiddled.append(jnp.concatenate([zr[:, :w], zi[:, :w], zr[:, w:], zi[:, w:]],
                                        axis=0).astype(BF16))
    seconds = [jnp.dot(z, f2, preferred_element_type=F32) for z in twiddled]

    def spectrum(q):
        qr, qi = q[:n1], q[n1:]
        return qr[:, :w] - qi[:, w:], qr[:, w:] + qi[:, :w]

    return [(spectrum(p[:2 * n1]), spectrum(p[2 * n1:])) for p in seconds]


def _inverse_signals(spectra, inv1, twr, twi, f2):
    n1, w = twr.shape
    firsts = [jnp.dot(jnp.concatenate([wa[0], wa[1], wb[0], wb[1]], axis=0).astype(BF16), f2,
                      preferred_element_type=F32) for wa, wb in spectra]

    def untwiddle(r):
        rr, ri = r[:n1], r[n1:]
        vr, vi = rr[:, :w] + ri[:, w:], ri[:, :w] - rr[:, w:]
        return vr * twr + vi * twi, vi * twr - vr * twi

    stacked = []
    for q in firsts:
        (bra, bia), (brb, bib) = untwiddle(q[:2 * n1]), untwiddle(q[2 * n1:])
        stacked.append(jnp.concatenate([jnp.concatenate([bra, brb], axis=1),
                                        jnp.concatenate([bia, bib], axis=1)],
                                       axis=0).astype(BF16))
    outs = [jnp.dot(inv1, r, preferred_element_type=F32) for r in stacked]
    return [(y[:, :w], y[:, w:]) for y in outs]


def _tile_lanes(x, width):
    return jnp.concatenate([x] * (width // x.shape[1]), axis=1)


def _spectrum_kernel(h_ref, fwd1_ref, f2_ref, twr_ref, twi_ref, kr_ref, ki_ref):
    fwd1, f2, twr, twi = fwd1_ref[...], f2_ref[...], twr_ref[...], twi_ref[...]
    cs = range(h_ref.shape[2])
    out = _forward_spectra([(h_ref[0, :, c, :], h_ref[1, :, c, :]) for c in cs], fwd1, twr, twi, f2)
    for c, ((fr, fi), (gr, gi)) in zip(cs, out):
        kr_ref[c] = (fr + gr).astype(BF16)
        ki_ref[c] = (fi - gi).astype(BF16)


def _filter_spectrum(h3, tables, cb=8):
    fwd1, _, f2, twr, twi = tables
    _, n1h, c, w = h3.shape
    n1 = 2 * n1h
    const = lambda a: pl.BlockSpec(a.shape, lambda i: (0,) * a.ndim)
    o_spec = pl.BlockSpec((cb, n1, w), lambda i: (i, 0, 0))
    return pl.pallas_call(
        _spectrum_kernel,
        grid=(c // cb,),
        in_specs=[pl.BlockSpec((2, n1h, cb, w), lambda i: (0, 0, i, 0)),
                  const(fwd1), const(f2), const(twr), const(twi)],
        out_specs=[o_spec, o_spec],
        out_shape=[jax.ShapeDtypeStruct((c, n1, w), BF16)] * 2,
        compiler_params=_params(("parallel",)),
        name="filter_spectrum",
    )(h3, fwd1, f2, twr, twi)


def _longconv_kernel(u_ref, kr_ref, ki_ref, skip_ref, fwd1_ref, inv1_ref, f2_ref, twr_ref,
                     twi_ref, y_ref):
    fwd1, inv1, f2 = fwd1_ref[...], inv1_ref[...], f2_ref[...]
    twr, twi = twr_ref[...], twi_ref[...]

    def filtered(c, spec):
        kr, ki = kr_ref[c].astype(F32), ki_ref[c].astype(F32)
        return spec[0] * kr - spec[1] * ki, spec[0] * ki + spec[1] * kr

    cs = range(0, u_ref.shape[1], 2)
    xs = [(u_ref[:, c, :], u_ref[:, c + 1, :]) for c in cs]
    spectra = _forward_spectra(xs, fwd1, twr, twi, f2)
    prods = [(filtered(c, sa), filtered(c + 1, sb)) for c, (sa, sb) in zip(cs, spectra)]
    for c, (xa, xb), (ya, yb) in zip(cs, xs, _inverse_signals(prods, inv1, twr, twi, f2)):
        y_ref[c] = ya + skip_ref[c] * xa
        y_ref[c + 1] = yb + skip_ref[c + 1] * xb


def _longconv(ut3, kr, ki, skip3, tables, batch, cb=16):
    fwd1, inv1, f2, twr, twi = tables
    rows, c, w = ut3.shape
    n1h = rows // batch
    n1 = 2 * n1h
    const = lambda a: pl.BlockSpec(a.shape, lambda i, b: (0,) * a.ndim)
    spec = pl.BlockSpec((cb, n1, w), lambda i, b: (i, 0, 0))
    return pl.pallas_call(
        _longconv_kernel,
        grid=(c // cb, batch),
        in_specs=[pl.BlockSpec((n1h, cb, w), lambda i, b: (b, i, 0)), spec, spec,
                  pl.BlockSpec((cb, 1, w), lambda i, b: (i, 0, 0)),
                  const(fwd1), const(inv1), const(f2), const(twr), const(twi)],
        out_specs=pl.BlockSpec((cb, n1h, w), lambda i, b: (i, b, 0)),
        out_shape=jax.ShapeDtypeStruct((c, rows, w), F32),
        compiler_params=_params(("parallel", "parallel")),
        name="long_conv",
    )(ut3, kr, ki, skip3, fwd1, inv1, f2, twr, twi)


def _gate_kernel(y_ref, x0_ref, o_ref):
    rows, c = x0_ref.shape
    for j in range(rows // LANES):
        for cb in range(c // LANES):
            r, l = slice(j * LANES, (j + 1) * LANES), slice(cb * LANES, (cb + 1) * LANES)
            o_ref[r, l] = (x0_ref[r, l] * y_ref[l, j, :].T).astype(BF16)


def _gate(y3, x0, tl=1024):
    t, c = x0.shape
    return pl.pallas_call(
        _gate_kernel,
        grid=(t // tl,),
        in_specs=[pl.BlockSpec((c, tl // LANES, LANES), lambda i: (0, i, 0)),
                  pl.BlockSpec((tl, c), lambda i: (i, 0))],
        out_specs=pl.BlockSpec((tl, c), lambda i: (i, 0)),
        out_shape=jax.ShapeDtypeStruct((t, c), BF16),
        compiler_params=_params(("parallel",)),
        name="hyena_gate",
    )(y3, x0)


def _hyena_long_conv(ut3, x0, filters, skip, batch, seq_len):
    tables = _dft_tables(seq_len)
    kr, ki = _filter_spectrum(filters, tables)
    skip3 = jnp.broadcast_to(skip[:, None, None], (HY_WIDTH, 1, LANES))
    return _gate(_longconv(ut3, kr, ki, skip3, tables, batch), x0)


def _pipelined(units, scores, softmax, values):
    s_next = scores(units[0])
    pending = None
    for i, unit in enumerate(units):
        s_cur = s_next
        if i + 1 < len(units):
            s_next = scores(units[i + 1])
        if pending is not None:
            values(*pending)
        pending = (unit,) + softmax(unit, s_cur)
    values(*pending)


def _attn_kernel(q1_ref, q2_ref, k_ref, vt_ref, lam_ref, sg_ref, o_ref,
                 m1, a1, m2, a2, *, chunk):
    ki = pl.program_id(3)
    q_refs = (q1_ref, q2_ref)
    m_refs = (m1, m2)
    a_refs = (a1, a2)
    n_chunks = k_ref.shape[0] // chunk
    units = [(c, comp) for c in range(n_chunks) for comp in (0, 1)]

    def scores(unit):
        c, comp = unit
        return jnp.dot(k_ref[c * chunk:(c + 1) * chunk, :], q_refs[comp][...],
                       preferred_element_type=F32)

    def value_dot(unit, p):
        c = unit[0]
        return jnp.dot(vt_ref[:, c * chunk:(c + 1) * chunk], p, preferred_element_type=F32)

    @pl.when(ki == 0)
    def _():
        for comp in (0, 1):
            head = jnp.dot(k_ref[0:2 * SUBLANES, :], q_refs[comp][...], preferred_element_type=F32)
            m_refs[comp][...] = jnp.max(head, axis=0, keepdims=True)
            a_refs[comp][...] = jnp.zeros(a_refs[comp].shape, F32)

    m_start = [m1[...], m2[...]]
    step_max = [None, None]
    step_acc = [None, None]

    def fast_softmax(unit, s):
        comp = unit[1]
        cm = jnp.max(s, axis=0, keepdims=True)
        step_max[comp] = cm if step_max[comp] is None else jnp.maximum(step_max[comp], cm)
        return (jnp.exp2(s - m_start[comp]).astype(BF16),)

    def fast_values(unit, p):
        comp = unit[1]
        d = value_dot(unit, p)
        step_acc[comp] = d if step_acc[comp] is None else step_acc[comp] + d

    _pipelined(units, scores, fast_softmax, fast_values)
    excess = jnp.maximum(step_max[0] - m_start[0], step_max[1] - m_start[1])
    fast_ok = jnp.max(excess) <= STABILISER_HEADROOM

    @pl.when(fast_ok)
    def _():
        for comp in (0, 1):
            m_new = jnp.maximum(m_start[comp], step_max[comp])
            a_refs[comp][...] = jnp.exp2(m_start[comp] - m_new) * (a_refs[comp][...] + step_acc[comp])
            m_refs[comp][...] = m_new

    @pl.when(jnp.logical_not(fast_ok))
    def _():
        stats = [[m1[...], a1[...]], [m2[...], a2[...]]]

        def softmax(unit, s):
            m_prev = stats[unit[1]][0]
            m_new = jnp.maximum(m_prev, jnp.max(s, axis=0, keepdims=True))
            stats[unit[1]][0] = m_new
            return jnp.exp2(s - m_new).astype(BF16), jnp.exp2(m_prev - m_new)

        def values(unit, p, alpha):
            comp = unit[1]
            stats[comp][1] = alpha * stats[comp][1] + value_dot(unit, p)

        _pipelined(units, scores, softmax, values)
        for comp in (0, 1):
            m_refs[comp][...] = stats[comp][0]
            a_refs[comp][...] = stats[comp][1]

    @pl.when(ki == pl.num_programs(3) - 1)
    def _():
        lv = lam_ref[...]
        lam = (jnp.exp(jnp.sum(lv[0:1] * lv[1:2], axis=1, keepdims=True))
               - jnp.exp(jnp.sum(lv[2:3] * lv[3:4], axis=1, keepdims=True)) + LAMBDA_INIT)
        d = ATT_V_DIM
        o = (a1[:d, :] / a1[d:d + 1, :] - lam * (a2[:d, :] / a2[d:d + 1, :]))
        inv = lax.rsqrt(jnp.mean(o * o, axis=0, keepdims=True) + SUBLN_EPS)
        g = pltpu.repeat(sg_ref[...], o.shape[1] // LANES, 1)
        o_ref[...] = (o * inv * g * (1.0 - LAMBDA_INIT)).astype(BF16)


def _attention(qt1, qt2, k, vt, lam_vecs, subln_g, batch, seq_len, tq=1024, tk=4096, chunk=512):
    tk = min(tk, seq_len)
    nq, nk = seq_len // tq, seq_len // tk
    q_spec = pl.BlockSpec((LANES, tq), lambda b, h, i, j: (h, b * nq + i))
    sg = jnp.broadcast_to(subln_g[:, None], (ATT_V_DIM, LANES))
    return pl.pallas_call(
        functools.partial(_attn_kernel, chunk=chunk),
        grid=(batch, N_ATT_HEADS, nq, nk),
        in_specs=[q_spec, q_spec,
                  pl.BlockSpec((tk, LANES), lambda b, h, i, j: (b * nk + j, h)),
                  pl.BlockSpec((VT_ROWS, tk), lambda b, h, i, j: (h, b * nk + j)),
                  pl.BlockSpec((4, ATT_HEAD_DIM), lambda b, h, i, j: (0, 0)),
                  pl.BlockSpec((ATT_V_DIM, LANES), lambda b, h, i, j: (0, 0))],
        out_specs=q_spec,
        out_shape=jax.ShapeDtypeStruct((ATT_WIDTH, batch * seq_len), BF16),
        scratch_shapes=[pltpu.VMEM((1, tq), F32), pltpu.VMEM((VT_ROWS, tq), F32)] * 2,
        compiler_params=_params(("parallel", "parallel", "parallel", "arbitrary")),
        name="diff_attention",
    )(qt1, qt2, k, vt, lam_vecs, sg)


def _split_bf16(x):
    hi = x.astype(BF16)
    return hi, (x - hi.astype(F32)).astype(BF16)


def _outproj_kernel(yh_ref, ya_ref, x_ref, wo_ref, g_ref, wqt_ref, kh_ref, kl_ref,
                    x2_ref, xnt_ref, st_ref):
    wo = wo_ref[...]
    x2 = (x_ref[...] + jnp.dot(yh_ref[...], wo[:HY_WIDTH], preferred_element_type=F32)
          + lax.dot_general(ya_ref[...], wo[HY_WIDTH:], (((0,), (0,)), ((), ())),
                            preferred_element_type=F32))
    x2_ref[...] = x2
    xn = _rms(x2, g_ref[...], EPS)
    xnt = xn.T.astype(BF16)
    xnt_ref[...] = xnt
    qt = jnp.dot(wqt_ref[...], xnt, preferred_element_type=F32)
    for hc in range(2 * PEER_HEADS):
        qh, ql = _split_bf16(qt[hc * PEER_HALF:(hc + 1) * PEER_HALF, :])
        kh, kl = kh_ref[hc], kl_ref[hc]
        st_ref[hc * PEER_KEYS:(hc + 1) * PEER_KEYS, :] = (
            jnp.dot(kh, qh, preferred_element_type=F32)
            + jnp.dot(kh, ql, preferred_element_type=F32)
            + jnp.dot(kl, qh, preferred_element_type=F32))


def _outproj(yh, ya, x, wo_bf, g2, wqt_bf, keys_hi, keys_lo, tm=512):
    t = x.shape[0]
    qw = 2 * PEER_HEADS * PEER_HALF
    sw = 2 * PEER_HEADS * PEER_KEYS
    row = lambda i: (i, 0)
    fixed = lambda i: (0, 0)
    fixed3 = lambda i: (0, 0, 0)
    return pl.pallas_call(
        _outproj_kernel,
        grid=(t // tm,),
        in_specs=[pl.BlockSpec((tm, HY_WIDTH), row),
                  pl.BlockSpec((ATT_WIDTH, tm), lambda i: (0, i)),
                  pl.BlockSpec((tm, D_MODEL), row),
                  pl.BlockSpec((HY_WIDTH + ATT_WIDTH, D_MODEL), fixed),
                  pl.BlockSpec((1, D_MODEL), fixed),
                  pl.BlockSpec((qw, D_MODEL), fixed),
                  pl.BlockSpec((2 * PEER_HEADS, PEER_KEYS, PEER_HALF), fixed3),
                  pl.BlockSpec((2 * PEER_HEADS, PEER_KEYS, PEER_HALF), fixed3)],
        out_specs=[pl.BlockSpec((tm, D_MODEL), row), pl.BlockSpec((D_MODEL, tm), lambda i: (0, i)),
                   pl.BlockSpec((sw, tm), lambda i: (0, i))],
        out_shape=[jax.ShapeDtypeStruct((t, D_MODEL), F32),
                   jax.ShapeDtypeStruct((D_MODEL, t), BF16),
                   jax.ShapeDtypeStruct((sw, t), F32)],
        compiler_params=_params(("parallel",)),
        name="outproj_peer_query",
    )(yh, ya, x, wo_bf, g2, wqt_bf, keys_hi, keys_lo)


def _extract_max(pieces, iotas, total, exact):
    m = pieces[0]
    for p in pieces[1:]:
        m = jnp.maximum(m, p)
    m = jnp.max(m, axis=0, keepdims=True)
    hot = [p == m for p in pieces]
    if exact:
        cand = [jnp.where(h, io, total) for h, io in zip(hot, iotas)]
        idx = cand[0]
        for c in cand[1:]:
            idx = jnp.minimum(idx, c)
        idx = jnp.min(idx, axis=0, keepdims=True)
        hot = [io == idx for io in iotas]
    return m, hot, [jnp.where(h, NEG_INF, p) for h, p in zip(hot, pieces)]


def _topk_rank(s, exact, want_rank):
    rank = jnp.full(s.shape, float(PEER_TOPK), F32) if want_rank else None
    iota = [lax.broadcasted_iota(jnp.int32, s.shape, 0)]
    vals = []
    for r in range(PEER_TOPK):
        m, hot, (s,) = _extract_max([s], iota, s.shape[0], exact)
        vals.append(m)
        if want_rank:
            rank = jnp.where(hot[0], float(r), rank)
    return vals, rank


_PAIRS = [(k0, k1) for k0 in range(PEER_TOPK) for k1 in range(PEER_TOPK // (k0 + 1))]
_PAIR_ROWS = -(-len(_PAIRS) // SUBLANES) * SUBLANES


def _rows_to_tiles(rows, width):
    sub = lax.broadcasted_iota(jnp.int32, (SUBLANES, width), 0)
    tiles = []
    for t0 in range(0, len(rows), SUBLANES):
        tile = jnp.full((SUBLANES, width), NEG_INF, F32)
        for r in range(SUBLANES):
            if rows[t0 + r] is not None:
                tile = jnp.where(sub == r, rows[t0 + r], tile)
        tiles.append(tile)
    return tiles


def _route_tables(s0, s1, exact):
    tt = s0.shape[1]
    v0, rank0 = _topk_rank(s0, exact, want_rank=exact)
    v1, rank1 = _topk_rank(s1, exact, want_rank=True)
    pad = [None] * (_PAIR_ROWS - len(_PAIRS))
    tiles = [a + b for a, b in zip(_rows_to_tiles([v0[k0] for k0, _ in _PAIRS] + pad, tt),
                                   _rows_to_tiles([v1[k1] for _, k1 in _PAIRS] + pad, tt))]
    iotas = [lax.broadcasted_iota(jnp.int32, (SUBLANES, tt), 0) + t0
             for t0 in range(0, _PAIR_ROWS, SUBLANES)]
    cmax = v0[0] + v1[0]
    z = None
    for r in range(PEER_TOPK):
        m, _, tiles = _extract_max(tiles, iotas, _PAIR_ROWS, exact)
        z = jnp.exp(m - cmax) if z is None else z + jnp.exp(m - cmax)
    cnt = jnp.zeros(s0.shape, F32)
    total = None
    for k0 in range(PEER_TOPK):
        rows = [r for r, (a, _) in enumerate(_PAIRS) if a == k0]
        taken = None
        for ti in sorted({r // SUBLANES for r in rows}):
            inside = (iotas[ti] >= rows[0]) & (iotas[ti] <= rows[-1])
            part = jnp.sum(jnp.where(inside & (tiles[ti] == NEG_INF), 1.0, 0.0),
                           axis=0, keepdims=True)
            taken = part if taken is None else taken + part
        total = taken if total is None else total + taken
        cnt = jnp.where((rank0 == float(k0)) if exact else (s0 == v0[k0]), taken, cnt)
    k = float(PEER_TOPK)
    count = lambda mask: jnp.sum(jnp.where(mask, 1.0, 0.0), axis=0, keepdims=True)
    top0 = (rank0 < k) if exact else (s0 >= v0[PEER_TOPK - 1])
    ok = (total == k) & (count(top0) == k) & (count(rank1 < k) == k)
    return cnt, jnp.exp(s0 - v0[0]) / z, rank1, jnp.exp(s1 - v1[0]), ok


def _route_kernel(st_ref, cnt_ref, w0_ref, r1_ref, e1_ref):
    def emit(exact):
        cnt, w0, rank1, e1, ok = _route_tables(st_ref[:PEER_KEYS, :], st_ref[PEER_KEYS:, :], exact)
        cnt_ref[0] = cnt
        w0_ref[0] = w0 * GELU_OUT_SCALE
        r1_ref[0] = rank1.astype(BF16)
        e1_ref[0] = e1.astype(BF16)
        return ok

    ok = emit(exact=False)
    clean = jnp.min(jnp.where(ok, 1, 0))

    @pl.when(clean == 0)
    def _():
        emit(exact=True)


def _route(st, tt=512):
    sw, t = st.shape
    o_spec = pl.BlockSpec((1, PEER_KEYS, tt), lambda i, h: (h, 0, i))
    tab = lambda dt: jax.ShapeDtypeStruct((PEER_HEADS, PEER_KEYS, t), dt)
    return pl.pallas_call(
        _route_kernel,
        grid=(t // tt, PEER_HEADS),
        in_specs=[pl.BlockSpec((2 * PEER_KEYS, tt), lambda i, h: (h, i))],
        out_specs=[o_spec] * 4,
        out_shape=[tab(F32), tab(F32), tab(BF16), tab(BF16)],
        compiler_params=_params(("parallel", "parallel")),
        name="peer_route",
    )(st)


GELU_IN_SCALE = 2.0 ** -0.5
GELU_OUT_SCALE = 0.5 * 2.0 ** 0.5


def _gelu_core(y):
    return y * (1.0 + lax.erf(y))


def _peer_kernel(xnt_ref, u_ref, vt_ref, cnt_ref, w0_ref, r1_ref, e1_ref, x2_ref, gf_ref,
                 y_ref, acc_ref, *, rows_per_step, rows_per_unit):
    e = pl.program_id(1)

    @pl.when(e == 0)
    def _():
        acc_ref[...] = jnp.zeros(acc_ref.shape, F32)

    tt = xnt_ref.shape[1]
    unit = rows_per_unit * PEER_KEYS
    n_units = rows_per_step // rows_per_unit

    def hidden(s):
        return jnp.dot(u_ref[s * unit:(s + 1) * unit, :], xnt_ref[...],
                       preferred_element_type=F32)

    def activation(s, hid):
        gates = []
        for ii in range(s * rows_per_unit, (s + 1) * rows_per_unit):
            g = None
            for h in range(PEER_HEADS):
                cnt_row = jnp.broadcast_to(cnt_ref[h, ii:ii + 1, :], (PEER_KEYS, tt)).astype(BF16)
                w_row = jnp.broadcast_to(w0_ref[h, ii:ii + 1, :], (PEER_KEYS, tt)).astype(BF16)
                term = jnp.where(r1_ref[h] < cnt_row, e1_ref[h] * w_row, jnp.zeros((), BF16))
                g = term if g is None else g + term
            gates.append(g)
        return _gelu_core(hid).astype(BF16) * jnp.concatenate(gates, axis=0)

    def mix(s, act):
        return jnp.dot(vt_ref[:, s * unit:(s + 1) * unit], act, preferred_element_type=F32)

    acc = acc_ref[...]
    h_next = hidden(0)
    pending = None
    for s in range(n_units):
        h_cur = h_next
        if s + 1 < n_units:
            h_next = hidden(s + 1)
        if pending is not None:
            acc = acc + mix(*pending)
        pending = (s, activation(s, h_cur))
    acc_ref[...] = acc + mix(*pending)

    @pl.when(e == pl.num_programs(1) - 1)
    def _():
        x3 = x2_ref[...] + acc_ref[...].T
        y_ref[...] = _rms(x3, gf_ref[...], EPS)


def _peer(xnt, u_bf, vt_bf, cnt, w0, r1, e1, x2, gf, tt=512, rows_per_step=16, rows_per_unit=2):
    t = xnt.shape[1]
    te = rows_per_step * PEER_KEYS
    tok = lambda i, e: (i, 0)
    tab_i = pl.BlockSpec((PEER_HEADS, rows_per_step, tt), lambda i, e: (0, e, i))
    tab_j = pl.BlockSpec((PEER_HEADS, PEER_KEYS, tt), lambda i, e: (0, 0, i))
    return pl.pallas_call(
        functools.partial(_peer_kernel, rows_per_step=rows_per_step,
                          rows_per_unit=rows_per_unit),
        grid=(t // tt, PEER_EXPERTS // te),
        in_specs=[pl.BlockSpec((D_MODEL, tt), lambda i, e: (0, i)),
                  pl.BlockSpec((te, D_MODEL), lambda i, e: (e, 0)),
                  pl.BlockSpec((D_MODEL, te), lambda i, e: (0, e)),
                  tab_i, tab_i, tab_j, tab_j,
                  pl.BlockSpec((tt, D_MODEL), tok),
                  pl.BlockSpec((1, D_MODEL), lambda i, e: (0, 0))],
        out_specs=pl.BlockSpec((tt, D_MODEL), tok),
        out_shape=jax.ShapeDtypeStruct((t, D_MODEL), F32),
        scratch_shapes=[pltpu.VMEM((D_MODEL, tt), F32)],
        compiler_params=_params(("parallel", "arbitrary")),
        name="peer_dense",
    )(xnt, u_bf, vt_bf, cnt, w0, r1, e1, x2, gf)


def _rope_tables(seq_len):
    f32 = np.float32
    inv = f32(ROPE_THETA) ** (-np.arange(0, ROT_DIM, 2, dtype=f32) / f32(ROT_DIM))
    ang = (np.arange(seq_len, dtype=f32)[:, None] * inv[None, :]).astype(np.float64)
    cos, sin = np.cos(ang), np.sin(ang)
    half = ROT_DIM // 2
    comp = np.zeros((3, seq_len, ATT_HEAD_DIM))
    comp[0, :, :half], comp[0, :, half:ROT_DIM], comp[0, :, ROT_DIM:] = cos, cos, 1.0
    comp[1, :, :half] = -sin
    comp[2, :, half:ROT_DIM] = sin
    both = np.concatenate([comp, comp], axis=2).astype(f32)
    return jnp.asarray(both[0]), jnp.asarray(both[1]), jnp.asarray(both[2])


def _trunk(x, p):
    batch, seq_len, _ = x.shape
    xf = x.reshape(batch * seq_len, D_MODEL)
    rope_c, rope_sa, rope_sb = _rope_tables(seq_len)
    u, x0, qt1, qt2, k, vt = _inproj(xf, p["norm1_g"], p["w_in"], rope_c, rope_sa, rope_sb,
                                     p["hy_conv_w"], p["hy_conv_b"], seq_len)
    filters = _hyena_filters(seq_len, *p["filter"])
    yh = _hyena_long_conv(u, x0, filters, p["hy_skip"], batch, seq_len)
    ya = _attention(qt1, qt2, k, vt, p["lam_vecs"], p["subln_g"], batch, seq_len)
    x2, xnt, st = _outproj(yh, ya, xf, p["w_out"], p["norm2_g"], p["wqt"],
                          p["keys_hi"], p["keys_lo"])
    cnt, w0, r1, e1 = _route(st)
    y = _peer(xnt, p["peer_u"], p["peer_vt"], cnt, w0, r1, e1, x2, p["normf_g"])
    return y.reshape(batch, seq_len, D_MODEL)


def kernel(x_prompt, x_sample, norm1_g, w_in, hy_conv_w, hy_conv_b, hf_w1, hf_b1, hf_freq1,
           hf_w2, hf_b2, hf_freq2, hf_w3, hf_b3, hy_skip, lam_q1, lam_k1, lam_q2, lam_k2,
           subln_g, w_out, norm2_g, peer_wq, peer_keys, peer_u, peer_v, normf_g):
    l = 0
    keys = peer_keys[l].reshape(2 * PEER_HEADS, PEER_KEYS, PEER_HALF)
    keys_hi = keys.astype(BF16)
    p = {
        "norm1_g": norm1_g[l][None, :],
        "w_in": w_in[l].astype(BF16),
        "hy_conv_w": hy_conv_w[l],
        "hy_conv_b": hy_conv_b[l][None, :],
        "filter": (hf_w1[l], hf_b1[l], hf_freq1[l], hf_w2[l], hf_b2[l], hf_freq2[l],
                   hf_w3[l], hf_b3[l]),
        "hy_skip": hy_skip[l],
        "lam_vecs": jnp.stack([lam_q1[l], lam_k1[l], lam_q2[l], lam_k2[l]]),
        "subln_g": subln_g[l],
        "w_out": w_out[l].astype(BF16),
        "norm2_g": norm2_g[l][None, :],
        "wqt": peer_wq[l].astype(BF16).T,
        "keys_hi": keys_hi,
        "keys_lo": (keys - keys_hi.astype(F32)).astype(BF16),
        "peer_u": (peer_u[l] * GELU_IN_SCALE).astype(BF16),
        "peer_vt": peer_v[l].astype(BF16).T,
        "normf_g": normf_g[None, :],
    }
    return (_trunk(x_prompt, p), _trunk(x_sample, p))
```

```python
import functools
import math

import jax
import jax.numpy as jnp
import numpy as np
from jax import lax
from jax.experimental import pallas as pl
from jax.experimental.pallas import tpu as pltpu

F32 = jnp.float32
BF16 = jnp.bfloat16

D_MODEL = 1024
HY_WIDTH = 512
ATT_WIDTH = 512
N_ATT_HEADS = 4
ATT_HEAD_DIM = 64
ATT_V_DIM = 2 * ATT_HEAD_DIM
ROT_DIM = ATT_HEAD_DIM // 4
ROPE_THETA = 500000.0
IN_WIDTH = 3 * HY_WIDTH + 3 * ATT_WIDTH
FILTER_BANDS = 8
FILTER_EMB = 1 + 2 * FILTER_BANDS
FAST_DECAY_PCT = 0.3
SLOW_DECAY_PCT = 1.5
DECAY_TARGET = 1e-2
PEER_HEADS = 8
PEER_KEYS = 128
PEER_EXPERTS = PEER_KEYS * PEER_KEYS
PEER_TOPK = 16
PEER_HALF = 128
EPS = 1e-6
SUBLN_EPS = 1e-5
LAMBDA_INIT = 0.8 - 0.6 * math.exp(-0.3 * 0)

LANES = 128
SUBLANES = 8
VMEM_LIMIT = 56 * 1024 * 1024
FFT_N2 = 128
VT_ROWS = ATT_V_DIM + 2 * SUBLANES
STABILISER_HEADROOM = 64.0
NEG_INF = float("-inf")


def _params(semantics, vmem=VMEM_LIMIT):
    return pltpu.CompilerParams(dimension_semantics=semantics, vmem_limit_bytes=vmem)


def _rms(x, g, eps):
    return x * lax.rsqrt(jnp.mean(x * x, axis=-1, keepdims=True) + eps) * g


def _store_time_blocks(dst_ref, x, lead=()):
    rows, c = x.shape
    for j in range(rows // LANES):
        for cb in range(c // LANES):
            dst_ref[lead + (j, slice(cb * LANES, (cb + 1) * LANES), slice(None))] = (
                x[j * LANES:(j + 1) * LANES, cb * LANES:(cb + 1) * LANES].T)


def _inproj_kernel(x_ref, xp_ref, xn_ref, g_ref, w_ref, c_ref, sa_ref, sb_ref, cw_ref, cb_ref,
                   ut_ref, x0_ref, qt1_ref, qt2_ref, k_ref, vt_ref, *, blocks_per_seq):
    i = pl.program_id(0)
    tm = x_ref.shape[0]
    hy3 = 3 * HY_WIDTH
    x_all = jnp.concatenate([xp_ref[...], x_ref[...], xn_ref[...]], axis=0)
    h = _rms(x_all, g_ref[...], EPS).astype(BF16)
    z_all = jnp.dot(h, w_ref[...], preferred_element_type=F32)
    z = z_all[SUBLANES:SUBLANES + tm]

    zh = z[:, :hy3]
    row = lax.broadcasted_iota(jnp.int32, zh.shape, 0)
    at_start = (i % blocks_per_seq) == 0
    at_end = (i % blocks_per_seq) == blocks_per_seq - 1
    prev_row = jnp.where(at_start, 0.0, z_all[SUBLANES - 1:SUBLANES, :hy3])
    next_row = jnp.where(at_end, 0.0, z_all[SUBLANES + tm:SUBLANES + tm + 1, :hy3])
    zm = jnp.where(row == 0, prev_row, pltpu.roll(zh, 1, 0))
    zp = jnp.where(row == tm - 1, next_row, pltpu.roll(zh, tm - 1, 0))
    cw = cw_ref[...]
    conv = zm * cw[0:1] + zh * cw[1:2] + zp * cw[2:3] + cb_ref[...]
    x0_ref[...] = conv[:, :HY_WIDTH]
    _store_time_blocks(ut_ref, conv[:, 2 * HY_WIDTH:] * conv[:, HY_WIDTH:2 * HY_WIDTH])

    c, sa, sb = c_ref[...], sa_ref[...], sb_ref[...]
    scale = ATT_HEAD_DIM ** -0.5 * math.log2(math.e)

    def rope(xb):
        return (xb * c + pltpu.roll(xb, LANES - ROT_DIM // 2, 1) * sa
                + pltpu.roll(xb, ROT_DIM // 2, 1) * sb)

    first = lax.broadcasted_iota(jnp.int32, (LANES, tm), 0) < ATT_HEAD_DIM
    for hd in range(N_ATT_HEADS):
        lo, hi = hd * LANES, (hd + 1) * LANES
        qt = (rope(z[:, hy3 + lo:hy3 + hi]) * scale).T
        qt1_ref[lo:hi, :] = jnp.where(first, qt, 0.0).astype(BF16)
        qt2_ref[lo:hi, :] = jnp.where(first, 0.0, qt).astype(BF16)
        k_ref[:, lo:hi] = rope(z[:, hy3 + ATT_WIDTH + lo:hy3 + ATT_WIDTH + hi]).astype(BF16)
        vlo = hd * VT_ROWS
        vt_ref[vlo:vlo + ATT_V_DIM, :] = (
            z[:, hy3 + 2 * ATT_WIDTH + lo:hy3 + 2 * ATT_WIDTH + hi].T.astype(BF16))
        vt_ref[vlo + ATT_V_DIM:vlo + VT_ROWS, :] = jnp.ones((VT_ROWS - ATT_V_DIM, tm), BF16)


def _inproj(x, g, w_bf, rope_c, rope_sa, rope_sb, conv_w, conv_b, seq_len, tm=512):
    t = x.shape[0]
    nseq = seq_len // tm
    per8 = tm // SUBLANES
    last8 = t // SUBLANES - 1
    row = lambda i: (i, 0)
    col = lambda i: (0, i)
    fixed = lambda i: (0, 0)
    pos = lambda i: (i % nseq, 0)
    hy3 = 3 * HY_WIDTH
    tr_spec = pl.BlockSpec((ATT_WIDTH, tm), col)
    tr_shape = jax.ShapeDtypeStruct((ATT_WIDTH, t), BF16)
    return pl.pallas_call(
        functools.partial(_inproj_kernel, blocks_per_seq=nseq),
        grid=(t // tm,),
        in_specs=[pl.BlockSpec((tm, D_MODEL), row),
                  pl.BlockSpec((SUBLANES, D_MODEL), lambda i: (jnp.maximum(i * per8 - 1, 0), 0)),
                  pl.BlockSpec((SUBLANES, D_MODEL), lambda i: (jnp.minimum((i + 1) * per8, last8), 0)),
                  pl.BlockSpec((1, D_MODEL), fixed),
                  pl.BlockSpec((D_MODEL, IN_WIDTH), fixed),
                  pl.BlockSpec((tm, LANES), pos),
                  pl.BlockSpec((tm, LANES), pos),
                  pl.BlockSpec((tm, LANES), pos),
                  pl.BlockSpec((3, hy3), fixed),
                  pl.BlockSpec((1, hy3), fixed)],
        out_specs=[pl.BlockSpec((tm // LANES, HY_WIDTH, LANES), lambda i: (i, 0, 0)),
                   pl.BlockSpec((tm, HY_WIDTH), row), tr_spec, tr_spec,
                   pl.BlockSpec((tm, ATT_WIDTH), row),
                   pl.BlockSpec((N_ATT_HEADS * VT_ROWS, tm), col)],
        out_shape=[jax.ShapeDtypeStruct((t // LANES, HY_WIDTH, LANES), F32),
                   jax.ShapeDtypeStruct((t, HY_WIDTH), F32), tr_shape, tr_shape,
                   jax.ShapeDtypeStruct((t, ATT_WIDTH), BF16),
                   jax.ShapeDtypeStruct((N_ATT_HEADS * VT_ROWS, t), BF16)],
        compiler_params=_params(("parallel",)),
        name="inproj",
    )(x, x, x, g, w_bf, rope_c, rope_sa, rope_sb, conv_w, conv_b)


def _filter_kernel(f_ref, w1_ref, b1_ref, f1_ref, w2_ref, b2_ref, f2_ref, w3_ref, b3_ref,
                   dl_ref, h_ref):
    i = pl.program_id(0)
    hp = lax.Precision.HIGHEST
    feats = f_ref[...]
    tl = feats.shape[0]
    h = jnp.sin(f1_ref[...] * (jnp.dot(feats, w1_ref[...], precision=hp,
                                       preferred_element_type=F32) + b1_ref[...]))
    h = jnp.sin(f2_ref[...] * (jnp.dot(h, w2_ref[...], precision=hp,
                                       preferred_element_type=F32) + b2_ref[...]))
    h3 = jnp.dot(h, w3_ref[...], precision=hp, preferred_element_type=F32) + b3_ref[...]
    decay = jnp.exp(-feats[:, 0:1] * dl_ref[...])
    _store_time_blocks(h_ref, h3[:, :HY_WIDTH] * decay, lead=(0,))
    row = lax.broadcasted_iota(jnp.int32, (tl, HY_WIDTH), 0) + i * tl
    _store_time_blocks(h_ref, jnp.where(row == 0, 0.0, h3[:, HY_WIDTH:] * decay), lead=(1,))


def _pad2(a, rows, cols):
    return jnp.pad(a, ((0, rows - a.shape[0]), (0, cols - a.shape[1])))


def _hyena_filters(seq_len, w1, b1, f1, w2, b2, f2, w3, b3, tl=1024):
    f32 = np.float32
    t = np.linspace(0.0, 1.0, seq_len).astype(f32)[:, None]
    w = f32(2.0 * math.pi) * np.arange(seq_len, dtype=f32) / f32(seq_len)
    bands = np.linspace(1e-4, FILTER_BANDS - 1, FILTER_BANDS).astype(f32)
    ang = (w[:, None] * bands[None, :]).astype(np.float64)
    feats_np = np.zeros((seq_len, LANES), f32)
    feats_np[:, :FILTER_EMB] = np.concatenate([t, np.cos(ang), -np.sin(ang)], axis=-1)
    feats = jnp.asarray(feats_np)
    max_decay = math.log(DECAY_TARGET) / FAST_DECAY_PCT
    min_decay = math.log(DECAY_TARGET) / SLOW_DECAY_PCT
    deltas = jnp.abs(jnp.linspace(min_decay, max_decay, HY_WIDTH, dtype=F32))[None, :]
    w1p = _pad2(w1, LANES, LANES)
    w2p = _pad2(w2, LANES, LANES)
    w3p = _pad2(w3, LANES, 2 * HY_WIDTH)
    pad1 = lambda v: _pad2(v[None, :], 1, LANES)
    fixed = lambda i: (0, 0)
    vec = pl.BlockSpec((1, LANES), fixed)
    mat = pl.BlockSpec((LANES, LANES), fixed)
    return pl.pallas_call(
        _filter_kernel,
        grid=(seq_len // tl,),
        in_specs=[pl.BlockSpec((tl, LANES), lambda i: (i, 0)), mat, vec, vec, mat, vec, vec,
                  pl.BlockSpec((LANES, 2 * HY_WIDTH), fixed),
                  pl.BlockSpec((1, 2 * HY_WIDTH), fixed),
                  pl.BlockSpec((1, HY_WIDTH), fixed)],
        out_specs=pl.BlockSpec((2, tl // LANES, HY_WIDTH, LANES), lambda i: (0, i, 0, 0)),
        out_shape=jax.ShapeDtypeStruct((2, seq_len // LANES, HY_WIDTH, LANES), F32),
        compiler_params=_params(("parallel",)),
        name="hyena_filter",
    )(feats, w1p, pad1(b1), pad1(f1), w2p, pad1(b2), pad1(f2), w3p, b3[None, :], deltas)


def _dft_tables(seq_len):
    n = 2 * seq_len
    n2 = FFT_N2
    n1 = n // n2
    n1h = n1 // 2
    a1 = (2.0 * np.pi / n1) * np.outer(np.arange(n1), np.arange(n1))
    f1r, f1i = np.cos(a1), -np.sin(a1)
    fwd1 = np.concatenate([f1r[:, :n1h], f1i[:, :n1h]], axis=0)
    inv1 = np.concatenate([f1r[:n1h, :], f1i[:n1h, :]], axis=1) / n
    a2 = (2.0 * np.pi / n2) * np.outer(np.arange(n2), np.arange(n2))
    f2 = np.concatenate([np.cos(a2), -np.sin(a2)], axis=1)
    at = (2.0 * np.pi / n) * np.outer(np.arange(n1), np.arange(n2))
    return (jnp.asarray(fwd1, BF16), jnp.asarray(inv1, BF16), jnp.asarray(f2, BF16),
            jnp.asarray(np.cos(at), F32), jnp.asarray(-np.sin(at), F32))


def _forward_spectra(tiles, fwd1, twr, twi, f2):
    n1, w = twr.shape
    tr, ti = _tile_lanes(twr, 2 * w), _tile_lanes(twi, 2 * w)
    firsts = [jnp.dot(fwd1, jnp.concatenate([xa, xb], axis=1).astype(BF16),
                      preferred_element_type=F32) for xa, xb in tiles]
    twiddled = []
    for a in firsts:
        ar, ai = a[:n1], a[n1:]
        zr, zi = ar * tr - ai * ti, ar * ti + ai * tr
        twiddled.append(jnp.concatenate([zr[:, :w], zi[:, :w], zr[:, w:], zi[:, w:]],
                                        axis=0).astype(BF16))
    seconds = [jnp.dot(z, f2, preferred_element_type=F32) for z in twiddled]

    def spectrum(q):
        qr, qi = q[:n1], q[n1:]
        return qr[:, :w] - qi[:, w:], qr[:, w:] + qi[:, :w]

    return [(spectrum(p[:2 * n1]), spectrum(p[2 * n1:])) for p in seconds]


def _inverse_signals(spectra, inv1, twr, twi, f2):
    n1, w = twr.shape
    firsts = [jnp.dot(jnp.concatenate([wa[0], wa[1], wb[0], wb[1]], axis=0).astype(BF16), f2,
                      preferred_element_type=F32) for wa, wb in spectra]

    def untwiddle(r):
        rr, ri = r[:n1], r[n1:]
        vr, vi = rr[:, :w] + ri[:, w:], ri[:, :w] - rr[:, w:]
        return vr * twr + vi * twi, vi * twr - vr * twi

    stacked = []
    for q in firsts:
        (bra, bia), (brb, bib) = untwiddle(q[:2 * n1]), untwiddle(q[2 * n1:])
        stacked.append(jnp.concatenate([jnp.concatenate([bra, brb], axis=1),
                                        jnp.concatenate([bia, bib], axis=1)],
                                       axis=0).astype(BF16))
    outs = [jnp.dot(inv1, r, preferred_element_type=F32) for r in stacked]
    return [(y[:, :w], y[:, w:]) for y in outs]


def _tile_lanes(x, width):
    return jnp.concatenate([x] * (width // x.shape[1]), axis=1)


def _spectrum_kernel(h_ref, fwd1_ref, f2_ref, twr_ref, twi_ref, kr_ref, ki_ref):
    fwd1, f2, twr, twi = fwd1_ref[...], f2_ref[...], twr_ref[...], twi_ref[...]
    cs = range(h_ref.shape[2])
    out = _forward_spectra([(h_ref[0, :, c, :], h_ref[1, :, c, :]) for c in cs], fwd1, twr, twi, f2)
    for c, ((fr, fi), (gr, gi)) in zip(cs, out):
        kr_ref[c] = (fr + gr).astype(BF16)
        ki_ref[c] = (fi - gi).astype(BF16)


def _filter_spectrum(h3, tables, cb=8):
    fwd1, _, f2, twr, twi = tables
    _, n1h, c, w = h3.shape
    n1 = 2 * n1h
    const = lambda a: pl.BlockSpec(a.shape, lambda i: (0,) * a.ndim)
    o_spec = pl.BlockSpec((cb, n1, w), lambda i: (i, 0, 0))
    return pl.pallas_call(
        _spectrum_kernel,
        grid=(c // cb,),
        in_specs=[pl.BlockSpec((2, n1h, cb, w), lambda i: (0, 0, i, 0)),
                  const(fwd1), const(f2), const(twr), const(twi)],
        out_specs=[o_spec, o_spec],
        out_shape=[jax.ShapeDtypeStruct((c, n1, w), BF16)] * 2,
        compiler_params=_params(("parallel",)),
        name="filter_spectrum",
    )(h3, fwd1, f2, twr, twi)


def _longconv_kernel(u_ref, kr_ref, ki_ref, skip_ref, fwd1_ref, inv1_ref, f2_ref, twr_ref,
                     twi_ref, y_ref):
    fwd1, inv1, f2 = fwd1_ref[...], inv1_ref[...], f2_ref[...]
    twr, twi = twr_ref[...], twi_ref[...]

    def filtered(c, spec):
        kr, ki = kr_ref[c].astype(F32), ki_ref[c].astype(F32)
        return spec[0] * kr - spec[1] * ki, spec[0] * ki + spec[1] * kr

    cs = range(0, u_ref.shape[1], 2)
    xs = [(u_ref[:, c, :], u_ref[:, c + 1, :]) for c in cs]
    spectra = _forward_spectra(xs, fwd1, twr, twi, f2)
    prods = [(filtered(c, sa), filtered(c + 1, sb)) for c, (sa, sb) in zip(cs, spectra)]
    for c, (xa, xb), (ya, yb) in zip(cs, xs, _inverse_signals(prods, inv1, twr, twi, f2)):
        y_ref[c] = ya + skip_ref[c] * xa
        y_ref[c + 1] = yb + skip_ref[c + 1] * xb


def _longconv(ut3, kr, ki, skip3, tables, batch, cb=16):
    fwd1, inv1, f2, twr, twi = tables
    rows, c, w = ut3.shape
    n1h = rows // batch
    n1 = 2 * n1h
    const = lambda a: pl.BlockSpec(a.shape, lambda i, b: (0,) * a.ndim)
    spec = pl.BlockSpec((cb, n1, w), lambda i, b: (i, 0, 0))
    return pl.pallas_call(
        _longconv_kernel,
        grid=(c // cb, batch),
        in_specs=[pl.BlockSpec((n1h, cb, w), lambda i, b: (b, i, 0)), spec, spec,
                  pl.BlockSpec((cb, 1, w), lambda i, b: (i, 0, 0)),
                  const(fwd1), const(inv1), const(f2), const(twr), const(twi)],
        out_specs=pl.BlockSpec((cb, n1h, w), lambda i, b: (i, b, 0)),
        out_shape=jax.ShapeDtypeStruct((c, rows, w), F32),
        compiler_params=_params(("parallel", "parallel")),
        name="long_conv",
    )(ut3, kr, ki, skip3, fwd1, inv1, f2, twr, twi)


def _gate_kernel(y_ref, x0_ref, o_ref):
    rows, c = x0_ref.shape
    for j in range(rows // LANES):
        for cb in range(c // LANES):
            r, l = slice(j * LANES, (j + 1) * LANES), slice(cb * LANES, (cb + 1) * LANES)
            o_ref[r, l] = (x0_ref[r, l] * y_ref[l, j, :].T).astype(BF16)


def _gate(y3, x0, tl=1024):
    t, c = x0.shape
    return pl.pallas_call(
        _gate_kernel,
        grid=(t // tl,),
        in_specs=[pl.BlockSpec((c, tl // LANES, LANES), lambda i: (0, i, 0)),
                  pl.BlockSpec((tl, c), lambda i: (i, 0))],
        out_specs=pl.BlockSpec((tl, c), lambda i: (i, 0)),
        out_shape=jax.ShapeDtypeStruct((t, c), BF16),
        compiler_params=_params(("parallel",)),
        name="hyena_gate",
    )(y3, x0)


def _hyena_long_conv(ut3, x0, filters, skip, batch, seq_len):
    tables = _dft_tables(seq_len)
    kr, ki = _filter_spectrum(filters, tables)
    skip3 = jnp.broadcast_to(skip[:, None, None], (HY_WIDTH, 1, LANES))
    return _gate(_longconv(ut3, kr, ki, skip3, tables, batch), x0)


def _pipelined(units, scores, softmax, values):
    s_next = scores(units[0])
    pending = None
    for i, unit in enumerate(units):
        s_cur = s_next
        if i + 1 < len(units):
            s_next = scores(units[i + 1])
        if pending is not None:
            values(*pending)
        pending = (unit,) + softmax(unit, s_cur)
    values(*pending)


def _attn_kernel(q1_ref, q2_ref, k_ref, vt_ref, lam_ref, sg_ref, o_ref,
                 m1, a1, m2, a2, *, chunk):
    ki = pl.program_id(3)
    q_refs = (q1_ref, q2_ref)
    m_refs = (m1, m2)
    a_refs = (a1, a2)
    n_chunks = k_ref.shape[0] // chunk
    units = [(c, comp) for c in range(n_chunks) for comp in (0, 1)]

    def scores(unit):
        c, comp = unit
        return jnp.dot(k_ref[c * chunk:(c + 1) * chunk, :], q_refs[comp][...],
                       preferred_element_type=F32)

    def value_dot(unit, p):
        c = unit[0]
        return jnp.dot(vt_ref[:, c * chunk:(c + 1) * chunk], p, preferred_element_type=F32)

    @pl.when(ki == 0)
    def _():
        for comp in (0, 1):
            head = jnp.dot(k_ref[0:2 * SUBLANES, :], q_refs[comp][...], preferred_element_type=F32)
            m_refs[comp][...] = jnp.max(head, axis=0, keepdims=True)
            a_refs[comp][...] = jnp.zeros(a_refs[comp].shape, F32)

    m_start = [m1[...], m2[...]]
    step_max = [None, None]
    step_acc = [None, None]

    def fast_softmax(unit, s):
        comp = unit[1]
        cm = jnp.max(s, axis=0, keepdims=True)
        step_max[comp] = cm if step_max[comp] is None else jnp.maximum(step_max[comp], cm)
        return (jnp.exp2(s - m_start[comp]).astype(BF16),)

    def fast_values(unit, p):
        comp = unit[1]
        d = value_dot(unit, p)
        step_acc[comp] = d if step_acc[comp] is None else step_acc[comp] + d

    _pipelined(units, scores, fast_softmax, fast_values)
    excess = jnp.maximum(step_max[0] - m_start[0], step_max[1] - m_start[1])
    fast_ok = jnp.max(excess) <= STABILISER_HEADROOM

    @pl.when(fast_ok)
    def _():
        for comp in (0, 1):
            m_new = jnp.maximum(m_start[comp], step_max[comp])
            a_refs[comp][...] = jnp.exp2(m_start[comp] - m_new) * (a_refs[comp][...] + step_acc[comp])
            m_refs[comp][...] = m_new

    @pl.when(jnp.logical_not(fast_ok))
    def _():
        stats = [[m1[...], a1[...]], [m2[...], a2[...]]]

        def softmax(unit, s):
            m_prev = stats[unit[1]][0]
            m_new = jnp.maximum(m_prev, jnp.max(s, axis=0, keepdims=True))
            stats[unit[1]][0] = m_new
            return jnp.exp2(s - m_new).astype(BF16), jnp.exp2(m_prev - m_new)

        def values(unit, p, alpha):
            comp = unit[1]
            stats[comp][1] = alpha * stats[comp][1] + value_dot(unit, p)

        _pipelined(units, scores, softmax, values)
        for comp in (0, 1):
            m_refs[comp][...] = stats[comp][0]
            a_refs[comp][...] = stats[comp][1]

    @pl.when(ki == pl.num_programs(3) - 1)
    def _():
        lv = lam_ref[...]
        lam = (jnp.exp(jnp.sum(lv[0:1] * lv[1:2], axis=1, keepdims=True))
               - jnp.exp(jnp.sum(lv[2:3] * lv[3:4], axis=1, keepdims=True)) + LAMBDA_INIT)
        d = ATT_V_DIM
        o = (a1[:d, :] / a1[d:d + 1, :] - lam * (a2[:d, :] / a2[d:d + 1, :]))
        inv = lax.rsqrt(jnp.mean(o * o, axis=0, keepdims=True) + SUBLN_EPS)
        g = pltpu.repeat(sg_ref[...], o.shape[1] // LANES, 1)
        o_ref[...] = (o * inv * g * (1.0 - LAMBDA_INIT)).astype(BF16)


def _attention(qt1, qt2, k, vt, lam_vecs, subln_g, batch, seq_len, tq=1024, tk=4096, chunk=512):
    tk = min(tk, seq_len)
    nq, nk = seq_len // tq, seq_len // tk
    q_spec = pl.BlockSpec((LANES, tq), lambda b, h, i, j: (h, b * nq + i))
    sg = jnp.broadcast_to(subln_g[:, None], (ATT_V_DIM, LANES))
    return pl.pallas_call(
        functools.partial(_attn_kernel, chunk=chunk),
        grid=(batch, N_ATT_HEADS, nq, nk),
        in_specs=[q_spec, q_spec,
                  pl.BlockSpec((tk, LANES), lambda b, h, i, j: (b * nk + j, h)),
                  pl.BlockSpec((VT_ROWS, tk), lambda b, h, i, j: (h, b * nk + j)),
                  pl.BlockSpec((4, ATT_HEAD_DIM), lambda b, h, i, j: (0, 0)),
                  pl.BlockSpec((ATT_V_DIM, LANES), lambda b, h, i, j: (0, 0))],
        out_specs=q_spec,
        out_shape=jax.ShapeDtypeStruct((ATT_WIDTH, batch * seq_len), BF16),
        scratch_shapes=[pltpu.VMEM((1, tq), F32), pltpu.VMEM((VT_ROWS, tq), F32)] * 2,
        compiler_params=_params(("parallel", "parallel", "parallel", "arbitrary")),
        name="diff_attention",
    )(qt1, qt2, k, vt, lam_vecs, sg)


def _split_bf16(x):
    hi = x.astype(BF16)
    return hi, (x - hi.astype(F32)).astype(BF16)


def _outproj_kernel(yh_ref, ya_ref, x_ref, wo_ref, g_ref, wqt_ref, kh_ref, kl_ref,
                    x2_ref, xnt_ref, st_ref):
    wo = wo_ref[...]
    x2 = (x_ref[...] + jnp.dot(yh_ref[...], wo[:HY_WIDTH], preferred_element_type=F32)
          + lax.dot_general(ya_ref[...], wo[HY_WIDTH:], (((0,), (0,)), ((), ())),
                            preferred_element_type=F32))
    x2_ref[...] = x2
    xn = _rms(x2, g_ref[...], EPS)
    xnt = xn.T.astype(BF16)
    xnt_ref[...] = xnt
    qt = jnp.dot(wqt_ref[...], xnt, preferred_element_type=F32)
    for hc in range(2 * PEER_HEADS):
        qh, ql = _split_bf16(qt[hc * PEER_HALF:(hc + 1) * PEER_HALF, :])
        kh, kl = kh_ref[hc], kl_ref[hc]
        st_ref[hc * PEER_KEYS:(hc + 1) * PEER_KEYS, :] = (
            jnp.dot(kh, qh, preferred_element_type=F32)
            + jnp.dot(kh, ql, preferred_element_type=F32)
            + jnp.dot(kl, qh, preferred_element_type=F32))


def _outproj(yh, ya, x, wo_bf, g2, wqt_bf, keys_hi, keys_lo, tm=512):
    t = x.shape[0]
    qw = 2 * PEER_HEADS * PEER_HALF
    sw = 2 * PEER_HEADS * PEER_KEYS
    row = lambda i: (i, 0)
    fixed = lambda i: (0, 0)
    fixed3 = lambda i: (0, 0, 0)
    return pl.pallas_call(
        _outproj_kernel,
        grid=(t // tm,),
        in_specs=[pl.BlockSpec((tm, HY_WIDTH), row),
                  pl.BlockSpec((ATT_WIDTH, tm), lambda i: (0, i)),
                  pl.BlockSpec((tm, D_MODEL), row),
                  pl.BlockSpec((HY_WIDTH + ATT_WIDTH, D_MODEL), fixed),
                  pl.BlockSpec((1, D_MODEL), fixed),
                  pl.BlockSpec((qw, D_MODEL), fixed),
                  pl.BlockSpec((2 * PEER_HEADS, PEER_KEYS, PEER_HALF), fixed3),
                  pl.BlockSpec((2 * PEER_HEADS, PEER_KEYS, PEER_HALF), fixed3)],
        out_specs=[pl.BlockSpec((tm, D_MODEL), row), pl.BlockSpec((D_MODEL, tm), lambda i: (0, i)),
                   pl.BlockSpec((sw, tm), lambda i: (0, i))],
        out_shape=[jax.ShapeDtypeStruct((t, D_MODEL), F32),
                   jax.ShapeDtypeStruct((D_MODEL, t), BF16),
                   jax.ShapeDtypeStruct((sw, t), F32)],
        compiler_params=_params(("parallel",)),
        name="outproj_peer_query",
    )(yh, ya, x, wo_bf, g2, wqt_bf, keys_hi, keys_lo)


def _extract_max(pieces, iotas, total, exact):
    m = pieces[0]
    for p in pieces[1:]:
        m = jnp.maximum(m, p)
    m = jnp.max(m, axis=0, keepdims=True)
    hot = [p == m for p in pieces]
    if exact:
        cand = [jnp.where(h, io, total) for h, io in zip(hot, iotas)]
        idx = cand[0]
        for c in cand[1:]:
            idx = jnp.minimum(idx, c)
        idx = jnp.min(idx, axis=0, keepdims=True)
        hot = [io == idx for io in iotas]
    return m, hot, [jnp.where(h, NEG_INF, p) for h, p in zip(hot, pieces)]


def _topk_rank(s, exact, want_rank):
    rank = jnp.full(s.shape, float(PEER_TOPK), F32) if want_rank else None
    iota = [lax.broadcasted_iota(jnp.int32, s.shape, 0)]
    vals = []
    for r in range(PEER_TOPK):
        m, hot, (s,) = _extract_max([s], iota, s.shape[0], exact)
        vals.append(m)
        if want_rank:
            rank = jnp.where(hot[0], float(r), rank)
    return vals, rank


_PAIRS = [(k0, k1) for k0 in range(PEER_TOPK) for k1 in range(PEER_TOPK // (k0 + 1))]
_PAIR_ROWS = -(-len(_PAIRS) // SUBLANES) * SUBLANES


def _rows_to_tiles(rows, width):
    sub = lax.broadcasted_iota(jnp.int32, (SUBLANES, width), 0)
    tiles = []
    for t0 in range(0, len(rows), SUBLANES):
        tile = jnp.full((SUBLANES, width), NEG_INF, F32)
        for r in range(SUBLANES):
            if rows[t0 + r] is not None:
                tile = jnp.where(sub == r, rows[t0 + r], tile)
        tiles.append(tile)
    return tiles


def _route_tables(s0, s1, exact):
    tt = s0.shape[1]
    v0, rank0 = _topk_rank(s0, exact, want_rank=exact)
    v1, rank1 = _topk_rank(s1, exact, want_rank=True)
    pad = [None] * (_PAIR_ROWS - len(_PAIRS))
    tiles = [a + b for a, b in zip(_rows_to_tiles([v0[k0] for k0, _ in _PAIRS] + pad, tt),
                                   _rows_to_tiles([v1[k1] for _, k1 in _PAIRS] + pad, tt))]
    iotas = [lax.broadcasted_iota(jnp.int32, (SUBLANES, tt), 0) + t0
             for t0 in range(0, _PAIR_ROWS, SUBLANES)]
    cmax = v0[0] + v1[0]
    z = None
    for r in range(PEER_TOPK):
        m, _, tiles = _extract_max(tiles, iotas, _PAIR_ROWS, exact)
        z = jnp.exp(m - cmax) if z is None else z + jnp.exp(m - cmax)
    cnt = jnp.zeros(s0.shape, F32)
    total = None
    for k0 in range(PEER_TOPK):
        rows = [r for r, (a, _) in enumerate(_PAIRS) if a == k0]
        taken = None
        for ti in sorted({r // SUBLANES for r in rows}):
            inside = (iotas[ti] >= rows[0]) & (iotas[ti] <= rows[-1])
            part = jnp.sum(jnp.where(inside & (tiles[ti] == NEG_INF), 1.0, 0.0),
                           axis=0, keepdims=True)
            taken = part if taken is None else taken + part
        total = taken if total is None else total + taken
        cnt = jnp.where((rank0 == float(k0)) if exact else (s0 == v0[k0]), taken, cnt)
    k = float(PEER_TOPK)
    count = lambda mask: jnp.sum(jnp.where(mask, 1.0, 0.0), axis=0, keepdims=True)
    top0 = (rank0 < k) if exact else (s0 >= v0[PEER_TOPK - 1])
    ok = (total == k) & (count(top0) == k) & (count(rank1 < k) == k)
    return cnt, jnp.exp(s0 - v0[0]) / z, rank1, jnp.exp(s1 - v1[0]), ok


def _route_kernel(st_ref, cnt_ref, w0_ref, r1_ref, e1_ref):
    def emit(exact):
        cnt, w0, rank1, e1, ok = _route_tables(st_ref[:PEER_KEYS, :], st_ref[PEER_KEYS:, :], exact)
        cnt_ref[0] = cnt
        w0_ref[0] = w0 * GELU_OUT_SCALE
        r1_ref[0] = rank1.astype(BF16)
        e1_ref[0] = e1.astype(BF16)
        return ok

    ok = emit(exact=False)
    clean = jnp.min(jnp.where(ok, 1, 0))

    @pl.when(clean == 0)
    def _():
        emit(exact=True)


def _route(st, tt=512):
    sw, t = st.shape
    o_spec = pl.BlockSpec((1, PEER_KEYS, tt), lambda i, h: (h, 0, i))
    tab = lambda dt: jax.ShapeDtypeStruct((PEER_HEADS, PEER_KEYS, t), dt)
    return pl.pallas_call(
        _route_kernel,
        grid=(t // tt, PEER_HEADS),
        in_specs=[pl.BlockSpec((2 * PEER_KEYS, tt), lambda i, h: (h, i))],
        out_specs=[o_spec] * 4,
        out_shape=[tab(F32), tab(F32), tab(BF16), tab(BF16)],
        compiler_params=_params(("parallel", "parallel")),
        name="peer_route",
    )(st)


GELU_IN_SCALE = 2.0 ** -0.5
GELU_OUT_SCALE = 0.5 * 2.0 ** 0.5


def _gelu_core(y):
    return y * (1.0 + lax.erf(y))


def _peer_kernel(xnt_ref, u_ref, vt_ref, cnt_ref, w0_ref, r1_ref, e1_ref, x2_ref, gf_ref,
                 y_ref, acc_ref, *, rows_per_step, rows_per_unit):
    e = pl.program_id(1)

    @pl.when(e == 0)
    def _():
        acc_ref[...] = jnp.zeros(acc_ref.shape, F32)

    tt = xnt_ref.shape[1]
    unit = rows_per_unit * PEER_KEYS
    n_units = rows_per_step // rows_per_unit

    def hidden(s):
        return jnp.dot(u_ref[s * unit:(s + 1) * unit, :], xnt_ref[...],
                       preferred_element_type=F32)

    def activation(s, hid):
        gates = []
        for ii in range(s * rows_per_unit, (s + 1) * rows_per_unit):
            g = None
            for h in range(PEER_HEADS):
                cnt_row = jnp.broadcast_to(cnt_ref[h, ii:ii + 1, :], (PEER_KEYS, tt)).astype(BF16)
                w_row = jnp.broadcast_to(w0_ref[h, ii:ii + 1, :], (PEER_KEYS, tt)).astype(BF16)
                term = jnp.where(r1_ref[h] < cnt_row, e1_ref[h] * w_row, jnp.zeros((), BF16))
                g = term if g is None else g + term
            gates.append(g)
        return _gelu_core(hid).astype(BF16) * jnp.concatenate(gates, axis=0)

    def mix(s, act):
        return jnp.dot(vt_ref[:, s * unit:(s + 1) * unit], act, preferred_element_type=F32)

    acc = acc_ref[...]
    h_next = hidden(0)
    pending = None
    for s in range(n_units):
        h_cur = h_next
        if s + 1 < n_units:
            h_next = hidden(s + 1)
        if pending is not None:
            acc = acc + mix(*pending)
        pending = (s, activation(s, h_cur))
    acc_ref[...] = acc + mix(*pending)

    @pl.when(e == pl.num_programs(1) - 1)
    def _():
        x3 = x2_ref[...] + acc_ref[...].T
        y_ref[...] = _rms(x3, gf_ref[...], EPS)


def _peer(xnt, u_bf, vt_bf, cnt, w0, r1, e1, x2, gf, tt=512, rows_per_step=32, rows_per_unit=2):
    t = xnt.shape[1]
    te = rows_per_step * PEER_KEYS
    tok = lambda i, e: (i, 0)
    once = pl.Buffered(1)
    tab_i = pl.BlockSpec((PEER_HEADS, rows_per_step, tt), lambda i, e: (0, e, i))
    tab_j = pl.BlockSpec((PEER_HEADS, PEER_KEYS, tt), lambda i, e: (0, 0, i), pipeline_mode=once)
    return pl.pallas_call(
        functools.partial(_peer_kernel, rows_per_step=rows_per_step,
                          rows_per_unit=rows_per_unit),
        grid=(t // tt, PEER_EXPERTS // te),
        in_specs=[pl.BlockSpec((D_MODEL, tt), lambda i, e: (0, i), pipeline_mode=once),
                  pl.BlockSpec((te, D_MODEL), lambda i, e: (e, 0)),
                  pl.BlockSpec((D_MODEL, te), lambda i, e: (0, e)),
                  tab_i, tab_i, tab_j, tab_j,
                  pl.BlockSpec((tt, D_MODEL), tok, pipeline_mode=once),
                  pl.BlockSpec((1, D_MODEL), lambda i, e: (0, 0))],
        out_specs=pl.BlockSpec((tt, D_MODEL), tok),
        out_shape=jax.ShapeDtypeStruct((t, D_MODEL), F32),
        scratch_shapes=[pltpu.VMEM((D_MODEL, tt), F32)],
        compiler_params=_params(("parallel", "arbitrary")),
        name="peer_dense",
    )(xnt, u_bf, vt_bf, cnt, w0, r1, e1, x2, gf)


def _rope_tables(seq_len):
    f32 = np.float32
    inv = f32(ROPE_THETA) ** (-np.arange(0, ROT_DIM, 2, dtype=f32) / f32(ROT_DIM))
    ang = (np.arange(seq_len, dtype=f32)[:, None] * inv[None, :]).astype(np.float64)
    cos, sin = np.cos(ang), np.sin(ang)
    half = ROT_DIM // 2
    comp = np.zeros((3, seq_len, ATT_HEAD_DIM))
    comp[0, :, :half], comp[0, :, half:ROT_DIM], comp[0, :, ROT_DIM:] = cos, cos, 1.0
    comp[1, :, :half] = -sin
    comp[2, :, half:ROT_DIM] = sin
    both = np.concatenate([comp, comp], axis=2).astype(f32)
    return jnp.asarray(both[0]), jnp.asarray(both[1]), jnp.asarray(both[2])


def _trunk(x, p):
    batch, seq_len, _ = x.shape
    xf = x.reshape(batch * seq_len, D_MODEL)
    rope_c, rope_sa, rope_sb = _rope_tables(seq_len)
    u, x0, qt1, qt2, k, vt = _inproj(xf, p["norm1_g"], p["w_in"], rope_c, rope_sa, rope_sb,
                                     p["hy_conv_w"], p["hy_conv_b"], seq_len)
    filters = _hyena_filters(seq_len, *p["filter"])
    yh = _hyena_long_conv(u, x0, filters, p["hy_skip"], batch, seq_len)
    ya = _attention(qt1, qt2, k, vt, p["lam_vecs"], p["subln_g"], batch, seq_len)
    x2, xnt, st = _outproj(yh, ya, xf, p["w_out"], p["norm2_g"], p["wqt"],
                          p["keys_hi"], p["keys_lo"])
    cnt, w0, r1, e1 = _route(st)
    y = _peer(xnt, p["peer_u"], p["peer_vt"], cnt, w0, r1, e1, x2, p["normf_g"])
    return y.reshape(batch, seq_len, D_MODEL)


def kernel(x_prompt, x_sample, norm1_g, w_in, hy_conv_w, hy_conv_b, hf_w1, hf_b1, hf_freq1,
           hf_w2, hf_b2, hf_freq2, hf_w3, hf_b3, hy_skip, lam_q1, lam_k1, lam_q2, lam_k2,
           subln_g, w_out, norm2_g, peer_wq, peer_keys, peer_u, peer_v, normf_g):
    l = 0
    keys = peer_keys[l].reshape(2 * PEER_HEADS, PEER_KEYS, PEER_HALF)
    keys_hi = keys.astype(BF16)
    p = {
        "norm1_g": norm1_g[l][None, :],
        "w_in": w_in[l].astype(BF16),
        "hy_conv_w": hy_conv_w[l],
        "hy_conv_b": hy_conv_b[l][None, :],
        "filter": (hf_w1[l], hf_b1[l], hf_freq1[l], hf_w2[l], hf_b2[l], hf_freq2[l],
                   hf_w3[l], hf_b3[l]),
        "hy_skip": hy_skip[l],
        "lam_vecs": jnp.stack([lam_q1[l], lam_k1[l], lam_q2[l], lam_k2[l]]),
        "subln_g": subln_g[l],
        "w_out": w_out[l].astype(BF16),
        "norm2_g": norm2_g[l][None, :],
        "wqt": peer_wq[l].astype(BF16).T,
        "keys_hi": keys_hi,
        "keys_lo": (keys - keys_hi.astype(F32)).astype(BF16),
        "peer_u": (peer_u[l] * GELU_IN_SCALE).astype(BF16),
        "peer_vt": peer_v[l].astype(BF16).T,
        "normf_g": normf_g[None, :],
    }
    return (_trunk(x_prompt, p), _trunk(x_sample, p))
```
